```python
import math
import jax, jax.numpy as jnp
from jax import lax
import numpy as np

D_MODEL = 2048
BATCH = 8
SEQ = 2048
DEPTH = 2

HEAD_DIM = 128
N_HEADS_A = 4
N_HEADS_B = 4
N_HEADS_D = 4
DIFF_DIM = HEAD_DIM // 2
CONV_CH = 512
CONV_K = 3
IDX_HEADS = 4
IDX_DIM = 64
TOPK_MAX = 256
MIX_W = 512
N_BRANCH = 4
D_FF = 4 * D_MODEL
N_BUCKETS = 32
MAX_DIST = 128
Q_BLOCK = 128
EPS = 1e-6
NEG = -1e30
IDX_SCALE = (IDX_DIM ** -0.5) * (IDX_HEADS ** -0.5)

IN_SIZES = (
    N_HEADS_A * HEAD_DIM, N_HEADS_A * HEAD_DIM, N_HEADS_A * HEAD_DIM,
    IDX_HEADS * IDX_DIM, IDX_DIM, IDX_HEADS,
    N_HEADS_B * HEAD_DIM, N_HEADS_B * HEAD_DIM, N_HEADS_B * HEAD_DIM,
    CONV_CH, CONV_CH, CONV_CH,
    N_HEADS_D * HEAD_DIM, N_HEADS_D * HEAD_DIM, N_HEADS_D * HEAD_DIM,
    N_HEADS_D,
)
D_IN = sum(IN_SIZES)

kernel_name = "hybrid_gated_dsa_diff_conv_fox"


def _split_points():
    pts, acc = [], 0
    for s in IN_SIZES[:-1]:
        acc += s
        pts.append(acc)
    return pts


def rmsnorm(x, g):
    xf = x.astype(jnp.float32)
    y = xf * lax.rsqrt(jnp.mean(xf * xf, axis=-1, keepdims=True) + EPS)
    return (y * g.astype(jnp.float32)).astype(x.dtype)


def t5_bucket(rel):
    n = jnp.maximum(rel, 0)
    max_exact = N_BUCKETS // 2
    nf = jnp.maximum(n, 1).astype(jnp.float32)
    large = max_exact + (jnp.log(nf / max_exact) / math.log(MAX_DIST / max_exact)
                         * (N_BUCKETS - max_exact)).astype(jnp.int32)
    large = jnp.minimum(large, N_BUCKETS - 1)
    return jnp.where(n < max_exact, n, large)


def merge_blocks(o):
    nb, b, q, h, d = o.shape
    return jnp.moveaxis(o, 0, 1).reshape(b, nb * q, h, d)


def dsa_attention(q, k, v, iq, ik, iw, bias_tab, k_top):
    S = q.shape[1]
    hd = q.shape[-1]
    scale = hd ** -0.5
    kpos = jnp.arange(S)

    def block(i):
        t0 = i * Q_BLOCK
        qb = lax.dynamic_slice_in_dim(q, t0, Q_BLOCK, 1)
        iqb = lax.dynamic_slice_in_dim(iq, t0, Q_BLOCK, 1)
        iwb = lax.dynamic_slice_in_dim(iw, t0, Q_BLOCK, 1)
        qpos = t0 + jnp.arange(Q_BLOCK)
        causal = kpos[None, :] <= qpos[:, None]
        dots = jnp.einsum('bqhd,bsd->bqhs', iqb, ik).astype(jnp.float32)
        score = jnp.einsum('bqh,bqhs->bqs', iwb.astype(jnp.float32) * IDX_SCALE, jax.nn.relu(dots))
        score = jnp.where(causal[None], score, NEG)
        _, idx = lax.top_k(score, k_top)
        k_sel = jax.vmap(lambda kk, ii: kk[ii])(k, idx)
        v_sel = jax.vmap(lambda vv, ii: vv[ii])(v, idx)
        logits = jnp.einsum('bqhd,bqkhd->bhqk', qb, k_sel).astype(jnp.float32) * scale
        rel = qpos[None, :, None] - idx
        bias = bias_tab[t5_bucket(rel)].astype(jnp.float32)
        logits = logits + jnp.transpose(bias, (0, 3, 1, 2))
        logits = jnp.where((rel >= 0)[:, None], logits, NEG)
        p = jax.nn.softmax(logits, axis=-1).astype(v.dtype)
        return jnp.einsum('bhqk,bqkhd->bqhd', p, v_sel)

    return merge_blocks(lax.map(block, jnp.arange(S // Q_BLOCK)))


def diff_attention(q, k, v, lam_val, lam_init, g_subln, bias_tab):
    S = q.shape[1]
    scale = DIFF_DIM ** -0.5
    kpos = jnp.arange(S)

    def block(i):
        t0 = i * Q_BLOCK
        qb = lax.dynamic_slice_in_dim(q, t0, Q_BLOCK, 1)
        qpos = t0 + jnp.arange(Q_BLOCK)
        rel = qpos[:, None] - kpos[None, :]
        bias = jnp.transpose(bias_tab[t5_bucket(rel)], (2, 0, 1)).astype(jnp.float32)
        logits = jnp.einsum('bqhcd,bshcd->bchqs', qb, k).astype(jnp.float32) * scale + bias
        logits = jnp.where((rel >= 0)[None, None, None], logits, NEG)
        p = jax.nn.softmax(logits, axis=-1)
        attn = (p[:, 0] - lam_val * p[:, 1]).astype(v.dtype)
        return jnp.einsum('bhqs,bshd->bqhd', attn, v)

    o = merge_blocks(lax.map(block, jnp.arange(S // Q_BLOCK)))
    return rmsnorm(o, g_subln) * (1.0 - lam_init)


def short_conv(bg, cg, xin, conv_w):
    u = cg * xin
    y = lax.conv_general_dilated(
        u, conv_w[:, None, :], window_strides=(1,), padding=[(CONV_K - 1, 0)],
        dimension_numbers=('NWC', 'WIO', 'NWC'), feature_group_count=CONV_CH)
    return bg * y


def forgetting_attention(q, k, v, f_logit):
    S = q.shape[1]
    scale = q.shape[-1] ** -0.5
    kpos = jnp.arange(S)
    F = jnp.cumsum(jax.nn.log_sigmoid(f_logit.astype(jnp.float32)), axis=1)
    Fk = jnp.transpose(F, (0, 2, 1))

    def block(i):
        t0 = i * Q_BLOCK
        qb = lax.dynamic_slice_in_dim(q, t0, Q_BLOCK, 1)
        Fq = lax.dynamic_slice_in_dim(Fk, t0, Q_BLOCK, 2)
        qpos = t0 + jnp.arange(Q_BLOCK)
        causal = kpos[None, :] <= qpos[:, None]
        logits = (jnp.einsum('bqhd,bshd->bhqs', qb, k).astype(jnp.float32) * scale
                  + Fq[..., None] - Fk[:, :, None, :])
        logits = jnp.where(causal[None, None], logits, NEG)
        p = jax.nn.softmax(logits, axis=-1).astype(v.dtype)
        return jnp.einsum('bhqs,bshd->bqhd', p, v)

    return merge_blocks(lax.map(block, jnp.arange(S // Q_BLOCK)))


def setup_inputs(seed: int = 0) -> dict:
    key = jax.random.key(seed)
    ks = jax.random.split(key, 24)
    n = lambda k, shape, s: jax.random.normal(k, shape, jnp.float32) * s
    D = D_MODEL
    return {
        "x": n(ks[0], (BATCH, SEQ, D), 1.0),
        "c": n(ks[1], (BATCH, D), 1.0),
        "w_ada": n(ks[2], (DEPTH, D, 6 * D), 0.5 * D ** -0.5),
        "b_ada": n(ks[3], (DEPTH, 6 * D), 0.02),
        "g_norm1": 1.0 + n(ks[4], (DEPTH, D), 0.05),
        "w_in": n(ks[5], (DEPTH, D, D_IN), D ** -0.5),
        "b_f": 2.0 + n(ks[6], (DEPTH, N_HEADS_D), 0.1),
        "conv_w": n(ks[7], (DEPTH, CONV_K, CONV_CH), CONV_K ** -0.5),
        "lam": n(ks[8], (DEPTH, 4, DIFF_DIM), 0.1),
        "g_subln": 1.0 + n(ks[9], (DEPTH, HEAD_DIM), 0.05),
        "rel_bias": n(ks[10], (N_BUCKETS, N_HEADS_A + N_HEADS_B), 0.5),
        "w_gate": n(ks[11], (DEPTH, D, N_BRANCH * D), D ** -0.5),
        "b_gate": n(ks[12], (DEPTH, N_BRANCH * D), 0.02),
        "w_branch": n(ks[13], (DEPTH, N_BRANCH, MIX_W, D), MIX_W ** -0.5),
        "w_o": n(ks[14], (DEPTH, D, D), D ** -0.5),
        "g_norm2": 1.0 + n(ks[15], (DEPTH, D), 0.05),
        "w_ff1": n(ks[16], (DEPTH, D, D_FF), D ** -0.5),
        "w_ff2": n(ks[17], (DEPTH, D_FF, D), D_FF ** -0.5),
        "g_final": 1.0 + n(ks[18], (D,), 0.05),
    }


def reference(x, c, w_ada, b_ada, g_norm1, w_in, b_f, conv_w, lam, g_subln, rel_bias,
              w_gate, b_gate, w_branch, w_o, g_norm2, w_ff1, w_ff2, g_final):
    Bn, S, D = x.shape
    k_top = min(TOPK_MAX, S // 4)
    c_act = jax.nn.silu(c)
    bias_a = rel_bias[:, :N_HEADS_A]
    bias_b = rel_bias[:, N_HEADS_A:]
    heads = lambda t, nh: t.reshape(Bn, S, nh, -1)
    for l in range(DEPTH):
        mod = (c_act @ w_ada[l] + b_ada[l])[:, None, :]
        sh1, sc1, gt1, sh2, sc2, gt2 = jnp.split(mod, 6, axis=-1)
        h = rmsnorm(x, g_norm1[l]) * (1.0 + sc1) + sh1
        (qa, ka, va, iq, ik, iw, qb, kb, vb, bg, cg, xin, qd, kd, vd, fd) = jnp.split(
            h @ w_in[l], _split_points(), axis=-1)
        y_a = dsa_attention(heads(qa, N_HEADS_A), heads(ka, N_HEADS_A), heads(va, N_HEADS_A),
                            heads(iq, IDX_HEADS), ik, iw, bias_a, k_top)
        lam_init = 0.8 - 0.6 * math.exp(-0.3 * l)
        lq = lam[l].astype(jnp.float32)
        lam_val = jnp.exp(jnp.sum(lq[0] * lq[1])) - jnp.exp(jnp.sum(lq[2] * lq[3])) + lam_init
        y_b = diff_attention(qb.reshape(Bn, S, N_HEADS_B, 2, DIFF_DIM),
                             kb.reshape(Bn, S, N_HEADS_B, 2, DIFF_DIM),
                             heads(vb, N_HEADS_B), lam_val, lam_init, g_subln[l], bias_b)
        y_c = short_conv(bg, cg, xin, conv_w[l])
        y_d = forgetting_attention(heads(qd, N_HEADS_D), heads(kd, N_HEADS_D), heads(vd, N_HEADS_D),
                                   fd + b_f[l])
        branches = jnp.stack([y_a.reshape(Bn, S, MIX_W), y_b.reshape(Bn, S, MIX_W),
                              y_c, y_d.reshape(Bn, S, MIX_W)], axis=2)
        gates = jax.nn.sigmoid(h @ w_gate[l] + b_gate[l]).reshape(Bn, S, N_BRANCH, D)
        merged = jnp.einsum('bsgd,bsgd->bsd', gates,
                            jnp.einsum('bsgk,gkd->bsgd', branches, w_branch[l]))
        x = x + gt1 * (merged @ w_o[l])
        h2 = rmsnorm(x, g_norm2[l]) * (1.0 + sc2) + sh2
        x = x + gt2 * (jnp.square(jax.nn.relu(h2 @ w_ff1[l])) @ w_ff2[l])
    return rmsnorm(x, g_final)
```

```python
import functools
import math

import numpy as np
import jax
import jax.numpy as jnp
from jax import lax
from jax.experimental import pallas as pl
from jax.experimental.pallas import tpu as pltpu

HEAD_DIM = 128
N_HEADS = 4
DIFF_DIM = HEAD_DIM // 2
CONV_CH = 512
CONV_K = 3
IDX_HEADS = 4
IDX_DIM = 64
TOPK_MAX = 256
MIX_W = 512
N_BRANCH = 4
N_BUCKETS = 32
MAX_DIST = 128
EPS = 1e-6
NEG = -1e30
IDX_SCALE = (IDX_DIM ** -0.5) * (IDX_HEADS ** -0.5)
INT_MIN = -(2 ** 31)

LANE = 128
VMEM_LIMIT = 56 * 1024 * 1024

BLK_QA, BLK_KA, BLK_VA = 0, 4, 8
BLK_IQ = 12
BLK_IK = 14
BLK_QB, BLK_KB, BLK_VB = 15, 19, 23
BLK_QD, BLK_KD, BLK_VD = 27, 31, 35
N_MAIN = 40 * LANE
N_F32 = 3 * CONV_CH + LANE
BLK_SMALL = 3 * CONV_CH // LANE
COL_IW, COL_FD = 0, 4


def _cparams(sem):
    return pltpu.CompilerParams(dimension_semantics=sem, vmem_limit_bytes=VMEM_LIMIT)


def _tile(n, pref):
    t = min(n, pref)
    while n % t:
        t -= 1
    return t


def _mod_kernel(c_ref, w_ref, b_ref, o_ref):
    c = c_ref[...]
    ca = (c * jax.nn.sigmoid(c)).astype(jnp.bfloat16)
    w = w_ref[0].astype(jnp.bfloat16)
    o_ref[0] = jnp.dot(ca, w, preferred_element_type=jnp.float32) + b_ref[0]


def _modulation(c, w_ada, b_ada):
    L, D, N = w_ada.shape
    Bn = c.shape[0]
    tn = _tile(N, 1024)
    return pl.pallas_call(
        _mod_kernel,
        grid=(L, N // tn),
        in_specs=[
            pl.BlockSpec((Bn, D), lambda l, j: (0, 0)),
            pl.BlockSpec((1, D, tn), lambda l, j: (l, 0, j)),
            pl.BlockSpec((1, 1, tn), lambda l, j: (l, 0, j)),
        ],
        out_specs=pl.BlockSpec((1, Bn, tn), lambda l, j: (l, 0, j)),
        out_shape=jax.ShapeDtypeStruct((L, Bn, N), jnp.float32),
        name="ada_mod",
        compiler_params=_cparams(("parallel", "parallel")),
    )(c, w_ada, b_ada.reshape(L, 1, N))


def _norm_kernel(*refs, has_delta, modulate):
    it = iter(refs)
    x_ref = next(it)
    d_ref = next(it) if has_delta else None
    gate_ref = next(it) if has_delta else None
    g_ref = next(it)
    sc_ref = next(it) if modulate else None
    sh_ref = next(it) if modulate else None
    xo_ref = next(it) if has_delta else None
    h_ref = next(it)

    x = x_ref[...]
    if has_delta:
        x = x + gate_ref[0] * d_ref[...]
        xo_ref[...] = x
    y = x * lax.rsqrt(jnp.mean(x * x, axis=-1, keepdims=True) + EPS) * g_ref[...]
    if modulate:
        y = y * (1.0 + sc_ref[0]) + sh_ref[0]
    h_ref[...] = y.astype(h_ref.dtype)


def _norm(x, g, *, seq, delta=None, gate=None, sc=None, sh=None, out_dtype=jnp.bfloat16):
    T, D = x.shape
    tm = _tile(seq, 512)
    per_seq = seq // tm
    has_delta = delta is not None
    modulate = sc is not None
    row = pl.BlockSpec((tm, D), lambda i: (i, 0))
    per_b = pl.BlockSpec((1, 1, D), lambda i: (i // per_seq, 0, 0))
    args, specs = [x], [row]
    if has_delta:
        args += [delta, gate]
        specs += [row, per_b]
    args.append(g.reshape(1, D))
    specs.append(pl.BlockSpec((1, D), lambda i: (0, 0)))
    if modulate:
        args += [sc, sh]
        specs += [per_b, per_b]
    out_shape, out_specs = [], []
    if has_delta:
        out_shape.append(jax.ShapeDtypeStruct((T, D), jnp.float32))
        out_specs.append(row)
    out_shape.append(jax.ShapeDtypeStruct((T, D), out_dtype))
    out_specs.append(row)
    res = pl.pallas_call(
        functools.partial(_norm_kernel, has_delta=has_delta, modulate=modulate),
        grid=(T // tm,),
        in_specs=specs,
        out_specs=out_specs,
        out_shape=out_shape,
        name="norm",
        compiler_params=_cparams(("parallel",)),
    )(*args)
    if has_delta:
        return res[0], res[1]
    return None, res[0]


def _mm_kernel(a_ref, w_ref, o_ref):
    o_ref[...] = jnp.dot(a_ref[...], w_ref[...],
                         preferred_element_type=jnp.float32).astype(o_ref.dtype)


def _mm(a, w, out_dtype, tn_pref):
    M, K = a.shape
    N = w.shape[1]
    tm = _tile(M, 1024)
    tn = _tile(N, tn_pref)
    return pl.pallas_call(
        _mm_kernel,
        grid=(M // tm, N // tn),
        in_specs=[pl.BlockSpec((tm, K), lambda i, j: (i, 0)),
                  pl.BlockSpec((K, tn), lambda i, j: (0, j))],
        out_specs=pl.BlockSpec((tm, tn), lambda i, j: (i, j)),
        out_shape=jax.ShapeDtypeStruct((M, N), out_dtype),
        name="matmul",
        compiler_params=_cparams(("parallel", "parallel")),
    )(a, w)


def _cumsum_kernel(s_ref, b_ref, o_ref):
    S = s_ref.shape[1]
    x = s_ref[0] + b_ref[...]
    ls = jnp.minimum(x, 0.0) - jnp.log(1.0 + jnp.exp(-jnp.abs(x)))
    r = lax.broadcasted_iota(jnp.int32, (LANE, LANE), 0)
    c = lax.broadcasted_iota(jnp.int32, (LANE, LANE), 1)
    tri = jnp.where(r >= c, 1.0, 0.0).astype(jnp.float32)
    carry = jnp.zeros((1, LANE), jnp.float32)
    for j in range(S // LANE):
        blk = jnp.dot(tri, ls[j * LANE:(j + 1) * LANE, :], precision=lax.Precision.HIGHEST,
                      preferred_element_type=jnp.float32) + carry
        o_ref[0, j * LANE:(j + 1) * LANE, :] = blk
        carry = blk[LANE - 1:LANE, :]


def _forget_cumsum(small, b_pad):
    Bn, S, _ = small.shape
    return pl.pallas_call(
        _cumsum_kernel,
        grid=(Bn,),
        in_specs=[pl.BlockSpec((1, S, LANE), lambda b: (b, 0, BLK_SMALL)),
                  pl.BlockSpec((1, LANE), lambda b: (0, 0))],
        out_specs=pl.BlockSpec((1, S, LANE), lambda b: (b, 0, 0)),
        out_shape=jax.ShapeDtypeStruct((Bn, S, LANE), jnp.float32),
        name="forget_cumsum",
        compiler_params=_cparams(("parallel",)),
    )(small, b_pad)


def _select_kernel(iq_ref, ik_ref, sm_ref, mask_ref, key_ref, *, k_top):
    tq = iq_ref.shape[1]
    S = ik_ref.shape[1]
    i = pl.program_id(1)
    iq = iq_ref[0]
    ik = ik_ref[0][:, :IDX_DIM]
    iw = sm_ref[0][:, COL_IW:COL_IW + IDX_HEADS] * IDX_SCALE
    score = jnp.zeros((tq, S), jnp.float32)
    for h in range(IDX_HEADS):
        d = lax.dot_general(iq[:, h * IDX_DIM:(h + 1) * IDX_DIM], ik,
                            (((1,), (1,)), ((), ())), preferred_element_type=jnp.float32)
        score = score + jnp.where(d > 0.0, d * iw[:, h:h + 1], 0.0)
    qpos = i * tq + lax.broadcasted_iota(jnp.int32, (tq, S), 0)
    col = lax.broadcasted_iota(jnp.int32, (tq, S), 1)
    causal = col <= qpos
    score = jnp.where(causal, score, NEG)
    bits = lax.bitcast_convert_type(score, jnp.int32)
    key_ref[...] = bits ^ ((bits >> 31) & 0x7FFFFFFF)
    kf = jnp.float32(k_top)

    def count_ge(cand):
        return jnp.sum(jnp.where(key_ref[...] >= cand, 1.0, 0.0), axis=1, keepdims=True)

    thr = jnp.where(count_ge(jnp.zeros((tq, 1), jnp.int32)) >= kf, 0, INT_MIN).astype(jnp.int32)

    def value_bit(it, thr):
        cand = thr | jnp.left_shift(jnp.int32(1), 30 - it)
        return jnp.where(count_ge(cand) >= kf, cand, thr)

    thr = lax.fori_loop(0, 31, value_bit, thr)

    key = key_ref[...]
    above = key > thr
    tie = key == thr
    need = kf - jnp.sum(jnp.where(above, 1.0, 0.0), axis=1, keepdims=True)
    n_idx_bits = max(1, (S - 1).bit_length())

    def index_bit(it, last):
        cand = last | jnp.left_shift(jnp.int32(1), n_idx_bits - 1 - it)
        below = jnp.sum(jnp.where((key_ref[...] == thr) & (col < cand), 1.0, 0.0),
                        axis=1, keepdims=True)
        return jnp.where(below < need, cand, last)

    last = lax.fori_loop(0, n_idx_bits, index_bit, jnp.zeros((tq, 1), jnp.int32))
    sel = (above | (tie & (col <= last))) & causal
    mask_ref[0] = jnp.where(sel, 1.0, 0.0).astype(mask_ref.dtype)


def _select(main, f32grp, k_top, tq):
    Bn, S, _ = main.shape
    return pl.pallas_call(
        functools.partial(_select_kernel, k_top=k_top),
        grid=(Bn, S // tq),
        in_specs=[pl.BlockSpec((1, tq, 2 * LANE), lambda b, i: (b, i, BLK_IQ // 2)),
                  pl.BlockSpec((1, S, LANE), lambda b, i: (b, 0, BLK_IK)),
                  pl.BlockSpec((1, tq, LANE), lambda b, i: (b, i, BLK_SMALL))],
        out_specs=pl.BlockSpec((1, tq, S), lambda b, i: (b, i, 0)),
        out_shape=jax.ShapeDtypeStruct((Bn, S, S), jnp.bfloat16),
        scratch_shapes=[pltpu.VMEM((tq, S), jnp.int32)],
        name="dsa_select",
        compiler_params=_cparams(("parallel", "parallel")),
    )(main, main, f32grp)


def _t5_thresholds():
    max_exact = N_BUCKETS // 2
    rel = np.arange(MAX_DIST)
    nf = np.maximum(rel, 1).astype(np.float64)
    large = max_exact + (np.log(nf / max_exact) / math.log(MAX_DIST / max_exact)
                         * (N_BUCKETS - max_exact)).astype(np.int64)
    bucket = np.where(rel < max_exact, rel, np.minimum(large, N_BUCKETS - 1))
    out = []
    for b in range(1, N_BUCKETS):
        hit = np.nonzero(bucket >= b)[0]
        out.append(int(hit[0]) if hit.size else MAX_DIST)
    return out


def _attn_kernel(*refs, mode, head_off, lam_init):
    it = iter(refs)
    q_ref, k_ref, v_ref = next(it), next(it), next(it)
    if mode == "dsa":
        mask_ref = next(it)
    if mode in ("dsa", "diff"):
        tab_ref = next(it)
    if mode == "diff":
        lam_ref, gs_ref = next(it), next(it)
    if mode == "fox":
        fc_ref, fr_ref = next(it), next(it)
    o_ref = next(it)
    m_ref, l_ref, acc_ref = next(it), next(it), next(it)
    if mode in ("dsa", "diff"):
        band_ref = next(it)

    h = pl.program_id(1)
    i = pl.program_id(2)
    tq = q_ref.shape[1]
    nmap = 2 if mode == "diff" else 1
    dq = HEAD_DIM // nmap
    scale = dq ** -0.5
    biased = mode in ("dsa", "diff")

    if biased:
        @pl.when(i == 0)
        def _():
            r = lax.broadcasted_iota(jnp.int32, (tq, tq), 0)
            c = lax.broadcasted_iota(jnp.int32, (tq, tq), 1)
            for t in range(2):
                rel = r - c + t * tq
                val = jnp.full((tq, tq), tab_ref[0, head_off + h], jnp.float32)
                for b, first in enumerate(_t5_thresholds(), start=1):
                    val = jnp.where(rel >= first, tab_ref[b, head_off + h], val)
                band_ref[t] = val
        far_bias = tab_ref[N_BUCKETS - 1, head_off + h]

    m_ref[...] = jnp.full(m_ref.shape, NEG, jnp.float32)
    l_ref[...] = jnp.zeros(l_ref.shape, jnp.float32)
    acc_ref[...] = jnp.zeros(acc_ref.shape, jnp.float32)

    q = q_ref[0]
    if mode == "fox":
        lane = lax.broadcasted_iota(jnp.int32, (tq, LANE), 1)
        fq = jnp.sum(jnp.where(lane == COL_FD + h, fc_ref[0], 0.0), axis=1, keepdims=True)

    def chunk(off, kind):
        kc = k_ref[0, pl.ds(off, tq), :]
        vc = v_ref[0, pl.ds(off, tq), :]
        if mode == "dsa":
            keep = mask_ref[0, :, pl.ds(off, tq)].astype(jnp.float32) > 0.5
        elif kind == "diag":
            keep = (lax.broadcasted_iota(jnp.int32, (tq, tq), 0)
                    >= lax.broadcasted_iota(jnp.int32, (tq, tq), 1))
        else:
            keep = None
        for mp in range(nmap):
            s = lax.dot_general(q[:, mp * dq:(mp + 1) * dq], kc[:, mp * dq:(mp + 1) * dq],
                                (((1,), (1,)), ((), ())), preferred_element_type=jnp.float32) * scale
            if biased:
                if kind == "far":
                    s = s + far_bias
                else:
                    s = s + band_ref[1 if kind == "prev" else 0]
            else:
                s = s + fq - fr_ref[0, 0, :, pl.ds(off, tq)]
            if keep is not None:
                s = jnp.where(keep, s, NEG)
            m_old = m_ref[mp]
            m_new = jnp.maximum(m_old, jnp.max(s, axis=1, keepdims=True))
            alpha = jnp.exp(m_old - m_new)
            p = jnp.exp(s - m_new)
            l_ref[mp] = alpha * l_ref[mp] + jnp.sum(p, axis=1, keepdims=True)
            acc_ref[mp] = alpha * acc_ref[mp] + jnp.dot(p.astype(vc.dtype), vc,
                                                       preferred_element_type=jnp.float32)
            m_ref[mp] = m_new

    n_far = jnp.maximum(i - 1, 0) if biased else i

    def far_body(c, carry):
        chunk(pl.multiple_of(c * tq, tq), "far")
        return carry

    lax.fori_loop(0, n_far, far_body, 0)
    if biased:
        @pl.when(i >= 1)
        def _():
            chunk(pl.multiple_of((i - 1) * tq, tq), "prev")
    chunk(pl.multiple_of(i * tq, tq), "diag")

    o = acc_ref[0] / l_ref[0]
    if mode == "diff":
        lq = lam_ref[...]
        lam_val = (jnp.exp(jnp.sum(lq[0:1] * lq[1:2], axis=1, keepdims=True))
                   - jnp.exp(jnp.sum(lq[2:3] * lq[3:4], axis=1, keepdims=True)) + lam_init)
        o = o - lam_val * (acc_ref[1] / l_ref[1])
        o = o * lax.rsqrt(jnp.mean(o * o, axis=-1, keepdims=True) + EPS) * gs_ref[...]
        o = o * (1.0 - lam_init)
    o_ref[0] = o.astype(o_ref.dtype)


def _attention(main, mode, tq, *, mask=None, tab=None, lam=None, g_subln=None,
               fcol=None, frow=None, lam_init=0.0):
    Bn, S, _ = main.shape
    blk_q, blk_k, blk_v, head_off = {
        "dsa": (BLK_QA, BLK_KA, BLK_VA, 0),
        "diff": (BLK_QB, BLK_KB, BLK_VB, N_HEADS),
        "fox": (BLK_QD, BLK_KD, BLK_VD, 0),
    }[mode]
    nmap = 2 if mode == "diff" else 1
    args = [main, main, main]
    specs = [pl.BlockSpec((1, tq, HEAD_DIM), lambda b, h, i: (b, i, blk_q + h)),
             pl.BlockSpec((1, S, HEAD_DIM), lambda b, h, i: (b, 0, blk_k + h)),
             pl.BlockSpec((1, S, HEAD_DIM), lambda b, h, i: (b, 0, blk_v + h))]
    scratch = [pltpu.VMEM((nmap, tq, 1), jnp.float32),
               pltpu.VMEM((nmap, tq, 1), jnp.float32),
               pltpu.VMEM((nmap, tq, HEAD_DIM), jnp.float32)]
    if mode == "dsa":
        args.append(mask)
        specs.append(pl.BlockSpec((1, tq, S), lambda b, h, i: (b, i, 0)))
    if mode in ("dsa", "diff"):
        args.append(tab)
        specs.append(pl.BlockSpec(memory_space=pltpu.SMEM))
        scratch.append(pltpu.VMEM((2, tq, tq), jnp.float32))
    if mode == "diff":
        args += [lam, g_subln.reshape(1, HEAD_DIM)]
        specs += [pl.BlockSpec((4, DIFF_DIM), lambda b, h, i: (0, 0)),
                  pl.BlockSpec((1, HEAD_DIM), lambda b, h, i: (0, 0))]
    if mode == "fox":
        args += [fcol, frow]
        specs += [pl.BlockSpec((1, tq, LANE), lambda b, h, i: (b, i, 0)),
                  pl.BlockSpec((1, 1, 1, S), lambda b, h, i: (b, h, 0, 0))]
    return pl.pallas_call(
        functools.partial(_attn_kernel, mode=mode, head_off=head_off, lam_init=lam_init),
        grid=(Bn, N_HEADS, S // tq),
        in_specs=specs,
        out_specs=pl.BlockSpec((1, tq, HEAD_DIM), lambda b, h, i: (b, i, h)),
        out_shape=jax.ShapeDtypeStruct((Bn, S, MIX_W), jnp.bfloat16),
        scratch_shapes=scratch,
        name="attn_" + mode,
        compiler_params=_cparams(("parallel", "parallel", "arbitrary")),
    )(*args)


def _conv_kernel(bg_ref, cg_ref, xin_ref, w_ref, o_ref):
    u = cg_ref[0] * xin_ref[0]
    row = lax.broadcasted_iota(jnp.int32, u.shape, 0)
    w = w_ref[...]
    y = u * w[CONV_K - 1:CONV_K, :]
    for d in range(1, CONV_K):
        shifted = jnp.where(row >= d, pltpu.roll(u, d, 0), 0.0)
        y = y + shifted * w[CONV_K - 1 - d:CONV_K - d, :]
    o_ref[0] = (bg_ref[0] * y).astype(o_ref.dtype)


def _short_conv(f32grp, conv_w):
    Bn, S, _ = f32grp.shape
    tc = LANE
    per = CONV_CH // tc
    return pl.pallas_call(
        _conv_kernel,
        grid=(Bn, per),
        in_specs=[pl.BlockSpec((1, S, tc), lambda b, j: (b, 0, j)),
                  pl.BlockSpec((1, S, tc), lambda b, j: (b, 0, per + j)),
                  pl.BlockSpec((1, S, tc), lambda b, j: (b, 0, 2 * per + j)),
                  pl.BlockSpec((CONV_K, tc), lambda b, j: (0, j))],
        out_specs=pl.BlockSpec((1, S, tc), lambda b, j: (b, 0, j)),
        out_shape=jax.ShapeDtypeStruct((Bn, S, CONV_CH), jnp.bfloat16),
        name="short_conv",
        compiler_params=_cparams(("parallel", "parallel")),
    )(f32grp, f32grp, f32grp, conv_w)


def _mix_kernel(*refs):
    h_ref = refs[0]
    y_refs = refs[1:1 + N_BRANCH]
    wg_refs = refs[1 + N_BRANCH:1 + 2 * N_BRANCH]
    bg_refs = refs[1 + 2 * N_BRANCH:1 + 3 * N_BRANCH]
    wb_refs = refs[1 + 3 * N_BRANCH:1 + 4 * N_BRANCH]
    o_ref = refs[1 + 4 * N_BRANCH]
    h = h_ref[...]
    merged = None
    for g in range(N_BRANCH):
        gate = jax.nn.sigmoid(jnp.dot(h, wg_refs[g][...], preferred_element_type=jnp.float32)
                              + bg_refs[g][...])
        proj = jnp.dot(y_refs[g][...], wb_refs[g][0], preferred_element_type=jnp.float32)
        merged = gate * proj if merged is None else merged + gate * proj
    o_ref[...] = merged.astype(o_ref.dtype)


def _mix(h, ys, w_gate, b_gate, w_branch):
    T, D = h.shape
    tm = _tile(T, 1024)
    tn = _tile(D, 512)
    per = D // tn
    specs = [pl.BlockSpec((tm, D), lambda i, n: (i, 0))]
    specs += [pl.BlockSpec((tm, MIX_W), lambda i, n: (i, 0)) for _ in range(N_BRANCH)]
    specs += [pl.BlockSpec((D, tn), lambda i, n, g=g: (0, g * per + n)) for g in range(N_BRANCH)]
    specs += [pl.BlockSpec((1, tn), lambda i, n, g=g: (0, g * per + n)) for g in range(N_BRANCH)]
    specs += [pl.BlockSpec((1, MIX_W, tn), lambda i, n, g=g: (g, 0, n)) for g in range(N_BRANCH)]
    return pl.pallas_call(
        _mix_kernel,
        grid=(T // tm, per),
        in_specs=specs,
        out_specs=pl.BlockSpec((tm, tn), lambda i, n: (i, n)),
        out_shape=jax.ShapeDtypeStruct((T, D), jnp.bfloat16),
        name="gated_mix",
        compiler_params=_cparams(("parallel", "parallel")),
    )(h, *ys, *([w_gate] * N_BRANCH), *([b_gate] * N_BRANCH), *([w_branch] * N_BRANCH))


def _ffn_kernel(h_ref, w1_ref, w2_ref, o_ref):
    f = pl.program_id(1)
    u = jnp.dot(h_ref[...], w1_ref[...], preferred_element_type=jnp.float32)
    u = jnp.square(jnp.maximum(u, 0.0)).astype(jnp.bfloat16)
    part = jnp.dot(u, w2_ref[...], preferred_element_type=jnp.float32)

    @pl.when(f == 0)
    def _():
        o_ref[...] = part

    @pl.when(f > 0)
    def _():
        o_ref[...] += part


def _ffn(h, w1, w2):
    T, D = h.shape
    F = w1.shape[1]
    tm = _tile(T, 1024)
    tf = _tile(F, 512)
    return pl.pallas_call(
        _ffn_kernel,
        grid=(T // tm, F // tf),
        in_specs=[pl.BlockSpec((tm, D), lambda i, f: (i, 0)),
                  pl.BlockSpec((D, tf), lambda i, f: (0, f)),
                  pl.BlockSpec((tf, D), lambda i, f: (f, 0))],
        out_specs=pl.BlockSpec((tm, D), lambda i, f: (i, 0)),
        out_shape=jax.ShapeDtypeStruct((T, D), jnp.float32),
        name="ffn",
        compiler_params=_cparams(("parallel", "arbitrary")),
    )(h, w1, w2)


def _split_in_proj(w_in):
    sizes = (512, 512, 512, IDX_HEADS * IDX_DIM, IDX_DIM, IDX_HEADS,
             512, 512, 512, CONV_CH, CONV_CH, CONV_CH, 512, 512, 512, N_HEADS)
    offs = np.concatenate([[0], np.cumsum(sizes)])
    part = lambda j: w_in[:, :, offs[j]:offs[j + 1]]
    (qa, ka, va, iq, ik, iw, qb, kb, vb, bg, cg, xin, qd, kd, vd, fd) = [part(j) for j in range(16)]
    L, D, _ = w_in.shape
    zeros = lambda n: jnp.zeros((L, D, n), w_in.dtype)
    main = jnp.concatenate([qa, ka, va, iq, ik, zeros(LANE - IDX_DIM), qb, kb, vb, qd, kd, vd,
                            zeros(LANE)], axis=-1)
    f32grp = jnp.concatenate([bg, cg, xin, iw, fd, zeros(LANE - IDX_HEADS - N_HEADS)], axis=-1)
    return main.astype(jnp.bfloat16), f32grp.astype(jnp.bfloat16)


def kernel(x, c, w_ada, b_ada, g_norm1, w_in, b_f, conv_w, lam, g_subln, rel_bias,
           w_gate, b_gate, w_branch, w_o, g_norm2, w_ff1, w_ff2, g_final):
    Bn, S, D = x.shape
    L = w_ada.shape[0]
    T = Bn * S
    k_top = min(TOPK_MAX, S // 4)
    tq = _tile(S, 256)
    bf = jnp.bfloat16

    w_main, w_f32 = _split_in_proj(w_in)
    w_gate_b, w_branch_b, w_o_b = w_gate.astype(bf), w_branch.astype(bf), w_o.astype(bf)
    w_ff1_b, w_ff2_b = w_ff1.astype(bf), w_ff2.astype(bf)

    mod = _modulation(c, w_ada, b_ada).reshape(L, Bn, 6, 1, D)
    xt = x.reshape(T, D)
    delta, gate = None, None
    for l in range(L):
        sh1, sc1, gt1, sh2, sc2, gt2 = [mod[l, :, j] for j in range(6)]
        x_new, h = _norm(xt, g_norm1[l], seq=S, delta=delta, gate=gate, sc=sc1, sh=sh1)
        xt = xt if x_new is None else x_new
        main = _mm(h, w_main[l], bf, 1024).reshape(Bn, S, N_MAIN)
        f32grp = _mm(h, w_f32[l], jnp.float32, N_F32).reshape(Bn, S, N_F32)

        b_pad = jnp.zeros((1, LANE), jnp.float32).at[0, COL_FD:COL_FD + N_HEADS].set(b_f[l])
        fcol = _forget_cumsum(f32grp, b_pad)
        frow = jnp.transpose(fcol[:, :, COL_FD:COL_FD + N_HEADS], (0, 2, 1)).reshape(Bn, N_HEADS, 1, S)

        mask = _select(main, f32grp, k_top, tq)
        lam_init = 0.8 - 0.6 * math.exp(-0.3 * l)
        y_a = _attention(main, "dsa", tq, mask=mask, tab=rel_bias)
        y_b = _attention(main, "diff", tq, tab=rel_bias, lam=lam[l], g_subln=g_subln[l],
                         lam_init=lam_init)
        y_c = _short_conv(f32grp, conv_w[l])
        y_d = _attention(main, "fox", tq, fcol=fcol, frow=frow)
        ys = [y.reshape(T, MIX_W) for y in (y_a, y_b, y_c, y_d)]

        merged = _mix(h, ys, w_gate_b[l], b_gate[l].reshape(1, N_BRANCH * D), w_branch_b[l])
        delta = _mm(merged, w_o_b[l], jnp.float32, 1024)
        xt, h2 = _norm(xt, g_norm2[l], seq=S, delta=delta, gate=gt1, sc=sc2, sh=sh2)
        delta = _ffn(h2, w_ff1_b[l], w_ff2_b[l])
        gate = gt2
    _, out = _norm(xt, g_final, seq=S, delta=delta, gate=gate, out_dtype=jnp.float32)
    return out.reshape(Bn, S, D)
```

```python
import functools
import math

import numpy as np
import jax
import jax.numpy as jnp
from jax import lax
from jax.experimental import pallas as pl
from jax.experimental.pallas import tpu as pltpu

HEAD_DIM = 128
N_HEADS = 4
DIFF_DIM = HEAD_DIM // 2
CONV_CH = 512
CONV_K = 3
IDX_HEADS = 4
IDX_DIM = 64
TOPK_MAX = 256
MIX_W = 512
N_BRANCH = 4
N_BUCKETS = 32
MAX_DIST = 128
EPS = 1e-6
NEG = -1e30
IDX_SCALE = (IDX_DIM ** -0.5) * (IDX_HEADS ** -0.5)
INT_MIN = -(2 ** 31)

LANE = 128
VMEM_LIMIT = 56 * 1024 * 1024

BLK_QA, BLK_KA, BLK_VA = 0, 4, 8
BLK_QB, BLK_KB, BLK_VB = 12, 16, 20
BLK_QD, BLK_KD, BLK_VD = 24, 28, 32
BLK_IQ = 36
BLK_IK = 38
N_MAIN = 40 * LANE
N_F32 = 3 * CONV_CH + LANE
BLK_SMALL = 3 * CONV_CH // LANE
COL_IW, COL_FD = 0, 4


def _cparams(sem):
    return pltpu.CompilerParams(dimension_semantics=sem, vmem_limit_bytes=VMEM_LIMIT)


def _tile(n, pref):
    t = min(n, pref)
    while n % t:
        t -= 1
    return t


def _mod_kernel(c_ref, w_ref, b_ref, o_ref):
    c = c_ref[...]
    ca = (c * jax.nn.sigmoid(c)).astype(jnp.bfloat16)
    w = w_ref[0].astype(jnp.bfloat16)
    o_ref[0] = jnp.dot(ca, w, preferred_element_type=jnp.float32) + b_ref[0]


def _modulation(c, w_ada, b_ada):
    L, D, N = w_ada.shape
    Bn = c.shape[0]
    tn = _tile(N, 1024)
    return pl.pallas_call(
        _mod_kernel,
        grid=(L, N // tn),
        in_specs=[
            pl.BlockSpec((Bn, D), lambda l, j: (0, 0)),
            pl.BlockSpec((1, D, tn), lambda l, j: (l, 0, j)),
            pl.BlockSpec((1, 1, tn), lambda l, j: (l, 0, j)),
        ],
        out_specs=pl.BlockSpec((1, Bn, tn), lambda l, j: (l, 0, j)),
        out_shape=jax.ShapeDtypeStruct((L, Bn, N), jnp.float32),
        name="ada_mod",
        compiler_params=_cparams(("parallel", "parallel")),
    )(c, w_ada, b_ada.reshape(L, 1, N))


def _norm_kernel(*refs, has_delta, modulate):
    it = iter(refs)
    x_ref = next(it)
    d_ref = next(it) if has_delta else None
    gate_ref = next(it) if has_delta else None
    g_ref = next(it)
    sc_ref = next(it) if modulate else None
    sh_ref = next(it) if modulate else None
    xo_ref = next(it) if has_delta else None
    h_ref = next(it)

    x = x_ref[...]
    if has_delta:
        x = x + gate_ref[0] * d_ref[...]
        xo_ref[...] = x
    y = x * lax.rsqrt(jnp.mean(x * x, axis=-1, keepdims=True) + EPS) * g_ref[...]
    if modulate:
        y = y * (1.0 + sc_ref[0]) + sh_ref[0]
    h_ref[...] = y.astype(h_ref.dtype)


def _norm(x, g, *, seq, delta=None, gate=None, sc=None, sh=None, out_dtype=jnp.bfloat16):
    T, D = x.shape
    tm = _tile(seq, 512)
    per_seq = seq // tm
    has_delta = delta is not None
    modulate = sc is not None
    row = pl.BlockSpec((tm, D), lambda i: (i, 0))
    per_b = pl.BlockSpec((1, 1, D), lambda i: (i // per_seq, 0, 0))
    args, specs = [x], [row]
    if has_delta:
        args += [delta, gate]
        specs += [row, per_b]
    args.append(g.reshape(1, D))
    specs.append(pl.BlockSpec((1, D), lambda i: (0, 0)))
    if modulate:
        args += [sc, sh]
        specs += [per_b, per_b]
    out_shape, out_specs = [], []
    if has_delta:
        out_shape.append(jax.ShapeDtypeStruct((T, D), jnp.float32))
        out_specs.append(row)
    out_shape.append(jax.ShapeDtypeStruct((T, D), out_dtype))
    out_specs.append(row)
    res = pl.pallas_call(
        functools.partial(_norm_kernel, has_delta=has_delta, modulate=modulate),
        grid=(T // tm,),
        in_specs=specs,
        out_specs=out_specs,
        out_shape=out_shape,
        name="norm",
        compiler_params=_cparams(("parallel",)),
    )(*args)
    if has_delta:
        return res[0], res[1]
    return None, res[0]


def _mm_kernel(a_ref, w_ref, o_ref):
    o_ref[...] = jnp.dot(a_ref[...], w_ref[...],
                         preferred_element_type=jnp.float32).astype(o_ref.dtype)


def _mm(a, w, out_dtype, tn_pref):
    M, K = a.shape
    N = w.shape[1]
    tm = _tile(M, 1024)
    tn = _tile(N, tn_pref)
    return pl.pallas_call(
        _mm_kernel,
        grid=(M // tm, N // tn),
        in_specs=[pl.BlockSpec((tm, K), lambda i, j: (i, 0)),
                  pl.BlockSpec((K, tn), lambda i, j: (0, j))],
        out_specs=pl.BlockSpec((tm, tn), lambda i, j: (i, j)),
        out_shape=jax.ShapeDtypeStruct((M, N), out_dtype),
        name="matmul",
        compiler_params=_cparams(("parallel", "parallel")),
    )(a, w)


def _cumsum_kernel(s_ref, b_ref, o_ref):
    S = s_ref.shape[1]
    x = s_ref[0] + b_ref[...]
    ls = jnp.minimum(x, 0.0) - jnp.log(1.0 + jnp.exp(-jnp.abs(x)))
    r = lax.broadcasted_iota(jnp.int32, (LANE, LANE), 0)
    c = lax.broadcasted_iota(jnp.int32, (LANE, LANE), 1)
    tri = jnp.where(r >= c, 1.0, 0.0).astype(jnp.float32)
    carry = jnp.zeros((1, LANE), jnp.float32)
    for j in range(S // LANE):
        blk = jnp.dot(tri, ls[j * LANE:(j + 1) * LANE, :], precision=lax.Precision.HIGHEST,
                      preferred_element_type=jnp.float32) + carry
        o_ref[0, j * LANE:(j + 1) * LANE, :] = blk
        carry = blk[LANE - 1:LANE, :]


def _forget_cumsum(small, b_pad):
    Bn, S, _ = small.shape
    return pl.pallas_call(
        _cumsum_kernel,
        grid=(Bn,),
        in_specs=[pl.BlockSpec((1, S, LANE), lambda b: (b, 0, BLK_SMALL)),
                  pl.BlockSpec((1, LANE), lambda b: (0, 0))],
        out_specs=pl.BlockSpec((1, S, LANE), lambda b: (b, 0, 0)),
        out_shape=jax.ShapeDtypeStruct((Bn, S, LANE), jnp.float32),
        name="forget_cumsum",
        compiler_params=_cparams(("parallel",)),
    )(small, b_pad)


def _select_kernel(iq_ref, ik_ref, sm_ref, mask_ref, key_ref, *, k_top):
    tq = iq_ref.shape[1]
    S = ik_ref.shape[1]
    i = pl.program_id(1)
    iq = iq_ref[0]
    ik = ik_ref[0][:, :IDX_DIM]
    iw = sm_ref[0][:, COL_IW:COL_IW + IDX_HEADS] * IDX_SCALE
    score = jnp.zeros((tq, S), jnp.float32)
    for h in range(IDX_HEADS):
        d = lax.dot_general(iq[:, h * IDX_DIM:(h + 1) * IDX_DIM], ik,
                            (((1,), (1,)), ((), ())), preferred_element_type=jnp.float32)
        score = score + jnp.where(d > 0.0, d * iw[:, h:h + 1], 0.0)
    qpos = i * tq + lax.broadcasted_iota(jnp.int32, (tq, S), 0)
    col = lax.broadcasted_iota(jnp.int32, (tq, S), 1)
    causal = col <= qpos
    score = jnp.where(causal, score, NEG)
    bits = lax.bitcast_convert_type(score, jnp.int32)
    key_ref[...] = bits ^ ((bits >> 31) & 0x7FFFFFFF)
    kf = jnp.float32(k_top)

    def count_ge(cand):
        return jnp.sum(jnp.where(key_ref[...] >= cand, 1.0, 0.0), axis=1, keepdims=True)

    thr = jnp.where(count_ge(jnp.zeros((tq, 1), jnp.int32)) >= kf, 0, INT_MIN).astype(jnp.int32)

    def value_bit(it, thr):
        cand = thr | jnp.left_shift(jnp.int32(1), 30 - it)
        return jnp.where(count_ge(cand) >= kf, cand, thr)

    thr = lax.fori_loop(0, 31, value_bit, thr)

    key = key_ref[...]
    above = key > thr
    tie = key == thr
    need = kf - jnp.sum(jnp.where(above, 1.0, 0.0), axis=1, keepdims=True)
    n_idx_bits = max(1, (S - 1).bit_length())

    def index_bit(it, last):
        cand = last | jnp.left_shift(jnp.int32(1), n_idx_bits - 1 - it)
        below = jnp.sum(jnp.where((key_ref[...] == thr) & (col < cand), 1.0, 0.0),
                        axis=1, keepdims=True)
        return jnp.where(below < need, cand, last)

    last = lax.fori_loop(0, n_idx_bits, index_bit, jnp.zeros((tq, 1), jnp.int32))
    sel = (above | (tie & (col <= last))) & causal
    mask_ref[0] = jnp.where(sel, 1.0, 0.0).astype(mask_ref.dtype)


def _select(main, f32grp, k_top, tq):
    Bn, S, _ = main.shape
    return pl.pallas_call(
        functools.partial(_select_kernel, k_top=k_top),
        grid=(Bn, S // tq),
        in_specs=[pl.BlockSpec((1, tq, 2 * LANE), lambda b, i: (b, i, BLK_IQ // 2)),
                  pl.BlockSpec((1, S, LANE), lambda b, i: (b, 0, BLK_IK)),
                  pl.BlockSpec((1, tq, LANE), lambda b, i: (b, i, BLK_SMALL))],
        out_specs=pl.BlockSpec((1, tq, S), lambda b, i: (b, i, 0)),
        out_shape=jax.ShapeDtypeStruct((Bn, S, S), jnp.bfloat16),
        scratch_shapes=[pltpu.VMEM((tq, S), jnp.int32)],
        name="dsa_select",
        compiler_params=_cparams(("parallel", "parallel")),
    )(main, main, f32grp)


def _t5_thresholds():
    max_exact = N_BUCKETS // 2
    rel = np.arange(MAX_DIST)
    nf = np.maximum(rel, 1).astype(np.float64)
    large = max_exact + (np.log(nf / max_exact) / math.log(MAX_DIST / max_exact)
                         * (N_BUCKETS - max_exact)).astype(np.int64)
    bucket = np.where(rel < max_exact, rel, np.minimum(large, N_BUCKETS - 1))
    out = []
    for b in range(1, N_BUCKETS):
        hit = np.nonzero(bucket >= b)[0]
        out.append(int(hit[0]) if hit.size else MAX_DIST)
    return out


def _attn_kernel(*refs, mode, head_off, lam_init):
    it = iter(refs)
    q_ref, k_ref, v_ref = next(it), next(it), next(it)
    if mode == "dsa":
        mask_ref = next(it)
    if mode in ("dsa", "diff"):
        tab_ref = next(it)
    if mode == "diff":
        lam_ref, gs_ref = next(it), next(it)
    if mode == "fox":
        fc_ref, fr_ref = next(it), next(it)
    o_ref = next(it)
    m_ref, acc_ref = next(it), next(it)
    if mode in ("dsa", "diff"):
        band_ref = next(it)

    i = pl.program_id(1)
    tq = q_ref.shape[1]
    nmap = 2 if mode == "diff" else 1
    dq = HEAD_DIM // nmap
    scale = dq ** -0.5
    biased = mode in ("dsa", "diff")

    if biased:
        @pl.when(i == 0)
        def _():
            r = lax.broadcasted_iota(jnp.int32, (tq, tq), 0)
            c = lax.broadcasted_iota(jnp.int32, (tq, tq), 1)
            for t in range(2):
                rel = r - c + t * tq
                vals = [jnp.full((tq, tq), tab_ref[0, head_off + h], jnp.float32)
                        for h in range(N_HEADS)]
                for b, first in enumerate(_t5_thresholds(), start=1):
                    hit = rel >= first
                    vals = [jnp.where(hit, tab_ref[b, head_off + h], vals[h]) for h in range(N_HEADS)]
                for h in range(N_HEADS):
                    band_ref[t, h] = vals[h]

    m_ref[...] = jnp.full(m_ref.shape, NEG, jnp.float32)
    acc_ref[...] = jnp.zeros(acc_ref.shape, jnp.float32)
    ones = jnp.ones((tq, LANE), jnp.bfloat16)

    def chunk(off, kind):
        if mode == "dsa":
            keep = mask_ref[0, :, pl.ds(off, tq)].astype(jnp.float32) > 0.5
        elif kind == "diag":
            keep = (lax.broadcasted_iota(jnp.int32, (tq, tq), 0)
                    >= lax.broadcasted_iota(jnp.int32, (tq, tq), 1))
        else:
            keep = None
        for h in range(N_HEADS):
            hs = slice(h * HEAD_DIM, (h + 1) * HEAD_DIM)
            q = q_ref[0, :, hs]
            kc = k_ref[0, pl.ds(off, tq), hs]
            vc = jnp.concatenate([v_ref[0, pl.ds(off, tq), hs], ones], axis=1)
            if mode == "fox":
                fq = fc_ref[0, :, COL_FD + h:COL_FD + h + 1]
                fk = fr_ref[0, h:h + 1, pl.ds(off, tq)]
            for mp in range(nmap):
                s = lax.dot_general(q[:, mp * dq:(mp + 1) * dq], kc[:, mp * dq:(mp + 1) * dq],
                                    (((1,), (1,)), ((), ())),
                                    preferred_element_type=jnp.float32) * scale
                if biased:
                    if kind == "far":
                        s = s + tab_ref[N_BUCKETS - 1, head_off + h]
                    else:
                        s = s + band_ref[1 if kind == "prev" else 0, h]
                else:
                    s = s + fq - fk
                if keep is not None:
                    s = jnp.where(keep, s, NEG)
                idx = h * nmap + mp
                m_old = m_ref[idx]
                m_new = jnp.maximum(m_old, jnp.max(s, axis=1, keepdims=True))
                alpha = jnp.exp(m_old - m_new)
                p = jnp.exp(s - jnp.tile(m_new, (1, tq // LANE)))
                acc_ref[idx] = (jnp.tile(alpha, (1, 2)) * acc_ref[idx]
                                + jnp.dot(p.astype(vc.dtype), vc, preferred_element_type=jnp.float32))
                m_ref[idx] = m_new

    n_far = jnp.maximum(i - 1, 0) if biased else i

    def far_body(c, carry):
        chunk(pl.multiple_of(c * tq, tq), "far")
        return carry

    lax.fori_loop(0, n_far, far_body, 0)
    if biased:
        @pl.when(i >= 1)
        def _():
            chunk(pl.multiple_of((i - 1) * tq, tq), "prev")
    chunk(pl.multiple_of(i * tq, tq), "diag")

    if mode == "diff":
        lq = lam_ref[...]
        lam_val = (jnp.exp(jnp.sum(lq[0:1] * lq[1:2], axis=1, keepdims=True))
                   - jnp.exp(jnp.sum(lq[2:3] * lq[3:4], axis=1, keepdims=True)) + lam_init)
    for h in range(N_HEADS):
        a = acc_ref[h * nmap]
        o = a[:, :HEAD_DIM] / a[:, HEAD_DIM:]
        if mode == "diff":
            a1 = acc_ref[h * nmap + 1]
            o = o - lam_val * (a1[:, :HEAD_DIM] / a1[:, HEAD_DIM:])
            o = o * lax.rsqrt(jnp.mean(o * o, axis=-1, keepdims=True) + EPS) * gs_ref[...]
            o = o * (1.0 - lam_init)
        o_ref[0, :, h * HEAD_DIM:(h + 1) * HEAD_DIM] = o.astype(o_ref.dtype)


def _attention(main, mode, tq, *, mask=None, tab=None, lam=None, g_subln=None,
               fcol=None, frow=None, lam_init=0.0):
    Bn, S, _ = main.shape
    blk_q, blk_k, blk_v, head_off = {
        "dsa": (BLK_QA, BLK_KA, BLK_VA, 0),
        "diff": (BLK_QB, BLK_KB, BLK_VB, N_HEADS),
        "fox": (BLK_QD, BLK_KD, BLK_VD, 0),
    }[mode]
    nmap = 2 if mode == "diff" else 1
    W = N_HEADS * HEAD_DIM
    args = [main, main, main]
    specs = [pl.BlockSpec((1, tq, W), lambda b, i: (b, i, blk_q // N_HEADS)),
             pl.BlockSpec((1, S, W), lambda b, i: (b, 0, blk_k // N_HEADS)),
             pl.BlockSpec((1, S, W), lambda b, i: (b, 0, blk_v // N_HEADS))]
    scratch = [pltpu.VMEM((N_HEADS * nmap, tq, LANE), jnp.float32),
               pltpu.VMEM((N_HEADS * nmap, tq, 2 * HEAD_DIM), jnp.float32)]
    if mode == "dsa":
        args.append(mask)
        specs.append(pl.BlockSpec((1, tq, S), lambda b, i: (b, i, 0)))
    if mode in ("dsa", "diff"):
        args.append(tab)
        specs.append(pl.BlockSpec(memory_space=pltpu.SMEM))
        scratch.append(pltpu.VMEM((2, N_HEADS, tq, tq), jnp.float32))
    if mode == "diff":
        args += [lam, g_subln.reshape(1, HEAD_DIM)]
        specs += [pl.BlockSpec((4, DIFF_DIM), lambda b, i: (0, 0)),
                  pl.BlockSpec((1, HEAD_DIM), lambda b, i: (0, 0))]
    if mode == "fox":
        args += [fcol, frow]
        specs += [pl.BlockSpec((1, tq, LANE), lambda b, i: (b, i, 0)),
                  pl.BlockSpec((1, N_HEADS, S), lambda b, i: (b, 0, 0))]
    return pl.pallas_call(
        functools.partial(_attn_kernel, mode=mode, head_off=head_off, lam_init=lam_init),
        grid=(Bn, S // tq),
        in_specs=specs,
        out_specs=pl.BlockSpec((1, tq, W), lambda b, i: (b, i, 0)),
        out_shape=jax.ShapeDtypeStruct((Bn, S, MIX_W), jnp.bfloat16),
        scratch_shapes=scratch,
        name="attn_" + mode,
        compiler_params=_cparams(("parallel", "arbitrary")),
    )(*args)


def _conv_kernel(bg_ref, cg_ref, xin_ref, w_ref, o_ref):
    u = cg_ref[0] * xin_ref[0]
    row = lax.broadcasted_iota(jnp.int32, u.shape, 0)
    w = w_ref[...]
    y = u * w[CONV_K - 1:CONV_K, :]
    for d in range(1, CONV_K):
        shifted = jnp.where(row >= d, pltpu.roll(u, d, 0), 0.0)
        y = y + shifted * w[CONV_K - 1 - d:CONV_K - d, :]
    o_ref[0] = (bg_ref[0] * y).astype(o_ref.dtype)


def _short_conv(f32grp, conv_w):
    Bn, S, _ = f32grp.shape
    tc = LANE
    per = CONV_CH // tc
    return pl.pallas_call(
        _conv_kernel,
        grid=(Bn, per),
        in_specs=[pl.BlockSpec((1, S, tc), lambda b, j: (b, 0, j)),
                  pl.BlockSpec((1, S, tc), lambda b, j: (b, 0, per + j)),
                  pl.BlockSpec((1, S, tc), lambda b, j: (b, 0, 2 * per + j)),
                  pl.BlockSpec((CONV_K, tc), lambda b, j: (0, j))],
        out_specs=pl.BlockSpec((1, S, tc), lambda b, j: (b, 0, j)),
        out_shape=jax.ShapeDtypeStruct((Bn, S, CONV_CH), jnp.bfloat16),
        name="short_conv",
        compiler_params=_cparams(("parallel", "parallel")),
    )(f32grp, f32grp, f32grp, conv_w)


def _mix_kernel(*refs):
    h_ref = refs[0]
    y_refs = refs[1:1 + N_BRANCH]
    wg_refs = refs[1 + N_BRANCH:1 + 2 * N_BRANCH]
    bg_refs = refs[1 + 2 * N_BRANCH:1 + 3 * N_BRANCH]
    wb_refs = refs[1 + 3 * N_BRANCH:1 + 4 * N_BRANCH]
    o_ref = refs[1 + 4 * N_BRANCH]
    h = h_ref[...]
    merged = None
    for g in range(N_BRANCH):
        gate = jax.nn.sigmoid(jnp.dot(h, wg_refs[g][...], preferred_element_type=jnp.float32)
                              + bg_refs[g][...])
        proj = jnp.dot(y_refs[g][...], wb_refs[g][0], preferred_element_type=jnp.float32)
        merged = gate * proj if merged is None else merged + gate * proj
    o_ref[...] = merged.astype(o_ref.dtype)


def _mix(h, ys, w_gate, b_gate, w_branch):
    T, D = h.shape
    tm = _tile(T, 1024)
    tn = _tile(D, 512)
    per = D // tn
    specs = [pl.BlockSpec((tm, D), lambda i, n: (i, 0))]
    specs += [pl.BlockSpec((tm, MIX_W), lambda i, n: (i, 0)) for _ in range(N_BRANCH)]
    specs += [pl.BlockSpec((D, tn), lambda i, n, g=g: (0, g * per + n)) for g in range(N_BRANCH)]
    specs += [pl.BlockSpec((1, tn), lambda i, n, g=g: (0, g * per + n)) for g in range(N_BRANCH)]
    specs += [pl.BlockSpec((1, MIX_W, tn), lambda i, n, g=g: (g, 0, n)) for g in range(N_BRANCH)]
    return pl.pallas_call(
        _mix_kernel,
        grid=(T // tm, per),
        in_specs=specs,
        out_specs=pl.BlockSpec((tm, tn), lambda i, n: (i, n)),
        out_shape=jax.ShapeDtypeStruct((T, D), jnp.bfloat16),
        name="gated_mix",
        compiler_params=_cparams(("parallel", "parallel")),
    )(h, *ys, *([w_gate] * N_BRANCH), *([b_gate] * N_BRANCH), *([w_branch] * N_BRANCH))


def _ffn_kernel(h_ref, w1_ref, w2_ref, o_ref):
    f = pl.program_id(1)
    u = jnp.dot(h_ref[...], w1_ref[...], preferred_element_type=jnp.float32)
    u = jnp.square(jnp.maximum(u, 0.0)).astype(jnp.bfloat16)
    part = jnp.dot(u, w2_ref[...], preferred_element_type=jnp.float32)

    @pl.when(f == 0)
    def _():
        o_ref[...] = part

    @pl.when(f > 0)
    def _():
        o_ref[...] += part


def _ffn(h, w1, w2):
    T, D = h.shape
    F = w1.shape[1]
    tm = _tile(T, 1024)
    tf = _tile(F, 512)
    return pl.pallas_call(
        _ffn_kernel,
        grid=(T // tm, F // tf),
        in_specs=[pl.BlockSpec((tm, D), lambda i, f: (i, 0)),
                  pl.BlockSpec((D, tf), lambda i, f: (0, f)),
                  pl.BlockSpec((tf, D), lambda i, f: (f, 0))],
        out_specs=pl.BlockSpec((tm, D), lambda i, f: (i, 0)),
        out_shape=jax.ShapeDtypeStruct((T, D), jnp.float32),
        name="ffn",
        compiler_params=_cparams(("parallel", "arbitrary")),
    )(h, w1, w2)


def _split_in_proj(w_in):
    sizes = (512, 512, 512, IDX_HEADS * IDX_DIM, IDX_DIM, IDX_HEADS,
             512, 512, 512, CONV_CH, CONV_CH, CONV_CH, 512, 512, 512, N_HEADS)
    offs = np.concatenate([[0], np.cumsum(sizes)])
    part = lambda j: w_in[:, :, offs[j]:offs[j + 1]]
    (qa, ka, va, iq, ik, iw, qb, kb, vb, bg, cg, xin, qd, kd, vd, fd) = [part(j) for j in range(16)]
    L, D, _ = w_in.shape
    zeros = lambda n: jnp.zeros((L, D, n), w_in.dtype)
    main = jnp.concatenate([qa, ka, va, qb, kb, vb, qd, kd, vd, iq, ik, zeros(LANE - IDX_DIM),
                            zeros(LANE)], axis=-1)
    f32grp = jnp.concatenate([bg, cg, xin, iw, fd, zeros(LANE - IDX_HEADS - N_HEADS)], axis=-1)
    return main.astype(jnp.bfloat16), f32grp.astype(jnp.bfloat16)


def kernel(x, c, w_ada, b_ada, g_norm1, w_in, b_f, conv_w, lam, g_subln, rel_bias,
           w_gate, b_gate, w_branch, w_o, g_norm2, w_ff1, w_ff2, g_final):
    Bn, S, D = x.shape
    L = w_ada.shape[0]
    T = Bn * S
    k_top = min(TOPK_MAX, S // 4)
    tq = _tile(S, 256)
    bf = jnp.bfloat16

    w_main, w_f32 = _split_in_proj(w_in)
    w_gate_b, w_branch_b, w_o_b = w_gate.astype(bf), w_branch.astype(bf), w_o.astype(bf)
    w_ff1_b, w_ff2_b = w_ff1.astype(bf), w_ff2.astype(bf)

    mod = _modulation(c, w_ada, b_ada).reshape(L, Bn, 6, 1, D)
    xt = x.reshape(T, D)
    delta, gate = None, None
    for l in range(L):
        sh1, sc1, gt1, sh2, sc2, gt2 = [mod[l, :, j] for j in range(6)]
        x_new, h = _norm(xt, g_norm1[l], seq=S, delta=delta, gate=gate, sc=sc1, sh=sh1)
        xt = xt if x_new is None else x_new
        main = _mm(h, w_main[l], bf, 1024).reshape(Bn, S, N_MAIN)
        f32grp = _mm(h, w_f32[l], jnp.float32, N_F32).reshape(Bn, S, N_F32)

        b_pad = jnp.zeros((1, LANE), jnp.float32).at[0, COL_FD:COL_FD + N_HEADS].set(b_f[l])
        fcol = _forget_cumsum(f32grp, b_pad)
        frow = jnp.transpose(fcol[:, :, COL_FD:COL_FD + N_HEADS], (0, 2, 1))

        mask = _select(main, f32grp, k_top, tq)
        lam_init = 0.8 - 0.6 * math.exp(-0.3 * l)
        y_a = _attention(main, "dsa", tq, mask=mask, tab=rel_bias)
        y_b = _attention(main, "diff", tq, tab=rel_bias, lam=lam[l], g_subln=g_subln[l],
                         lam_init=lam_init)
        y_c = _short_conv(f32grp, conv_w[l])
        y_d = _attention(main, "fox", tq, fcol=fcol, frow=frow)
        ys = [y.reshape(T, MIX_W) for y in (y_a, y_b, y_c, y_d)]

        merged = _mix(h, ys, w_gate_b[l], b_gate[l].reshape(1, N_BRANCH * D), w_branch_b[l])
        delta = _mm(merged, w_o_b[l], jnp.float32, 1024)
        xt, h2 = _norm(xt, g_norm2[l], seq=S, delta=delta, gate=gt1, sc=sc2, sh=sh2)
        delta = _ffn(h2, w_ff1_b[l], w_ff2_b[l])
        gate = gt2
    _, out = _norm(xt, g_final, seq=S, delta=delta, gate=gate, out_dtype=jnp.float32)
    return out.reshape(Bn, S, D)
```

```python
import functools
import math

import numpy as np
import jax
import jax.numpy as jnp
from jax import lax
from jax.experimental import pallas as pl
from jax.experimental.pallas import tpu as pltpu

HEAD_DIM = 128
N_HEADS = 4
DIFF_DIM = HEAD_DIM // 2
CONV_CH = 512
CONV_K = 3
IDX_HEADS = 4
IDX_DIM = 64
TOPK_MAX = 256
MIX_W = 512
N_BRANCH = 4
N_BUCKETS = 32
MAX_DIST = 128
EPS = 1e-6
NEG = -1e30
IDX_SCALE = (IDX_DIM ** -0.5) * (IDX_HEADS ** -0.5)
INT_MIN = -(2 ** 31)

LANE = 128
VMEM_LIMIT = 56 * 1024 * 1024

BLK_QA, BLK_KA, BLK_VA = 0, 4, 8
BLK_QB, BLK_KB, BLK_VB = 12, 16, 20
BLK_QD, BLK_KD, BLK_VD = 24, 28, 32
BLK_IQ = 36
BLK_IK = 38
N_MAIN = 40 * LANE
N_F32 = 3 * CONV_CH + LANE
BLK_SMALL = 3 * CONV_CH // LANE
COL_IW, COL_FD = 0, 4


def _cparams(sem):
    return pltpu.CompilerParams(dimension_semantics=sem, vmem_limit_bytes=VMEM_LIMIT)


def _tile(n, pref):
    t = min(n, pref)
    while n % t:
        t -= 1
    return t


def _mod_kernel(c_ref, w_ref, b_ref, o_ref):
    c = c_ref[...]
    ca = (c * jax.nn.sigmoid(c)).astype(jnp.bfloat16)
    w = w_ref[0].astype(jnp.bfloat16)
    o_ref[0] = jnp.dot(ca, w, preferred_element_type=jnp.float32) + b_ref[0]


def _modulation(c, w_ada, b_ada):
    L, D, N = w_ada.shape
    Bn = c.shape[0]
    tn = _tile(N, 1024)
    return pl.pallas_call(
        _mod_kernel,
        grid=(L, N // tn),
        in_specs=[
            pl.BlockSpec((Bn, D), lambda l, j: (0, 0)),
            pl.BlockSpec((1, D, tn), lambda l, j: (l, 0, j)),
            pl.BlockSpec((1, 1, tn), lambda l, j: (l, 0, j)),
        ],
        out_specs=pl.BlockSpec((1, Bn, tn), lambda l, j: (l, 0, j)),
        out_shape=jax.ShapeDtypeStruct((L, Bn, N), jnp.float32),
        name="ada_mod",
        compiler_params=_cparams(("parallel", "parallel")),
    )(c, w_ada, b_ada.reshape(L, 1, N))


def _norm_kernel(*refs, has_delta, modulate, keep_x):
    it = iter(refs)
    x_ref = next(it)
    d_ref = next(it) if has_delta else None
    gate_ref = next(it) if has_delta else None
    g_ref = next(it)
    sc_ref = next(it) if modulate else None
    sh_ref = next(it) if modulate else None
    xo_ref = next(it) if keep_x else None
    h_ref = next(it)

    x = x_ref[...]
    if has_delta:
        x = x + gate_ref[0] * d_ref[...]
    if keep_x:
        xo_ref[...] = x
    y = x * lax.rsqrt(jnp.mean(x * x, axis=-1, keepdims=True) + EPS) * g_ref[...]
    if modulate:
        y = y * (1.0 + sc_ref[0]) + sh_ref[0]
    h_ref[...] = y.astype(h_ref.dtype)


def _norm(x, g, *, seq, delta=None, gate=None, sc=None, sh=None, out_dtype=jnp.bfloat16,
          keep_x=True):
    T, D = x.shape
    tm = _tile(seq, 512)
    per_seq = seq // tm
    has_delta = delta is not None
    keep_x = keep_x and has_delta
    modulate = sc is not None
    row = pl.BlockSpec((tm, D), lambda i: (i, 0))
    per_b = pl.BlockSpec((1, 1, D), lambda i: (i // per_seq, 0, 0))
    args, specs = [x], [row]
    if has_delta:
        args += [delta, gate]
        specs += [row, per_b]
    args.append(g.reshape(1, D))
    specs.append(pl.BlockSpec((1, D), lambda i: (0, 0)))
    if modulate:
        args += [sc, sh]
        specs += [per_b, per_b]
    out_shape, out_specs = [], []
    if keep_x:
        out_shape.append(jax.ShapeDtypeStruct((T, D), jnp.float32))
        out_specs.append(row)
    out_shape.append(jax.ShapeDtypeStruct((T, D), out_dtype))
    out_specs.append(row)
    res = pl.pallas_call(
        functools.partial(_norm_kernel, has_delta=has_delta, modulate=modulate, keep_x=keep_x),
        grid=(T // tm,),
        in_specs=specs,
        out_specs=out_specs,
        out_shape=out_shape,
        name="norm",
        compiler_params=_cparams(("parallel",)),
    )(*args)
    if keep_x:
        return res[0], res[1]
    return None, res[0]


def _mm_kernel(a_ref, w_ref, o_ref):
    o_ref[...] = jnp.dot(a_ref[...], w_ref[...],
                         preferred_element_type=jnp.float32).astype(o_ref.dtype)


def _mm(a, w, l, out_dtype, tn_pref):
    M, K = a.shape
    N = w.shape[2]
    tm = _tile(M, 1024)
    tn = _tile(N, tn_pref)
    return pl.pallas_call(
        _mm_kernel,
        grid=(M // tm, N // tn),
        in_specs=[pl.BlockSpec((tm, K), lambda i, j: (i, 0)),
                  pl.BlockSpec((None, K, tn), lambda i, j: (l, 0, j))],
        out_specs=pl.BlockSpec((tm, tn), lambda i, j: (i, j)),
        out_shape=jax.ShapeDtypeStruct((M, N), out_dtype),
        name="matmul",
        compiler_params=_cparams(("parallel", "parallel")),
    )(a, w)


def _cumsum_kernel(s_ref, b_ref, o_ref):
    S = s_ref.shape[1]
    x = s_ref[0] + b_ref[...]
    ls = jnp.minimum(x, 0.0) - jnp.log(1.0 + jnp.exp(-jnp.abs(x)))
    r = lax.broadcasted_iota(jnp.int32, (LANE, LANE), 0)
    c = lax.broadcasted_iota(jnp.int32, (LANE, LANE), 1)
    tri = jnp.where(r >= c, 1.0, 0.0).astype(jnp.float32)
    carry = jnp.zeros((1, LANE), jnp.float32)
    for j in range(S // LANE):
        blk = jnp.dot(tri, ls[j * LANE:(j + 1) * LANE, :], precision=lax.Precision.HIGHEST,
                      preferred_element_type=jnp.float32) + carry
        o_ref[0, j * LANE:(j + 1) * LANE, :] = blk
        carry = blk[LANE - 1:LANE, :]


def _forget_cumsum(small, b_pad):
    Bn, S, _ = small.shape
    return pl.pallas_call(
        _cumsum_kernel,
        grid=(Bn,),
        in_specs=[pl.BlockSpec((1, S, LANE), lambda b: (b, 0, BLK_SMALL)),
                  pl.BlockSpec((1, LANE), lambda b: (0, 0))],
        out_specs=pl.BlockSpec((1, S, LANE), lambda b: (b, 0, 0)),
        out_shape=jax.ShapeDtypeStruct((Bn, S, LANE), jnp.float32),
        name="forget_cumsum",
        compiler_params=_cparams(("parallel",)),
    )(small, b_pad)


def _select_block(iq_ref, ik_ref, sm_ref, mask_ref, key_ref, *, k_top, blk):
    tq = iq_ref.shape[1]
    S = ik_ref.shape[1]
    W = (blk + 1) * tq
    iq = iq_ref[0]
    ik = ik_ref[0, :W, :IDX_DIM]
    iw = sm_ref[0][:, COL_IW:COL_IW + IDX_HEADS] * IDX_SCALE
    score = jnp.zeros((tq, W), jnp.float32)
    for h in range(IDX_HEADS):
        d = lax.dot_general(iq[:, h * IDX_DIM:(h + 1) * IDX_DIM], ik,
                            (((1,), (1,)), ((), ())), preferred_element_type=jnp.float32)
        score = score + jnp.where(d > 0.0, d * iw[:, h:h + 1], 0.0)
    causal = (lax.broadcasted_iota(jnp.int32, (tq, W), 1)
              <= blk * tq + lax.broadcasted_iota(jnp.int32, (tq, W), 0))
    score = jnp.where(causal, score, NEG)
    bits = lax.bitcast_convert_type(score, jnp.int32)
    key_ref[:, :W] = bits ^ ((bits >> 31) & 0x7FFFFFFF)
    kf = jnp.float32(k_top)

    def count_ge(cand):
        return jnp.sum(jnp.where(key_ref[:, :W] >= cand, 1.0, 0.0), axis=1, keepdims=True)

    thr = jnp.where(count_ge(jnp.zeros((tq, 1), jnp.int32)) >= kf, 0, INT_MIN).astype(jnp.int32)

    def value_bit(it, thr):
        cand = thr | jnp.left_shift(jnp.int32(1), 30 - it)
        return jnp.where(count_ge(cand) >= kf, cand, thr)

    thr = lax.fori_loop(0, 31, value_bit, thr)

    key = key_ref[:, :W]
    above = key > thr
    need = kf - jnp.sum(jnp.where(above, 1.0, 0.0), axis=1, keepdims=True)
    upper = jnp.where(lax.broadcasted_iota(jnp.int32, (tq, tq), 0)
                      <= lax.broadcasted_iota(jnp.int32, (tq, tq), 1), 1.0, 0.0).astype(jnp.bfloat16)
    seen = jnp.zeros((tq, 1), jnp.float32)
    for c in range(blk + 1):
        cs = slice(c * tq, (c + 1) * tq)
        tie = key[:, cs] == thr
        incl = seen + jnp.dot(jnp.where(tie, 1.0, 0.0).astype(jnp.bfloat16), upper,
                              preferred_element_type=jnp.float32)
        sel = above[:, cs] | (tie & (incl <= need))
        if c == blk:
            sel = sel & causal[:, cs]
        mask_ref[0, :, cs] = jnp.where(sel, 1.0, 0.0).astype(mask_ref.dtype)
        seen = incl[:, tq - 1:tq]
    if W < S:
        mask_ref[0, :, W:] = jnp.zeros((tq, S - W), mask_ref.dtype)


def _select_kernel(iq_ref, ik_ref, sm_ref, mask_ref, key_ref, *, k_top):
    i = pl.program_id(1)
    for blk in range(ik_ref.shape[1] // iq_ref.shape[1]):
        @pl.when(i == blk)
        def _(blk=blk):
            _select_block(iq_ref, ik_ref, sm_ref, mask_ref, key_ref, k_top=k_top, blk=blk)


def _select(main, f32grp, k_top, tq):
    Bn, S, _ = main.shape
    return pl.pallas_call(
        functools.partial(_select_kernel, k_top=k_top),
        grid=(Bn, S // tq),
        in_specs=[pl.BlockSpec((1, tq, 2 * LANE), lambda b, i: (b, i, BLK_IQ // 2)),
                  pl.BlockSpec((1, S, LANE), lambda b, i: (b, 0, BLK_IK)),
                  pl.BlockSpec((1, tq, LANE), lambda b, i: (b, i, BLK_SMALL))],
        out_specs=pl.BlockSpec((1, tq, S), lambda b, i: (b, i, 0)),
        out_shape=jax.ShapeDtypeStruct((Bn, S, S), jnp.bfloat16),
        scratch_shapes=[pltpu.VMEM((tq, S), jnp.int32)],
        name="dsa_select",
        compiler_params=_cparams(("parallel", "parallel")),
    )(main, main, f32grp)


def _t5_thresholds():
    max_exact = N_BUCKETS // 2
    rel = np.arange(MAX_DIST)
    nf = np.maximum(rel, 1).astype(np.float64)
    large = max_exact + (np.log(nf / max_exact) / math.log(MAX_DIST / max_exact)
                         * (N_BUCKETS - max_exact)).astype(np.int64)
    bucket = np.where(rel < max_exact, rel, np.minimum(large, N_BUCKETS - 1))
    out = []
    for b in range(1, N_BUCKETS):
        hit = np.nonzero(bucket >= b)[0]
        out.append(int(hit[0]) if hit.size else MAX_DIST)
    return out


def _attn_kernel(*refs, mode, head_off, lam_init):
    it = iter(refs)
    q_ref, k_ref, v_ref = next(it), next(it), next(it)
    if mode == "dsa":
        mask_ref = next(it)
    if mode in ("dsa", "diff"):
        tab_ref = next(it)
    if mode == "diff":
        lam_ref, gs_ref = next(it), next(it)
    if mode == "fox":
        fc_ref, fr_ref = next(it), next(it)
    o_ref = next(it)
    m_ref, acc_ref = next(it), next(it)
    if mode in ("dsa", "diff"):
        band_ref = next(it)

    i = pl.program_id(1)
    tq = q_ref.shape[1]
    nmap = 2 if mode == "diff" else 1
    dq = HEAD_DIM // nmap
    scale = dq ** -0.5
    biased = mode in ("dsa", "diff")

    if biased:
        @pl.when(i == 0)
        def _():
            r = lax.broadcasted_iota(jnp.int32, (tq, tq), 0)
            c = lax.broadcasted_iota(jnp.int32, (tq, tq), 1)
            for t in range(2):
                rel = r - c + t * tq
                vals = [jnp.full((tq, tq), tab_ref[0, head_off + h], jnp.float32)
                        for h in range(N_HEADS)]
                for b, first in enumerate(_t5_thresholds(), start=1):
                    hit = rel >= first
                    vals = [jnp.where(hit, tab_ref[b, head_off + h], vals[h]) for h in range(N_HEADS)]
                for h in range(N_HEADS):
                    band_ref[t, h] = vals[h]

    m_ref[...] = jnp.full(m_ref.shape, NEG, jnp.float32)
    acc_ref[...] = jnp.zeros(acc_ref.shape, jnp.float32)
    ones = jnp.ones((tq, LANE), jnp.bfloat16)

    def chunk(off, kind):
        if mode == "dsa":
            keep = mask_ref[0, :, pl.ds(off, tq)].astype(jnp.float32) > 0.5
        elif kind == "diag":
            keep = (lax.broadcasted_iota(jnp.int32, (tq, tq), 0)
                    >= lax.broadcasted_iota(jnp.int32, (tq, tq), 1))
        else:
            keep = None
        for h in range(N_HEADS):
            hs = slice(h * HEAD_DIM, (h + 1) * HEAD_DIM)
            q = q_ref[0, :, hs]
            kc = k_ref[0, pl.ds(off, tq), hs]
            vc = jnp.concatenate([v_ref[0, pl.ds(off, tq), hs], ones], axis=1)
            if mode == "fox":
                fq = fc_ref[0, :, COL_FD + h:COL_FD + h + 1]
                fk = fr_ref[0, h:h + 1, pl.ds(off, tq)]
            for mp in range(nmap):
                s = lax.dot_general(q[:, mp * dq:(mp + 1) * dq], kc[:, mp * dq:(mp + 1) * dq],
                                    (((1,), (1,)), ((), ())),
                                    preferred_element_type=jnp.float32) * scale
                if biased:
                    if kind == "far":
                        s = s + tab_ref[N_BUCKETS - 1, head_off + h]
                    else:
                        s = s + band_ref[1 if kind == "prev" else 0, h]
                else:
                    s = s + fq - fk
                if keep is not None:
                    s = jnp.where(keep, s, NEG)
                idx = h * nmap + mp
                m_old = m_ref[idx]
                m_new = jnp.maximum(m_old, jnp.max(s, axis=1, keepdims=True))
                alpha = jnp.exp(m_old - m_new)
                p = jnp.exp(s - jnp.tile(m_new, (1, tq // LANE)))
                acc_ref[idx] = (jnp.tile(alpha, (1, 2)) * acc_ref[idx]
                                + jnp.dot(p.astype(vc.dtype), vc, preferred_element_type=jnp.float32))
                m_ref[idx] = m_new

    n_far = jnp.maximum(i - 1, 0) if biased else i

    def far_body(c, carry):
        chunk(pl.multiple_of(c * tq, tq), "far")
        return carry

    lax.fori_loop(0, n_far, far_body, 0)
    if biased:
        @pl.when(i >= 1)
        def _():
            chunk(pl.multiple_of((i - 1) * tq, tq), "prev")
    chunk(pl.multiple_of(i * tq, tq), "diag")

    if mode == "diff":
        lq = lam_ref[...]
        lam_val = (jnp.exp(jnp.sum(lq[0:1] * lq[1:2], axis=1, keepdims=True))
                   - jnp.exp(jnp.sum(lq[2:3] * lq[3:4], axis=1, keepdims=True)) + lam_init)
    for h in range(N_HEADS):
        a = acc_ref[h * nmap]
        o = a[:, :HEAD_DIM] / a[:, HEAD_DIM:]
        if mode == "diff":
            a1 = acc_ref[h * nmap + 1]
            o = o - lam_val * (a1[:, :HEAD_DIM] / a1[:, HEAD_DIM:])
            o = o * lax.rsqrt(jnp.mean(o * o, axis=-1, keepdims=True) + EPS) * gs_ref[...]
            o = o * (1.0 - lam_init)
        o_ref[0, :, h * HEAD_DIM:(h + 1) * HEAD_DIM] = o.astype(o_ref.dtype)


def _attention(main, mode, tq, *, mask=None, tab=None, lam=None, g_subln=None,
               fcol=None, frow=None, lam_init=0.0):
    Bn, S, _ = main.shape
    blk_q, blk_k, blk_v, head_off = {
        "dsa": (BLK_QA, BLK_KA, BLK_VA, 0),
        "diff": (BLK_QB, BLK_KB, BLK_VB, N_HEADS),
        "fox": (BLK_QD, BLK_KD, BLK_VD, 0),
    }[mode]
    nmap = 2 if mode == "diff" else 1
    W = N_HEADS * HEAD_DIM
    args = [main, main, main]
    specs = [pl.BlockSpec((1, tq, W), lambda b, i: (b, i, blk_q // N_HEADS)),
             pl.BlockSpec((1, S, W), lambda b, i: (b, 0, blk_k // N_HEADS)),
             pl.BlockSpec((1, S, W), lambda b, i: (b, 0, blk_v // N_HEADS))]
    scratch = [pltpu.VMEM((N_HEADS * nmap, tq, LANE), jnp.float32),
               pltpu.VMEM((N_HEADS * nmap, tq, 2 * HEAD_DIM), jnp.float32)]
    if mode == "dsa":
        args.append(mask)
        specs.append(pl.BlockSpec((1, tq, S), lambda b, i: (b, i, 0)))
    if mode in ("dsa", "diff"):
        args.append(tab)
        specs.append(pl.BlockSpec(memory_space=pltpu.SMEM))
        scratch.append(pltpu.VMEM((2, N_HEADS, tq, tq), jnp.float32))
    if mode == "diff":
        args += [lam, g_subln.reshape(1, HEAD_DIM)]
        specs += [pl.BlockSpec((4, DIFF_DIM), lambda b, i: (0, 0)),
                  pl.BlockSpec((1, HEAD_DIM), lambda b, i: (0, 0))]
    if mode == "fox":
        args += [fcol, frow]
        specs += [pl.BlockSpec((1, tq, LANE), lambda b, i: (b, i, 0)),
                  pl.BlockSpec((1, N_HEADS, S), lambda b, i: (b, 0, 0))]
    return pl.pallas_call(
        functools.partial(_attn_kernel, mode=mode, head_off=head_off, lam_init=lam_init),
        grid=(Bn, S // tq),
        in_specs=specs,
        out_specs=pl.BlockSpec((1, tq, W), lambda b, i: (b, i, 0)),
        out_shape=jax.ShapeDtypeStruct((Bn, S, MIX_W), jnp.bfloat16),
        scratch_shapes=scratch,
        name="attn_" + mode,
        compiler_params=_cparams(("parallel", "arbitrary")),
    )(*args)


def _conv_kernel(bg_ref, cg_ref, xin_ref, w_ref, o_ref):
    u = cg_ref[0] * xin_ref[0]
    row = lax.broadcasted_iota(jnp.int32, u.shape, 0)
    w = w_ref[...]
    y = u * w[CONV_K - 1:CONV_K, :]
    for d in range(1, CONV_K):
        shifted = jnp.where(row >= d, pltpu.roll(u, d, 0), 0.0)
        y = y + shifted * w[CONV_K - 1 - d:CONV_K - d, :]
    o_ref[0] = (bg_ref[0] * y).astype(o_ref.dtype)


def _short_conv(f32grp, conv_w):
    Bn, S, _ = f32grp.shape
    tc = LANE
    per = CONV_CH // tc
    return pl.pallas_call(
        _conv_kernel,
        grid=(Bn, per),
        in_specs=[pl.BlockSpec((1, S, tc), lambda b, j: (b, 0, j)),
                  pl.BlockSpec((1, S, tc), lambda b, j: (b, 0, per + j)),
                  pl.BlockSpec((1, S, tc), lambda b, j: (b, 0, 2 * per + j)),
                  pl.BlockSpec((CONV_K, tc), lambda b, j: (0, j))],
        out_specs=pl.BlockSpec((1, S, tc), lambda b, j: (b, 0, j)),
        out_shape=jax.ShapeDtypeStruct((Bn, S, CONV_CH), jnp.bfloat16),
        name="short_conv",
        compiler_params=_cparams(("parallel", "parallel")),
    )(f32grp, f32grp, f32grp, conv_w)


def _mix_kernel(*refs):
    h_ref = refs[0]
    y_refs = refs[1:1 + N_BRANCH]
    wg_refs = refs[1 + N_BRANCH:1 + 2 * N_BRANCH]
    bg_refs = refs[1 + 2 * N_BRANCH:1 + 3 * N_BRANCH]
    wb_refs = refs[1 + 3 * N_BRANCH:1 + 4 * N_BRANCH]
    o_ref = refs[1 + 4 * N_BRANCH]
    h = h_ref[...]
    merged = None
    for g in range(N_BRANCH):
        gate = jax.nn.sigmoid(jnp.dot(h, wg_refs[g][...], preferred_element_type=jnp.float32)
                              + bg_refs[g][...])
        proj = jnp.dot(y_refs[g][...], wb_refs[g][...], preferred_element_type=jnp.float32)
        merged = gate * proj if merged is None else merged + gate * proj
    o_ref[...] = merged.astype(o_ref.dtype)


def _mix(h, ys, w_gate, b_gate, w_branch, l):
    T, D = h.shape
    tm = _tile(T, 1024)
    tn = _tile(D, 512)
    per = D // tn
    specs = [pl.BlockSpec((tm, D), lambda i, n: (i, 0))]
    specs += [pl.BlockSpec((tm, MIX_W), lambda i, n: (i, 0)) for _ in range(N_BRANCH)]
    specs += [pl.BlockSpec((None, D, tn), lambda i, n, g=g: (l, 0, g * per + n))
              for g in range(N_BRANCH)]
    specs += [pl.BlockSpec((None, 1, tn), lambda i, n, g=g: (l, 0, g * per + n))
              for g in range(N_BRANCH)]
    specs += [pl.BlockSpec((None, None, MIX_W, tn), lambda i, n, g=g: (l, g, 0, n))
              for g in range(N_BRANCH)]
    return pl.pallas_call(
        _mix_kernel,
        grid=(T // tm, per),
        in_specs=specs,
        out_specs=pl.BlockSpec((tm, tn), lambda i, n: (i, n)),
        out_shape=jax.ShapeDtypeStruct((T, D), jnp.bfloat16),
        name="gated_mix",
        compiler_params=_cparams(("parallel", "parallel")),
    )(h, *ys, *([w_gate] * N_BRANCH), *([b_gate] * N_BRANCH), *([w_branch] * N_BRANCH))


def _ffn_kernel(h_ref, w1_ref, w2_ref, o_ref, *, sub):
    f = pl.program_id(1)
    h = h_ref[...]
    part = None
    for j in range(w1_ref.shape[1] // sub):
        u = jnp.dot(h, w1_ref[:, j * sub:(j + 1) * sub], preferred_element_type=jnp.float32)
        u = jnp.square(jnp.maximum(u, 0.0)).astype(jnp.bfloat16)
        pj = jnp.dot(u, w2_ref[j * sub:(j + 1) * sub, :], preferred_element_type=jnp.float32)
        part = pj if part is None else part + pj

    @pl.when(f == 0)
    def _():
        o_ref[...] = part

    @pl.when(f > 0)
    def _():
        o_ref[...] += part


def _ffn(h, w1, w2, l):
    T, D = h.shape
    F = w1.shape[2]
    tm = _tile(T, 1024)
    tf = _tile(F, 1024)
    return pl.pallas_call(
        functools.partial(_ffn_kernel, sub=_tile(tf, 512)),
        grid=(T // tm, F // tf),
        in_specs=[pl.BlockSpec((tm, D), lambda i, f: (i, 0)),
                  pl.BlockSpec((None, D, tf), lambda i, f: (l, 0, f)),
                  pl.BlockSpec((None, tf, D), lambda i, f: (l, f, 0))],
        out_specs=pl.BlockSpec((tm, D), lambda i, f: (i, 0)),
        out_shape=jax.ShapeDtypeStruct((T, D), jnp.float32),
        name="ffn",
        compiler_params=_cparams(("parallel", "arbitrary")),
    )(h, w1, w2)


def _split_in_proj(w_in):
    W = N_HEADS * HEAD_DIM
    sizes = (3 * W,
             IDX_HEADS * IDX_DIM + IDX_DIM,
             IDX_HEADS,
             3 * W,
             3 * CONV_CH,
             3 * W,
             N_HEADS)
    offs = np.concatenate([[0], np.cumsum(sizes)])
    a, idx, iw, b, cv, d, fd = [w_in[:, :, offs[j]:offs[j + 1]] for j in range(7)]
    L, D, _ = w_in.shape
    zeros = lambda n: jnp.zeros((L, D, n), w_in.dtype)
    main = jnp.concatenate([a, b, d, idx, zeros(N_MAIN - 9 * W - sizes[1])], axis=-1)
    f32grp = jnp.concatenate([cv, iw, fd, zeros(LANE - IDX_HEADS - N_HEADS)], axis=-1)
    return main.astype(jnp.bfloat16), f32grp.astype(jnp.bfloat16)


def kernel(x, c, w_ada, b_ada, g_norm1, w_in, b_f, conv_w, lam, g_subln, rel_bias,
           w_gate, b_gate, w_branch, w_o, g_norm2, w_ff1, w_ff2, g_final):
    Bn, S, D = x.shape
    L = w_ada.shape[0]
    T = Bn * S
    k_top = min(TOPK_MAX, S // 4)
    tq = _tile(S, 256)
    bf = jnp.bfloat16

    w_main, w_f32 = _split_in_proj(w_in)
    w_gate_b, w_branch_b, w_o_b = w_gate.astype(bf), w_branch.astype(bf), w_o.astype(bf)
    w_ff1_b, w_ff2_b = w_ff1.astype(bf), w_ff2.astype(bf)

    mod = _modulation(c, w_ada, b_ada).reshape(L, Bn, 6, 1, D)
    xt = x.reshape(T, D)
    delta, gate = None, None
    for l in range(L):
        sh1, sc1, gt1, sh2, sc2, gt2 = [mod[l, :, j] for j in range(6)]
        x_new, h = _norm(xt, g_norm1[l], seq=S, delta=delta, gate=gate, sc=sc1, sh=sh1)
        xt = xt if x_new is None else x_new
        main = _mm(h, w_main, l, bf, 1024).reshape(Bn, S, N_MAIN)
        f32grp = _mm(h, w_f32, l, jnp.float32, N_F32).reshape(Bn, S, N_F32)

        b_pad = jnp.zeros((1, LANE), jnp.float32).at[0, COL_FD:COL_FD + N_HEADS].set(b_f[l])
        fcol = _forget_cumsum(f32grp, b_pad)
        frow = jnp.transpose(fcol[:, :, COL_FD:COL_FD + N_HEADS], (0, 2, 1))

        mask = _select(main, f32grp, k_top, tq)
        lam_init = 0.8 - 0.6 * math.exp(-0.3 * l)
        y_a = _attention(main, "dsa", tq, mask=mask, tab=rel_bias)
        y_b = _attention(main, "diff", tq, tab=rel_bias, lam=lam[l], g_subln=g_subln[l],
                         lam_init=lam_init)
        y_c = _short_conv(f32grp, conv_w[l])
        y_d = _attention(main, "fox", tq, fcol=fcol, frow=frow)
        ys = [y.reshape(T, MIX_W) for y in (y_a, y_b, y_c, y_d)]

        merged = _mix(h, ys, w_gate_b, b_gate.reshape(L, 1, N_BRANCH * D), w_branch_b, l)
        delta = _mm(merged, w_o_b, l, jnp.float32, 1024)
        xt, h2 = _norm(xt, g_norm2[l], seq=S, delta=delta, gate=gt1, sc=sc2, sh=sh2)
        delta = _ffn(h2, w_ff1_b, w_ff2_b, l)
        gate = gt2
    _, out = _norm(xt, g_final, seq=S, delta=delta, gate=gate, out_dtype=jnp.float32,
                   keep_x=False)
    return out.reshape(Bn, S, D)
```

```python
import functools
import math

import numpy as np
import jax
import jax.numpy as jnp
from jax import lax
from jax.experimental import pallas as pl
from jax.experimental.pallas import tpu as pltpu

HEAD_DIM = 128
N_HEADS = 4
DIFF_DIM = HEAD_DIM // 2
CONV_CH = 512
CONV_K = 3
IDX_HEADS = 4
IDX_DIM = 64
TOPK_MAX = 256
MIX_W = 512
N_BRANCH = 4
N_BUCKETS = 32
MAX_DIST = 128
EPS = 1e-6
NEG = -1e30
IDX_SCALE = (IDX_DIM ** -0.5) * (IDX_HEADS ** -0.5)
INT_MIN = -(2 ** 31)

LANE = 128
VMEM_LIMIT = 56 * 1024 * 1024

BLK_QA, BLK_KA, BLK_VA = 0, 4, 8
BLK_QB, BLK_KB, BLK_VB = 12, 16, 20
BLK_QD, BLK_KD, BLK_VD = 24, 28, 32
BLK_IQ = 36
BLK_IK = 38
N_MAIN = 40 * LANE
N_F32 = 3 * CONV_CH + LANE
BLK_SMALL = 3 * CONV_CH // LANE
COL_IW, COL_FD = 0, 4


def _cparams(sem):
    return pltpu.CompilerParams(dimension_semantics=sem, vmem_limit_bytes=VMEM_LIMIT)


def _tile(n, pref):
    t = min(n, pref)
    while n % t:
        t -= 1
    return t


def _mod_kernel(c_ref, w_ref, b_ref, o_ref):
    c = c_ref[...]
    ca = (c * jax.nn.sigmoid(c)).astype(jnp.bfloat16)
    w = w_ref[0].astype(jnp.bfloat16)
    o_ref[0] = jnp.dot(ca, w, preferred_element_type=jnp.float32) + b_ref[0]


def _modulation(c, w_ada, b_ada):
    L, D, N = w_ada.shape
    Bn = c.shape[0]
    tn = _tile(N, 1024)
    return pl.pallas_call(
        _mod_kernel,
        grid=(L, N // tn),
        in_specs=[
            pl.BlockSpec((Bn, D), lambda l, j: (0, 0)),
            pl.BlockSpec((1, D, tn), lambda l, j: (l, 0, j)),
            pl.BlockSpec((1, 1, tn), lambda l, j: (l, 0, j)),
        ],
        out_specs=pl.BlockSpec((1, Bn, tn), lambda l, j: (l, 0, j)),
        out_shape=jax.ShapeDtypeStruct((L, Bn, N), jnp.float32),
        name="ada_mod",
        compiler_params=_cparams(("parallel", "parallel")),
    )(c, w_ada, b_ada.reshape(L, 1, N))


def _norm_kernel(*refs, has_delta, modulate, keep_x):
    it = iter(refs)
    x_ref = next(it)
    d_ref = next(it) if has_delta else None
    gate_ref = next(it) if has_delta else None
    g_ref = next(it)
    sc_ref = next(it) if modulate else None
    sh_ref = next(it) if modulate else None
    xo_ref = next(it) if keep_x else None
    h_ref = next(it)

    x = x_ref[...]
    if has_delta:
        x = x + gate_ref[0] * d_ref[...]
    if keep_x:
        xo_ref[...] = x
    y = x * lax.rsqrt(jnp.mean(x * x, axis=-1, keepdims=True) + EPS) * g_ref[...]
    if modulate:
        y = y * (1.0 + sc_ref[0]) + sh_ref[0]
    h_ref[...] = y.astype(h_ref.dtype)


def _norm(x, g, *, seq, delta=None, gate=None, sc=None, sh=None, out_dtype=jnp.bfloat16,
          keep_x=True):
    T, D = x.shape
    tm = _tile(seq, 512)
    per_seq = seq // tm
    has_delta = delta is not None
    keep_x = keep_x and has_delta
    modulate = sc is not None
    row = pl.BlockSpec((tm, D), lambda i: (i, 0))
    per_b = pl.BlockSpec((1, 1, D), lambda i: (i // per_seq, 0, 0))
    args, specs = [x], [row]
    if has_delta:
        args += [delta, gate]
        specs += [row, per_b]
    args.append(g.reshape(1, D))
    specs.append(pl.BlockSpec((1, D), lambda i: (0, 0)))
    if modulate:
        args += [sc, sh]
        specs += [per_b, per_b]
    out_shape, out_specs = [], []
    if keep_x:
        out_shape.append(jax.ShapeDtypeStruct((T, D), jnp.float32))
        out_specs.append(row)
    out_shape.append(jax.ShapeDtypeStruct((T, D), out_dtype))
    out_specs.append(row)
    res = pl.pallas_call(
        functools.partial(_norm_kernel, has_delta=has_delta, modulate=modulate, keep_x=keep_x),
        grid=(T // tm,),
        in_specs=specs,
        out_specs=out_specs,
        out_shape=out_shape,
        name="norm",
        compiler_params=_cparams(("parallel",)),
    )(*args)
    if keep_x:
        return res[0], res[1]
    return None, res[0]


def _mm_kernel(a_ref, w_ref, o_ref):
    o_ref[...] = jnp.dot(a_ref[...], w_ref[...],
                         preferred_element_type=jnp.float32).astype(o_ref.dtype)


def _mm(a, w, l, out_dtype, tn_pref):
    M, K = a.shape
    N = w.shape[2]
    tm = _tile(M, 1024)
    tn = _tile(N, tn_pref)
    return pl.pallas_call(
        _mm_kernel,
        grid=(M // tm, N // tn),
        in_specs=[pl.BlockSpec((tm, K), lambda i, j: (i, 0)),
                  pl.BlockSpec((None, K, tn), lambda i, j: (l, 0, j))],
        out_specs=pl.BlockSpec((tm, tn), lambda i, j: (i, j)),
        out_shape=jax.ShapeDtypeStruct((M, N), out_dtype),
        name="matmul",
        compiler_params=_cparams(("parallel", "parallel")),
    )(a, w)


def _cumsum_kernel(s_ref, b_ref, o_ref):
    S = s_ref.shape[1]
    x = s_ref[0] + b_ref[...]
    ls = jnp.minimum(x, 0.0) - jnp.log(1.0 + jnp.exp(-jnp.abs(x)))
    r = lax.broadcasted_iota(jnp.int32, (LANE, LANE), 0)
    c = lax.broadcasted_iota(jnp.int32, (LANE, LANE), 1)
    tri = jnp.where(r >= c, 1.0, 0.0).astype(jnp.bfloat16)
    hi = ls.astype(jnp.bfloat16)
    r1 = ls - hi.astype(jnp.float32)
    mid = r1.astype(jnp.bfloat16)
    lo = (r1 - mid.astype(jnp.float32)).astype(jnp.bfloat16)
    carry = jnp.zeros((1, LANE), jnp.float32)
    for j in range(S // LANE):
        rows = slice(j * LANE, (j + 1) * LANE)
        blk = (jnp.dot(tri, hi[rows], preferred_element_type=jnp.float32)
               + jnp.dot(tri, mid[rows], preferred_element_type=jnp.float32)
               + jnp.dot(tri, lo[rows], preferred_element_type=jnp.float32)) + carry
        o_ref[0, j * LANE:(j + 1) * LANE, :] = blk
        carry = blk[LANE - 1:LANE, :]


def _forget_cumsum(small, b_pad):
    Bn, S, _ = small.shape
    return pl.pallas_call(
        _cumsum_kernel,
        grid=(Bn,),
        in_specs=[pl.BlockSpec((1, S, LANE), lambda b: (b, 0, BLK_SMALL)),
                  pl.BlockSpec((1, LANE), lambda b: (0, 0))],
        out_specs=pl.BlockSpec((1, S, LANE), lambda b: (b, 0, 0)),
        out_shape=jax.ShapeDtypeStruct((Bn, S, LANE), jnp.float32),
        name="forget_cumsum",
        compiler_params=_cparams(("parallel",)),
    )(small, b_pad)


def _select_block(iq_ref, ik_ref, sm_ref, mask_ref, key_ref, hi_ref, lo_ref, *, k_top, blk):
    tq = iq_ref.shape[1]
    S = ik_ref.shape[1]
    W = (blk + 1) * tq
    iq = iq_ref[0]
    ik = ik_ref[0, :W, :IDX_DIM]
    iw = sm_ref[0][:, COL_IW:COL_IW + IDX_HEADS] * IDX_SCALE
    score = jnp.zeros((tq, W), jnp.float32)
    for h in range(IDX_HEADS):
        d = lax.dot_general(iq[:, h * IDX_DIM:(h + 1) * IDX_DIM], ik,
                            (((1,), (1,)), ((), ())), preferred_element_type=jnp.float32)
        score = score + jnp.where(d > 0.0, d * iw[:, h:h + 1], 0.0)
    causal = (lax.broadcasted_iota(jnp.int32, (tq, W), 1)
              <= blk * tq + lax.broadcasted_iota(jnp.int32, (tq, W), 0))
    score = jnp.where(causal, score, NEG)
    bits = lax.bitcast_convert_type(score, jnp.int32)
    key = bits ^ ((bits >> 31) & 0x7FFFFFFF)
    key_ref[:, :W] = key
    kf = jnp.float32(k_top)
    i16_min = -(2 ** 15)

    def kth_largest16(half_ref, kth):
        def count_ge(cand):
            c16 = cand.astype(jnp.int16)
            acc = jnp.zeros((tq, LANE), jnp.int16)
            for cb in range(W // LANE):
                acc = acc + jnp.where(half_ref[:, cb * LANE:(cb + 1) * LANE] >= c16,
                                      jnp.int16(1), jnp.int16(0))
            return jnp.sum(acc.astype(jnp.float32), axis=1, keepdims=True)

        t0 = jnp.where(count_ge(jnp.zeros((tq, 1), jnp.int32)) >= kth, 0, i16_min).astype(jnp.int32)

        def bit(it, t):
            cand = t | jnp.left_shift(jnp.int32(1), 14 - it)
            return jnp.where(count_ge(cand) >= kth, cand, t)

        return lax.fori_loop(0, 15, bit, t0), count_ge

    hi_ref[:, :W] = (key >> 16).astype(jnp.int16)
    thr_hi, count_hi_ge = kth_largest16(hi_ref, kf)
    n_above_hi = jnp.where(thr_hi >= 2 ** 15 - 1, 0.0, count_hi_ge(thr_hi + 1))
    lo = ((key & 0xFFFF) + i16_min).astype(jnp.int16)
    lo_ref[:, :W] = jnp.where(hi_ref[:, :W] == thr_hi.astype(jnp.int16), lo, jnp.int16(i16_min))
    thr_lo, _ = kth_largest16(lo_ref, kf - n_above_hi)
    thr = thr_hi * 65536 + (thr_lo - i16_min)

    above = key > thr
    need = kf - jnp.sum(jnp.where(above, 1.0, 0.0), axis=1, keepdims=True)
    upper = jnp.where(lax.broadcasted_iota(jnp.int32, (tq, tq), 0)
                      <= lax.broadcasted_iota(jnp.int32, (tq, tq), 1), 1.0, 0.0).astype(jnp.bfloat16)
    seen = jnp.zeros((tq, 1), jnp.float32)
    for c in range(blk + 1):
        cs = slice(c * tq, (c + 1) * tq)
        tie = key[:, cs] == thr
        incl = seen + jnp.dot(jnp.where(tie, 1.0, 0.0).astype(jnp.bfloat16), upper,
                              preferred_element_type=jnp.float32)
        sel = above[:, cs] | (tie & (incl <= need))
        if c == blk:
            sel = sel & causal[:, cs]
        mask_ref[0, :, cs] = jnp.where(sel, 1.0, 0.0).astype(mask_ref.dtype)
        seen = incl[:, tq - 1:tq]
    if W < S:
        mask_ref[0, :, W:] = jnp.zeros((tq, S - W), mask_ref.dtype)


def _select_kernel(iq_ref, ik_ref, sm_ref, mask_ref, key_ref, hi_ref, lo_ref, *, k_top):
    i = pl.program_id(1)
    for blk in range(ik_ref.shape[1] // iq_ref.shape[1]):
        @pl.when(i == blk)
        def _(blk=blk):
            _select_block(iq_ref, ik_ref, sm_ref, mask_ref, key_ref, hi_ref, lo_ref, k_top=k_top, blk=blk)


def _select(main, f32grp, k_top, tq):
    Bn, S, _ = main.shape
    return pl.pallas_call(
        functools.partial(_select_kernel, k_top=k_top),
        grid=(Bn, S // tq),
        in_specs=[pl.BlockSpec((1, tq, 2 * LANE), lambda b, i: (b, i, BLK_IQ // 2)),
                  pl.BlockSpec((1, S, LANE), lambda b, i: (b, 0, BLK_IK)),
                  pl.BlockSpec((1, tq, LANE), lambda b, i: (b, i, BLK_SMALL))],
        out_specs=pl.BlockSpec((1, tq, S), lambda b, i: (b, i, 0)),
        out_shape=jax.ShapeDtypeStruct((Bn, S, S), jnp.bfloat16),
        scratch_shapes=[pltpu.VMEM((tq, S), jnp.int32), pltpu.VMEM((tq, S), jnp.int16),
                        pltpu.VMEM((tq, S), jnp.int16)],
        name="dsa_select",
        compiler_params=_cparams(("parallel", "parallel")),
    )(main, main, f32grp)


def _t5_thresholds():
    max_exact = N_BUCKETS // 2
    rel = np.arange(MAX_DIST)
    nf = np.maximum(rel, 1).astype(np.float64)
    large = max_exact + (np.log(nf / max_exact) / math.log(MAX_DIST / max_exact)
                         * (N_BUCKETS - max_exact)).astype(np.int64)
    bucket = np.where(rel < max_exact, rel, np.minimum(large, N_BUCKETS - 1))
    out = []
    for b in range(1, N_BUCKETS):
        hit = np.nonzero(bucket >= b)[0]
        out.append(int(hit[0]) if hit.size else MAX_DIST)
    return out


def _attn_kernel(*refs, mode, head_off, lam_init):
    it = iter(refs)
    q_ref, k_ref, v_ref = next(it), next(it), next(it)
    if mode == "dsa":
        mask_ref = next(it)
    if mode in ("dsa", "diff"):
        tab_ref = next(it)
    if mode == "diff":
        lam_ref, gs_ref = next(it), next(it)
    if mode == "fox":
        fc_ref, fr_ref = next(it), next(it)
    o_ref = next(it)
    m_ref, acc_ref = next(it), next(it)
    if mode in ("dsa", "diff"):
        band_ref = next(it)

    i = pl.program_id(1)
    tq = q_ref.shape[1]
    nmap = 2 if mode == "diff" else 1
    dq = HEAD_DIM // nmap
    scale = dq ** -0.5
    biased = mode in ("dsa", "diff")

    if biased:
        @pl.when(i == 0)
        def _():
            r = lax.broadcasted_iota(jnp.int32, (tq, tq), 0)
            c = lax.broadcasted_iota(jnp.int32, (tq, tq), 1)
            for t in range(2):
                rel = r - c + t * tq
                vals = [jnp.full((tq, tq), tab_ref[0, head_off + h], jnp.float32)
                        for h in range(N_HEADS)]
                for b, first in enumerate(_t5_thresholds(), start=1):
                    hit = rel >= first
                    vals = [jnp.where(hit, tab_ref[b, head_off + h], vals[h]) for h in range(N_HEADS)]
                for h in range(N_HEADS):
                    band_ref[t, h] = vals[h]

    m_ref[...] = jnp.full(m_ref.shape, NEG, jnp.float32)
    acc_ref[...] = jnp.zeros(acc_ref.shape, jnp.float32)
    def chunk(off, kind, tk):
        ones = jnp.ones((tk, LANE), jnp.bfloat16)
        if mode == "dsa":
            keep = mask_ref[0, :, pl.ds(off, tk)].astype(jnp.float32) > 0.5
        elif kind == "diag":
            keep = (lax.broadcasted_iota(jnp.int32, (tq, tq), 0)
                    >= lax.broadcasted_iota(jnp.int32, (tq, tq), 1))
        else:
            keep = None
        units = [(h, mp) for h in range(N_HEADS) for mp in range(nmap)]
        logits = []
        for h, mp in units:
            cols = slice(h * HEAD_DIM + mp * dq, h * HEAD_DIM + (mp + 1) * dq)
            logits.append(lax.dot_general(q_ref[0, :, cols], k_ref[0, pl.ds(off, tk), cols],
                                          (((1,), (1,)), ((), ())),
                                          preferred_element_type=jnp.float32))
        m_olds, m_news = [], []
        for u, (h, mp) in enumerate(units):
            s = logits[u] * scale
            if biased:
                if kind == "far":
                    s = s + tab_ref[N_BUCKETS - 1, head_off + h]
                else:
                    s = s + band_ref[1 if kind == "prev" else 0, h]
            else:
                s = s + fc_ref[0, :, COL_FD + h:COL_FD + h + 1] - fr_ref[0, h:h + 1, pl.ds(off, tk)]
            if keep is not None:
                s = jnp.where(keep, s, NEG)
            logits[u] = s
            m_old = m_ref[u]
            m_olds.append(m_old)
            m_news.append(jnp.maximum(m_old, jnp.max(s, axis=1, keepdims=True)))
        probs = []
        for u in range(len(units)):
            probs.append(jnp.exp(logits[u] - jnp.tile(m_news[u], (1, tk // LANE))).astype(v_ref.dtype))
            m_ref[u] = m_news[u]
        for u, (h, mp) in enumerate(units):
            vc = jnp.concatenate(
                [v_ref[0, pl.ds(off, tk), h * HEAD_DIM:(h + 1) * HEAD_DIM], ones], axis=1)
            alpha = jnp.exp(m_olds[u] - m_news[u])
            acc_ref[u] = (jnp.tile(alpha, (1, 2)) * acc_ref[u]
                          + jnp.dot(probs[u], vc, preferred_element_type=jnp.float32))

    n_far = jnp.maximum(i - 1, 0) if biased else i

    def far_body(c, carry):
        chunk(pl.multiple_of(c * 2 * tq, 2 * tq), "far", 2 * tq)
        return carry

    lax.fori_loop(0, n_far // 2, far_body, 0)

    @pl.when(n_far % 2 == 1)
    def _():
        chunk(pl.multiple_of((n_far - 1) * tq, tq), "far", tq)

    if biased:
        @pl.when(i >= 1)
        def _():
            chunk(pl.multiple_of((i - 1) * tq, tq), "prev", tq)
    chunk(pl.multiple_of(i * tq, tq), "diag", tq)

    if mode == "diff":
        lq = lam_ref[...]
        lam_val = (jnp.exp(jnp.sum(lq[0:1] * lq[1:2], axis=1, keepdims=True))
                   - jnp.exp(jnp.sum(lq[2:3] * lq[3:4], axis=1, keepdims=True)) + lam_init)
    for h in range(N_HEADS):
        a = acc_ref[h * nmap]
        o = a[:, :HEAD_DIM] / a[:, HEAD_DIM:]
        if mode == "diff":
            a1 = acc_ref[h * nmap + 1]
            o = o - lam_val * (a1[:, :HEAD_DIM] / a1[:, HEAD_DIM:])
            o = o * lax.rsqrt(jnp.mean(o * o, axis=-1, keepdims=True) + EPS) * gs_ref[...]
            o = o * (1.0 - lam_init)
        o_ref[0, :, h * HEAD_DIM:(h + 1) * HEAD_DIM] = o.astype(o_ref.dtype)


def _attention(main, mode, tq, *, mask=None, tab=None, lam=None, g_subln=None,
               fcol=None, frow=None, lam_init=0.0):
    Bn, S, _ = main.shape
    blk_q, blk_k, blk_v, head_off = {
        "dsa": (BLK_QA, BLK_KA, BLK_VA, 0),
        "diff": (BLK_QB, BLK_KB, BLK_VB, N_HEADS),
        "fox": (BLK_QD, BLK_KD, BLK_VD, 0),
    }[mode]
    nmap = 2 if mode == "diff" else 1
    W = N_HEADS * HEAD_DIM
    args = [main, main, main]
    specs = [pl.BlockSpec((1, tq, W), lambda b, i: (b, i, blk_q // N_HEADS)),
             pl.BlockSpec((1, S, W), lambda b, i: (b, 0, blk_k // N_HEADS)),
             pl.BlockSpec((1, S, W), lambda b, i: (b, 0, blk_v // N_HEADS))]
    scratch = [pltpu.VMEM((N_HEADS * nmap, tq, LANE), jnp.float32),
               pltpu.VMEM((N_HEADS * nmap, tq, 2 * HEAD_DIM), jnp.float32)]
    if mode == "dsa":
        args.append(mask)
        specs.append(pl.BlockSpec((1, tq, S), lambda b, i: (b, i, 0)))
    if mode in ("dsa", "diff"):
        args.append(tab)
        specs.append(pl.BlockSpec(memory_space=pltpu.SMEM))
        scratch.append(pltpu.VMEM((2, N_HEADS, tq, tq), jnp.float32))
    if mode == "diff":
        args += [lam, g_subln.reshape(1, HEAD_DIM)]
        specs += [pl.BlockSpec((4, DIFF_DIM), lambda b, i: (0, 0)),
                  pl.BlockSpec((1, HEAD_DIM), lambda b, i: (0, 0))]
    if mode == "fox":
        args += [fcol, frow]
        specs += [pl.BlockSpec((1, tq, LANE), lambda b, i: (b, i, 0)),
                  pl.BlockSpec((1, N_HEADS, S), lambda b, i: (b, 0, 0))]
    return pl.pallas_call(
        functools.partial(_attn_kernel, mode=mode, head_off=head_off, lam_init=lam_init),
        grid=(Bn, S // tq),
        in_specs=specs,
        out_specs=pl.BlockSpec((1, tq, W), lambda b, i: (b, i, 0)),
        out_shape=jax.ShapeDtypeStruct((Bn, S, MIX_W), jnp.bfloat16),
        scratch_shapes=scratch,
        name="attn_" + mode,
        compiler_params=_cparams(("parallel", "arbitrary")),
    )(*args)


def _conv_kernel(bg_ref, cg_ref, xin_ref, w_ref, o_ref):
    u = cg_ref[0] * xin_ref[0]
    row = lax.broadcasted_iota(jnp.int32, u.shape, 0)
    w = w_ref[...]
    y = u * w[CONV_K - 1:CONV_K, :]
    for d in range(1, CONV_K):
        shifted = jnp.where(row >= d, pltpu.roll(u, d, 0), 0.0)
        y = y + shifted * w[CONV_K - 1 - d:CONV_K - d, :]
    o_ref[0] = (bg_ref[0] * y).astype(o_ref.dtype)


def _short_conv(f32grp, conv_w):
    Bn, S, _ = f32grp.shape
    tc = LANE
    per = CONV_CH // tc
    return pl.pallas_call(
        _conv_kernel,
        grid=(Bn, per),
        in_specs=[pl.BlockSpec((1, S, tc), lambda b, j: (b, 0, j)),
                  pl.BlockSpec((1, S, tc), lambda b, j: (b, 0, per + j)),
                  pl.BlockSpec((1, S, tc), lambda b, j: (b, 0, 2 * per + j)),
                  pl.BlockSpec((CONV_K, tc), lambda b, j: (0, j))],
        out_specs=pl.BlockSpec((1, S, tc), lambda b, j: (b, 0, j)),
        out_shape=jax.ShapeDtypeStruct((Bn, S, CONV_CH), jnp.bfloat16),
        name="short_conv",
        compiler_params=_cparams(("parallel", "parallel")),
    )(f32grp, f32grp, f32grp, conv_w)


def _mix_kernel(*refs):
    h_ref = refs[0]
    y_refs = refs[1:1 + N_BRANCH]
    wg_refs = refs[1 + N_BRANCH:1 + 2 * N_BRANCH]
    bg_refs = refs[1 + 2 * N_BRANCH:1 + 3 * N_BRANCH]
    wb_refs = refs[1 + 3 * N_BRANCH:1 + 4 * N_BRANCH]
    o_ref = refs[1 + 4 * N_BRANCH]
    h = h_ref[...]
    merged = None
    for g in range(N_BRANCH):
        gate = jax.nn.sigmoid(jnp.dot(h, wg_refs[g][...], preferred_element_type=jnp.float32)
                              + bg_refs[g][...])
        proj = jnp.dot(y_refs[g][...], wb_refs[g][...], preferred_element_type=jnp.float32)
        merged = gate * proj if merged is None else merged + gate * proj
    o_ref[...] = merged.astype(o_ref.dtype)


def _mix(h, ys, w_gate, b_gate, w_branch, l):
    T, D = h.shape
    tm = _tile(T, 1024)
    tn = _tile(D, 512)
    per = D // tn
    specs = [pl.BlockSpec((tm, D), lambda i, n: (i, 0))]
    specs += [pl.BlockSpec((tm, MIX_W), lambda i, n: (i, 0)) for _ in range(N_BRANCH)]
    specs += [pl.BlockSpec((None, D, tn), lambda i, n, g=g: (l, 0, g * per + n))
              for g in range(N_BRANCH)]
    specs += [pl.BlockSpec((None, 1, tn), lambda i, n, g=g: (l, 0, g * per + n))
              for g in range(N_BRANCH)]
    specs += [pl.BlockSpec((None, None, MIX_W, tn), lambda i, n, g=g: (l, g, 0, n))
              for g in range(N_BRANCH)]
    return pl.pallas_call(
        _mix_kernel,
        grid=(T // tm, per),
        in_specs=specs,
        out_specs=pl.BlockSpec((tm, tn), lambda i, n: (i, n)),
        out_shape=jax.ShapeDtypeStruct((T, D), jnp.bfloat16),
        name="gated_mix",
        compiler_params=_cparams(("parallel", "parallel")),
    )(h, *ys, *([w_gate] * N_BRANCH), *([b_gate] * N_BRANCH), *([w_branch] * N_BRANCH))


def _ffn_kernel(h_ref, w1_ref, w2_ref, o_ref, *, sub):
    f = pl.program_id(1)
    h = h_ref[...]
    part = None
    for j in range(w1_ref.shape[1] // sub):
        u = jnp.dot(h, w1_ref[:, j * sub:(j + 1) * sub], preferred_element_type=jnp.float32)
        u = jnp.square(jnp.maximum(u, 0.0)).astype(jnp.bfloat16)
        pj = jnp.dot(u, w2_ref[j * sub:(j + 1) * sub, :], preferred_element_type=jnp.float32)
        part = pj if part is None else part + pj

    @pl.when(f == 0)
    def _():
        o_ref[...] = part

    @pl.when(f > 0)
    def _():
        o_ref[...] += part


def _ffn(h, w1, w2, l):
    T, D = h.shape
    F = w1.shape[2]
    tm = _tile(T, 1024)
    tf = _tile(F, 1024)
    return pl.pallas_call(
        functools.partial(_ffn_kernel, sub=_tile(tf, 512)),
        grid=(T // tm, F // tf),
        in_specs=[pl.BlockSpec((tm, D), lambda i, f: (i, 0)),
                  pl.BlockSpec((None, D, tf), lambda i, f: (l, 0, f)),
                  pl.BlockSpec((None, tf, D), lambda i, f: (l, f, 0))],
        out_specs=pl.BlockSpec((tm, D), lambda i, f: (i, 0)),
        out_shape=jax.ShapeDtypeStruct((T, D), jnp.float32),
        name="ffn",
        compiler_params=_cparams(("parallel", "arbitrary")),
    )(h, w1, w2)


def _split_in_proj(w_in):
    W = N_HEADS * HEAD_DIM
    sizes = (3 * W,
             IDX_HEADS * IDX_DIM + IDX_DIM,
             IDX_HEADS,
             3 * W,
             3 * CONV_CH,
             3 * W,
             N_HEADS)
    offs = np.concatenate([[0], np.cumsum(sizes)])
    a, idx, iw, b, cv, d, fd = [w_in[:, :, offs[j]:offs[j + 1]] for j in range(7)]
    L, D, _ = w_in.shape
    zeros = lambda n: jnp.zeros((L, D, n), w_in.dtype)
    main = jnp.concatenate([a, b, d, idx, zeros(N_MAIN - 9 * W - sizes[1])], axis=-1)
    f32grp = jnp.concatenate([cv, iw, fd, zeros(LANE - IDX_HEADS - N_HEADS)], axis=-1)
    return main.astype(jnp.bfloat16), f32grp.astype(jnp.bfloat16)


def kernel(x, c, w_ada, b_ada, g_norm1, w_in, b_f, conv_w, lam, g_subln, rel_bias,
           w_gate, b_gate, w_branch, w_o, g_norm2, w_ff1, w_ff2, g_final):
    Bn, S, D = x.shape
    L = w_ada.shape[0]
    T = Bn * S
    k_top = min(TOPK_MAX, S // 4)
    tq = _tile(S, 256)
    bf = jnp.bfloat16

    w_main, w_f32 = _split_in_proj(w_in)
    w_gate_b, w_branch_b, w_o_b = w_gate.astype(bf), w_branch.astype(bf), w_o.astype(bf)
    w_ff1_b, w_ff2_b = w_ff1.astype(bf), w_ff2.astype(bf)

    mod = _modulation(c, w_ada, b_ada).reshape(L, Bn, 6, 1, D)
    xt = x.reshape(T, D)
    delta, gate = None, None
    for l in range(L):
        sh1, sc1, gt1, sh2, sc2, gt2 = [mod[l, :, j] for j in range(6)]
        x_new, h = _norm(xt, g_norm1[l], seq=S, delta=delta, gate=gate, sc=sc1, sh=sh1)
        xt = xt if x_new is None else x_new
        main = _mm(h, w_main, l, bf, 1024).reshape(Bn, S, N_MAIN)
        f32grp = _mm(h, w_f32, l, jnp.float32, N_F32).reshape(Bn, S, N_F32)

        b_pad = jnp.zeros((1, LANE), jnp.float32).at[0, COL_FD:COL_FD + N_HEADS].set(b_f[l])
        fcol = _forget_cumsum(f32grp, b_pad)
        frow = jnp.transpose(fcol[:, :, COL_FD:COL_FD + N_HEADS], (0, 2, 1))

        mask = _select(main, f32grp, k_top, tq)
        lam_init = 0.8 - 0.6 * math.exp(-0.3 * l)
        y_a = _attention(main, "dsa", tq, mask=mask, tab=rel_bias)
        y_b = _attention(main, "diff", tq, tab=rel_bias, lam=lam[l], g_subln=g_subln[l],
                         lam_init=lam_init)
        y_c = _short_conv(f32grp, conv_w[l])
        y_d = _attention(main, "fox", tq, fcol=fcol, frow=frow)
        ys = [y.reshape(T, MIX_W) for y in (y_a, y_b, y_c, y_d)]

        merged = _mix(h, ys, w_gate_b, b_gate.reshape(L, 1, N_BRANCH * D), w_branch_b, l)
        delta = _mm(merged, w_o_b, l, jnp.float32, 1024)
        xt, h2 = _norm(xt, g_norm2[l], seq=S, delta=delta, gate=gt1, sc=sc2, sh=sh2)
        delta = _ffn(h2, w_ff1_b, w_ff2_b, l)
        gate = gt2
    _, out = _norm(xt, g_final, seq=S, delta=delta, gate=gate, out_dtype=jnp.float32,
                   keep_x=False)
    return out.reshape(Bn, S, D)
```

```python
import functools
import math

import numpy as np
import jax
import jax.numpy as jnp
from jax import lax
from jax.experimental import pallas as pl
from jax.experimental.pallas import tpu as pltpu

HEAD_DIM = 128
N_HEADS = 4
DIFF_DIM = HEAD_DIM // 2
CONV_CH = 512
CONV_K = 3
IDX_HEADS = 4
IDX_DIM = 64
TOPK_MAX = 256
MIX_W = 512
N_BRANCH = 4
N_BUCKETS = 32
MAX_DIST = 128
EPS = 1e-6
NEG = -1e30
IDX_SCALE = (IDX_DIM ** -0.5) * (IDX_HEADS ** -0.5)
INT_MIN = -(2 ** 31)

LANE = 128
VMEM_LIMIT = 56 * 1024 * 1024

BLK_QA, BLK_KA, BLK_VA = 0, 4, 8
BLK_QB, BLK_KB, BLK_VB = 12, 16, 20
BLK_QD, BLK_KD, BLK_VD = 24, 28, 32
BLK_IQ = 36
BLK_IK = 38
N_MAIN = 40 * LANE
N_F32 = 3 * CONV_CH + LANE
BLK_SMALL = 3 * CONV_CH // LANE
COL_IW, COL_FD = 0, 4


def _cparams(sem):
    return pltpu.CompilerParams(dimension_semantics=sem, vmem_limit_bytes=VMEM_LIMIT)


def _tile(n, pref):
    t = min(n, pref)
    while n % t:
        t -= 1
    return t


def _mod_kernel(c_ref, w_ref, b_ref, o_ref):
    c = c_ref[...]
    ca = (c * jax.nn.sigmoid(c)).astype(jnp.bfloat16)
    w = w_ref[0].astype(jnp.bfloat16)
    o_ref[0] = jnp.dot(ca, w, preferred_element_type=jnp.float32) + b_ref[0]


def _modulation(c, w_ada, b_ada):
    L, D, N = w_ada.shape
    Bn = c.shape[0]
    tn = _tile(N, 1024)
    return pl.pallas_call(
        _mod_kernel,
        grid=(L, N // tn),
        in_specs=[
            pl.BlockSpec((Bn, D), lambda l, j: (0, 0)),
            pl.BlockSpec((1, D, tn), lambda l, j: (l, 0, j)),
            pl.BlockSpec((1, 1, tn), lambda l, j: (l, 0, j)),
        ],
        out_specs=pl.BlockSpec((1, Bn, tn), lambda l, j: (l, 0, j)),
        out_shape=jax.ShapeDtypeStruct((L, Bn, N), jnp.float32),
        name="ada_mod",
        compiler_params=_cparams(("parallel", "parallel")),
    )(c, w_ada, b_ada.reshape(L, 1, N))


def _norm_kernel(*refs, has_delta, modulate, keep_x):
    it = iter(refs)
    x_ref = next(it)
    d_ref = next(it) if has_delta else None
    gate_ref = next(it) if has_delta else None
    g_ref = next(it)
    sc_ref = next(it) if modulate else None
    sh_ref = next(it) if modulate else None
    xo_ref = next(it) if keep_x else None
    h_ref = next(it)

    x = x_ref[...]
    if has_delta:
        x = x + gate_ref[0] * d_ref[...]
    if keep_x:
        xo_ref[...] = x
    y = x * lax.rsqrt(jnp.mean(x * x, axis=-1, keepdims=True) + EPS) * g_ref[...]
    if modulate:
        y = y * (1.0 + sc_ref[0]) + sh_ref[0]
    h_ref[...] = y.astype(h_ref.dtype)


def _norm(x, g, *, seq, delta=None, gate=None, sc=None, sh=None, out_dtype=jnp.bfloat16,
          keep_x=True):
    T, D = x.shape
    tm = _tile(seq, 512)
    per_seq = seq // tm
    has_delta = delta is not None
    keep_x = keep_x and has_delta
    modulate = sc is not None
    row = pl.BlockSpec((tm, D), lambda i: (i, 0))
    per_b = pl.BlockSpec((1, 1, D), lambda i: (i // per_seq, 0, 0))
    args, specs = [x], [row]
    if has_delta:
        args += [delta, gate]
        specs += [row, per_b]
    args.append(g.reshape(1, D))
    specs.append(pl.BlockSpec((1, D), lambda i: (0, 0)))
    if modulate:
        args += [sc, sh]
        specs += [per_b, per_b]
    out_shape, out_specs = [], []
    if keep_x:
        out_shape.append(jax.ShapeDtypeStruct((T, D), jnp.float32))
        out_specs.append(row)
    out_shape.append(jax.ShapeDtypeStruct((T, D), out_dtype))
    out_specs.append(row)
    res = pl.pallas_call(
        functools.partial(_norm_kernel, has_delta=has_delta, modulate=modulate, keep_x=keep_x),
        grid=(T // tm,),
        in_specs=specs,
        out_specs=out_specs,
        out_shape=out_shape,
        name="norm",
        compiler_params=_cparams(("parallel",)),
    )(*args)
    if keep_x:
        return res[0], res[1]
    return None, res[0]


def _rms_mod(x, g, sc, sh):
    y = x * lax.rsqrt(jnp.mean(x * x, axis=-1, keepdims=True) + EPS) * g
    return y * (1.0 + sc) + sh


def _norm_mm_kernel(*refs, has_delta):
    it = iter(refs)
    x_ref = next(it)
    d_ref = next(it) if has_delta else None
    gate_ref = next(it) if has_delta else None
    g_ref, sc_ref, sh_ref, w_ref = next(it), next(it), next(it), next(it)
    xo_ref = next(it) if has_delta else None
    h_ref, o_ref = next(it), next(it)

    @pl.when(pl.program_id(1) == 0)
    def _():
        x = x_ref[...]
        if has_delta:
            x = x + gate_ref[0] * d_ref[...]
            xo_ref[...] = x
        h_ref[...] = _rms_mod(x, g_ref[...], sc_ref[0], sh_ref[0]).astype(h_ref.dtype)

    o_ref[...] = jnp.dot(h_ref[...], w_ref[...],
                         preferred_element_type=jnp.float32).astype(o_ref.dtype)


def _norm_mm(x, g, sc, sh, w, l, *, seq, delta=None, gate=None):
    T, D = x.shape
    N = w.shape[2]
    tm = _tile(seq, 512)
    tn = _tile(N, 1024)
    per_seq = seq // tm
    has_delta = delta is not None
    row = pl.BlockSpec((tm, D), lambda i, j: (i, 0))
    per_b = pl.BlockSpec((1, 1, D), lambda i, j: (i // per_seq, 0, 0))
    args, specs = [x], [row]
    if has_delta:
        args += [delta, gate]
        specs += [row, per_b]
    args += [g.reshape(1, D), sc, sh, w]
    specs += [pl.BlockSpec((1, D), lambda i, j: (0, 0)), per_b, per_b,
              pl.BlockSpec((None, D, tn), lambda i, j: (l, 0, j))]
    out_shape, out_specs = [], []
    if has_delta:
        out_shape.append(jax.ShapeDtypeStruct((T, D), jnp.float32))
        out_specs.append(row)
    out_shape += [jax.ShapeDtypeStruct((T, D), jnp.bfloat16),
                  jax.ShapeDtypeStruct((T, N), jnp.bfloat16)]
    out_specs += [row, pl.BlockSpec((tm, tn), lambda i, j: (i, j))]
    res = pl.pallas_call(
        functools.partial(_norm_mm_kernel, has_delta=has_delta),
        grid=(T // tm, N // tn),
        in_specs=specs,
        out_specs=out_specs,
        out_shape=out_shape,
        name="norm_inproj",
        compiler_params=_cparams(("parallel", "arbitrary")),
    )(*args)
    return (res[0], res[1], res[2]) if has_delta else (None, res[0], res[1])


def _mm_norm_kernel(a_ref, w_ref, x_ref, gate_ref, g_ref, sc_ref, sh_ref, xo_ref, h_ref):
    acc = jnp.dot(a_ref[...], w_ref[...], preferred_element_type=jnp.float32)
    x = x_ref[...] + gate_ref[0] * acc
    xo_ref[...] = x
    h_ref[...] = _rms_mod(x, g_ref[...], sc_ref[0], sh_ref[0]).astype(h_ref.dtype)


def _mm_norm(a, w, l, x, gate, g, sc, sh, *, seq):
    T, D = x.shape
    K = a.shape[1]
    tm = _tile(seq, 512)
    per_seq = seq // tm
    row = pl.BlockSpec((tm, D), lambda i: (i, 0))
    per_b = pl.BlockSpec((1, 1, D), lambda i: (i // per_seq, 0, 0))
    return pl.pallas_call(
        _mm_norm_kernel,
        grid=(T // tm,),
        in_specs=[pl.BlockSpec((tm, K), lambda i: (i, 0)),
                  pl.BlockSpec((None, K, D), lambda i: (l, 0, 0)),
                  row, per_b, pl.BlockSpec((1, D), lambda i: (0, 0)), per_b, per_b],
        out_specs=[row, row],
        out_shape=[jax.ShapeDtypeStruct((T, D), jnp.float32),
                   jax.ShapeDtypeStruct((T, D), jnp.bfloat16)],
        name="outproj_norm",
        compiler_params=_cparams(("parallel",)),
    )(a, w, x, gate, g.reshape(1, D), sc, sh)


def _mm_kernel(a_ref, w_ref, o_ref):
    o_ref[...] = jnp.dot(a_ref[...], w_ref[...],
                         preferred_element_type=jnp.float32).astype(o_ref.dtype)


def _mm(a, w, l, out_dtype, tn_pref):
    M, K = a.shape
    N = w.shape[2]
    tm = _tile(M, 1024)
    tn = _tile(N, tn_pref)
    return pl.pallas_call(
        _mm_kernel,
        grid=(M // tm, N // tn),
        in_specs=[pl.BlockSpec((tm, K), lambda i, j: (i, 0)),
                  pl.BlockSpec((None, K, tn), lambda i, j: (l, 0, j))],
        out_specs=pl.BlockSpec((tm, tn), lambda i, j: (i, j)),
        out_shape=jax.ShapeDtypeStruct((M, N), out_dtype),
        name="matmul",
        compiler_params=_cparams(("parallel", "parallel")),
    )(a, w)


def _cumsum_kernel(s_ref, b_ref, o_ref):
    S = s_ref.shape[1]
    x = s_ref[0] + b_ref[...]
    ls = jnp.minimum(x, 0.0) - jnp.log(1.0 + jnp.exp(-jnp.abs(x)))
    r = lax.broadcasted_iota(jnp.int32, (LANE, LANE), 0)
    c = lax.broadcasted_iota(jnp.int32, (LANE, LANE), 1)
    tri = jnp.where(r >= c, 1.0, 0.0).astype(jnp.bfloat16)
    hi = ls.astype(jnp.bfloat16)
    r1 = ls - hi.astype(jnp.float32)
    mid = r1.astype(jnp.bfloat16)
    lo = (r1 - mid.astype(jnp.float32)).astype(jnp.bfloat16)
    carry = jnp.zeros((1, LANE), jnp.float32)
    for j in range(S // LANE):
        rows = slice(j * LANE, (j + 1) * LANE)
        blk = (jnp.dot(tri, hi[rows], preferred_element_type=jnp.float32)
               + jnp.dot(tri, mid[rows], preferred_element_type=jnp.float32)
               + jnp.dot(tri, lo[rows], preferred_element_type=jnp.float32)) + carry
        o_ref[0, j * LANE:(j + 1) * LANE, :] = blk
        carry = blk[LANE - 1:LANE, :]


def _forget_cumsum(small, b_pad):
    Bn, S, _ = small.shape
    return pl.pallas_call(
        _cumsum_kernel,
        grid=(Bn,),
        in_specs=[pl.BlockSpec((1, S, LANE), lambda b: (b, 0, BLK_SMALL)),
                  pl.BlockSpec((1, LANE), lambda b: (0, 0))],
        out_specs=pl.BlockSpec((1, S, LANE), lambda b: (b, 0, 0)),
        out_shape=jax.ShapeDtypeStruct((Bn, S, LANE), jnp.float32),
        name="forget_cumsum",
        compiler_params=_cparams(("parallel",)),
    )(small, b_pad)


def _select_block(iq_ref, ik_ref, sm_ref, mask_ref, key_ref, *, k_top, blk):
    tq = iq_ref.shape[1]
    S = ik_ref.shape[1]
    W = (blk + 1) * tq
    iq = iq_ref[0]
    ik = ik_ref[0, :W, :IDX_DIM]
    iw = sm_ref[0][:, COL_IW:COL_IW + IDX_HEADS] * IDX_SCALE
    score = jnp.zeros((tq, W), jnp.float32)
    for h in range(IDX_HEADS):
        d = lax.dot_general(iq[:, h * IDX_DIM:(h + 1) * IDX_DIM], ik,
                            (((1,), (1,)), ((), ())), preferred_element_type=jnp.float32)
        score = score + jnp.where(d > 0.0, d * iw[:, h:h + 1], 0.0)
    causal = (lax.broadcasted_iota(jnp.int32, (tq, W), 1)
              <= blk * tq + lax.broadcasted_iota(jnp.int32, (tq, W), 0))
    score = jnp.where(causal, score, NEG)
    bits = lax.bitcast_convert_type(score, jnp.int32)
    key_ref[:, :W] = bits ^ ((bits >> 31) & 0x7FFFFFFF)
    kf = jnp.float32(k_top)

    def count_ge(cand):
        return jnp.sum(jnp.where(key_ref[:, :W] >= cand, 1.0, 0.0), axis=1, keepdims=True)

    thr = jnp.where(count_ge(jnp.zeros((tq, 1), jnp.int32)) >= kf, 0, INT_MIN).astype(jnp.int32)

    def value_bit(it, thr):
        cand = thr | jnp.left_shift(jnp.int32(1), 30 - it)
        return jnp.where(count_ge(cand) >= kf, cand, thr)

    thr = lax.fori_loop(0, 31, value_bit, thr)

    key = key_ref[:, :W]
    above = key > thr
    need = kf - jnp.sum(jnp.where(above, 1.0, 0.0), axis=1, keepdims=True)
    upper = jnp.where(lax.broadcasted_iota(jnp.int32, (tq, tq), 0)
                      <= lax.broadcasted_iota(jnp.int32, (tq, tq), 1), 1.0, 0.0).astype(jnp.bfloat16)
    seen = jnp.zeros((tq, 1), jnp.float32)
    for c in range(blk + 1):
        cs = slice(c * tq, (c + 1) * tq)
        tie = key[:, cs] == thr
        incl = seen + jnp.dot(jnp.where(tie, 1.0, 0.0).astype(jnp.bfloat16), upper,
                              preferred_element_type=jnp.float32)
        sel = above[:, cs] | (tie & (incl <= need))
        if c == blk:
            sel = sel & causal[:, cs]
        mask_ref[0, :, cs] = jnp.where(sel, 1.0, 0.0).astype(mask_ref.dtype)
        seen = incl[:, tq - 1:tq]
    if W < S:
        mask_ref[0, :, W:] = jnp.zeros((tq, S - W), mask_ref.dtype)


def _select_kernel(iq_ref, ik_ref, sm_ref, mask_ref, key_ref, *, k_top):
    i = pl.program_id(1)
    for blk in range(ik_ref.shape[1] // iq_ref.shape[1]):
        @pl.when(i == blk)
        def _(blk=blk):
            _select_block(iq_ref, ik_ref, sm_ref, mask_ref, key_ref, k_top=k_top, blk=blk)


def _select(main, f32grp, k_top, tq):
    Bn, S, _ = main.shape
    return pl.pallas_call(
        functools.partial(_select_kernel, k_top=k_top),
        grid=(Bn, S // tq),
        in_specs=[pl.BlockSpec((1, tq, 2 * LANE), lambda b, i: (b, i, BLK_IQ // 2)),
                  pl.BlockSpec((1, S, LANE), lambda b, i: (b, 0, BLK_IK)),
                  pl.BlockSpec((1, tq, LANE), lambda b, i: (b, i, BLK_SMALL))],
        out_specs=pl.BlockSpec((1, tq, S), lambda b, i: (b, i, 0)),
        out_shape=jax.ShapeDtypeStruct((Bn, S, S), jnp.bfloat16),
        scratch_shapes=[pltpu.VMEM((tq, S), jnp.int32)],
        name="dsa_select",
        compiler_params=_cparams(("parallel", "parallel")),
    )(main, main, f32grp)


def _t5_thresholds():
    max_exact = N_BUCKETS // 2
    rel = np.arange(MAX_DIST)
    nf = np.maximum(rel, 1).astype(np.float64)
    large = max_exact + (np.log(nf / max_exact) / math.log(MAX_DIST / max_exact)
                         * (N_BUCKETS - max_exact)).astype(np.int64)
    bucket = np.where(rel < max_exact, rel, np.minimum(large, N_BUCKETS - 1))
    out = []
    for b in range(1, N_BUCKETS):
        hit = np.nonzero(bucket >= b)[0]
        out.append(int(hit[0]) if hit.size else MAX_DIST)
    return out


def _attn_kernel(*refs, mode, head_off, lam_init):
    it = iter(refs)
    q_ref, k_ref, v_ref = next(it), next(it), next(it)
    if mode == "dsa":
        mask_ref = next(it)
    if mode in ("dsa", "diff"):
        tab_ref = next(it)
    if mode == "diff":
        lam_ref, gs_ref = next(it), next(it)
    if mode == "fox":
        fc_ref, fr_ref = next(it), next(it)
    o_ref = next(it)
    m_ref, acc_ref = next(it), next(it)
    if mode in ("dsa", "diff"):
        band_ref = next(it)

    i = pl.program_id(1)
    tq = q_ref.shape[1]
    nmap = 2 if mode == "diff" else 1
    dq = HEAD_DIM // nmap
    scale = dq ** -0.5
    biased = mode in ("dsa", "diff")

    if biased:
        @pl.when(i == 0)
        def _():
            r = lax.broadcasted_iota(jnp.int32, (tq, tq), 0)
            c = lax.broadcasted_iota(jnp.int32, (tq, tq), 1)
            for t in range(2):
                rel = r - c + t * tq
                vals = [jnp.full((tq, tq), tab_ref[0, head_off + h], jnp.float32)
                        for h in range(N_HEADS)]
                for b, first in enumerate(_t5_thresholds(), start=1):
                    hit = rel >= first
                    vals = [jnp.where(hit, tab_ref[b, head_off + h], vals[h]) for h in range(N_HEADS)]
                for h in range(N_HEADS):
                    band_ref[t, h] = vals[h]

    m_ref[...] = jnp.full(m_ref.shape, NEG, jnp.float32)
    acc_ref[...] = jnp.zeros(acc_ref.shape, jnp.float32)
    def chunk(off, kind, tk):
        ones = jnp.ones((tk, LANE), jnp.bfloat16)
        if mode == "dsa":
            keep = mask_ref[0, :, pl.ds(off, tk)].astype(jnp.float32) > 0.5
        elif kind == "diag":
            keep = (lax.broadcasted_iota(jnp.int32, (tq, tq), 0)
                    >= lax.broadcasted_iota(jnp.int32, (tq, tq), 1))
        else:
            keep = None
        units = [(h, mp) for h in range(N_HEADS) for mp in range(nmap)]
        logits = []
        for h, mp in units:
            cols = slice(h * HEAD_DIM + mp * dq, h * HEAD_DIM + (mp + 1) * dq)
            logits.append(lax.dot_general(q_ref[0, :, cols], k_ref[0, pl.ds(off, tk), cols],
                                          (((1,), (1,)), ((), ())),
                                          preferred_element_type=jnp.float32))
        m_olds, m_news = [], []
        for u, (h, mp) in enumerate(units):
            s = logits[u] * scale
            if biased:
                if kind == "far":
                    s = s + tab_ref[N_BUCKETS - 1, head_off + h]
                else:
                    s = s + band_ref[1 if kind == "prev" else 0, h]
            else:
                s = s + fc_ref[0, :, COL_FD + h:COL_FD + h + 1] - fr_ref[0, h:h + 1, pl.ds(off, tk)]
            if keep is not None:
                s = jnp.where(keep, s, NEG)
            logits[u] = s
            m_old = m_ref[u]
            m_olds.append(m_old)
            m_news.append(jnp.maximum(m_old, jnp.max(s, axis=1, keepdims=True)))
        probs = []
        for u in range(len(units)):
            probs.append(jnp.exp(logits[u] - jnp.tile(m_news[u], (1, tk // LANE))).astype(v_ref.dtype))
            m_ref[u] = m_news[u]
        for u, (h, mp) in enumerate(units):
            vc = jnp.concatenate(
                [v_ref[0, pl.ds(off, tk), h * HEAD_DIM:(h + 1) * HEAD_DIM], ones], axis=1)
            alpha = jnp.exp(m_olds[u] - m_news[u])
            acc_ref[u] = (jnp.tile(alpha, (1, 2)) * acc_ref[u]
                          + jnp.dot(probs[u], vc, preferred_element_type=jnp.float32))

    n_far = jnp.maximum(i - 1, 0) if biased else i

    def far_body(c, carry):
        chunk(pl.multiple_of(c * 2 * tq, 2 * tq), "far", 2 * tq)
        return carry

    lax.fori_loop(0, n_far // 2, far_body, 0)

    @pl.when(n_far % 2 == 1)
    def _():
        chunk(pl.multiple_of((n_far - 1) * tq, tq), "far", tq)

    if biased:
        @pl.when(i >= 1)
        def _():
            chunk(pl.multiple_of((i - 1) * tq, tq), "prev", tq)
    chunk(pl.multiple_of(i * tq, tq), "diag", tq)

    if mode == "diff":
        lq = lam_ref[...]
        lam_val = (jnp.exp(jnp.sum(lq[0:1] * lq[1:2], axis=1, keepdims=True))
                   - jnp.exp(jnp.sum(lq[2:3] * lq[3:4], axis=1, keepdims=True)) + lam_init)
    for h in range(N_HEADS):
        a = acc_ref[h * nmap]
        o = a[:, :HEAD_DIM] / a[:, HEAD_DIM:]
        if mode == "diff":
            a1 = acc_ref[h * nmap + 1]
            o = o - lam_val * (a1[:, :HEAD_DIM] / a1[:, HEAD_DIM:])
            o = o * lax.rsqrt(jnp.mean(o * o, axis=-1, keepdims=True) + EPS) * gs_ref[...]
            o = o * (1.0 - lam_init)
        o_ref[0, :, h * HEAD_DIM:(h + 1) * HEAD_DIM] = o.astype(o_ref.dtype)


def _attention(main, mode, tq, *, mask=None, tab=None, lam=None, g_subln=None,
               fcol=None, frow=None, lam_init=0.0):
    Bn, S, _ = main.shape
    blk_q, blk_k, blk_v, head_off = {
        "dsa": (BLK_QA, BLK_KA, BLK_VA, 0),
        "diff": (BLK_QB, BLK_KB, BLK_VB, N_HEADS),
        "fox": (BLK_QD, BLK_KD, BLK_VD, 0),
    }[mode]
    nmap = 2 if mode == "diff" else 1
    W = N_HEADS * HEAD_DIM
    args = [main, main, main]
    specs = [pl.BlockSpec((1, tq, W), lambda b, i: (b, i, blk_q // N_HEADS)),
             pl.BlockSpec((1, S, W), lambda b, i: (b, 0, blk_k // N_HEADS)),
             pl.BlockSpec((1, S, W), lambda b, i: (b, 0, blk_v // N_HEADS))]
    scratch = [pltpu.VMEM((N_HEADS * nmap, tq, LANE), jnp.float32),
               pltpu.VMEM((N_HEADS * nmap, tq, 2 * HEAD_DIM), jnp.float32)]
    if mode == "dsa":
        args.append(mask)
        specs.append(pl.BlockSpec((1, tq, S), lambda b, i: (b, i, 0)))
    if mode in ("dsa", "diff"):
        args.append(tab)
        specs.append(pl.BlockSpec(memory_space=pltpu.SMEM))
        scratch.append(pltpu.VMEM((2, N_HEADS, tq, tq), jnp.float32))
    if mode == "diff":
        args += [lam, g_subln.reshape(1, HEAD_DIM)]
        specs += [pl.BlockSpec((4, DIFF_DIM), lambda b, i: (0, 0)),
                  pl.BlockSpec((1, HEAD_DIM), lambda b, i: (0, 0))]
    if mode == "fox":
        args += [fcol, frow]
        specs += [pl.BlockSpec((1, tq, LANE), lambda b, i: (b, i, 0)),
                  pl.BlockSpec((1, N_HEADS, S), lambda b, i: (b, 0, 0))]
    return pl.pallas_call(
        functools.partial(_attn_kernel, mode=mode, head_off=head_off, lam_init=lam_init),
        grid=(Bn, S // tq),
        in_specs=specs,
        out_specs=pl.BlockSpec((1, tq, W), lambda b, i: (b, i, 0)),
        out_shape=jax.ShapeDtypeStruct((Bn, S, MIX_W), jnp.bfloat16),
        scratch_shapes=scratch,
        name="attn_" + mode,
        compiler_params=_cparams(("parallel", "arbitrary")),
    )(*args)


def _conv_kernel(bg_ref, cg_ref, xin_ref, w_ref, o_ref):
    u = cg_ref[0] * xin_ref[0]
    row = lax.broadcasted_iota(jnp.int32, u.shape, 0)
    w = w_ref[...]
    y = u * w[CONV_K - 1:CONV_K, :]
    for d in range(1, CONV_K):
        shifted = jnp.where(row >= d, pltpu.roll(u, d, 0), 0.0)
        y = y + shifted * w[CONV_K - 1 - d:CONV_K - d, :]
    o_ref[0] = (bg_ref[0] * y).astype(o_ref.dtype)


def _short_conv(f32grp, conv_w):
    Bn, S, _ = f32grp.shape
    tc = LANE
    per = CONV_CH // tc
    return pl.pallas_call(
        _conv_kernel,
        grid=(Bn, per),
        in_specs=[pl.BlockSpec((1, S, tc), lambda b, j: (b, 0, j)),
                  pl.BlockSpec((1, S, tc), lambda b, j: (b, 0, per + j)),
                  pl.BlockSpec((1, S, tc), lambda b, j: (b, 0, 2 * per + j)),
                  pl.BlockSpec((CONV_K, tc), lambda b, j: (0, j))],
        out_specs=pl.BlockSpec((1, S, tc), lambda b, j: (b, 0, j)),
        out_shape=jax.ShapeDtypeStruct((Bn, S, CONV_CH), jnp.bfloat16),
        name="short_conv",
        compiler_params=_cparams(("parallel", "parallel")),
    )(f32grp, f32grp, f32grp, conv_w)


def _mix_kernel(*refs):
    h_ref = refs[0]
    y_refs = refs[1:1 + N_BRANCH]
    wg_refs = refs[1 + N_BRANCH:1 + 2 * N_BRANCH]
    bg_refs = refs[1 + 2 * N_BRANCH:1 + 3 * N_BRANCH]
    wb_refs = refs[1 + 3 * N_BRANCH:1 + 4 * N_BRANCH]
    o_ref = refs[1 + 4 * N_BRANCH]
    h = h_ref[...]
    merged = None
    for g in range(N_BRANCH):
        gate = jax.nn.sigmoid(jnp.dot(h, wg_refs[g][...], preferred_element_type=jnp.float32)
                              + bg_refs[g][...])
        proj = jnp.dot(y_refs[g][...], wb_refs[g][...], preferred_element_type=jnp.float32)
        merged = gate * proj if merged is None else merged + gate * proj
    o_ref[...] = merged.astype(o_ref.dtype)


def _mix(h, ys, w_gate, b_gate, w_branch, l):
    T, D = h.shape
    tm = _tile(T, 1024)
    tn = _tile(D, 512)
    per = D // tn
    specs = [pl.BlockSpec((tm, D), lambda i, n: (i, 0))]
    specs += [pl.BlockSpec((tm, MIX_W), lambda i, n: (i, 0)) for _ in range(N_BRANCH)]
    specs += [pl.BlockSpec((None, D, tn), lambda i, n, g=g: (l, 0, g * per + n))
              for g in range(N_BRANCH)]
    specs += [pl.BlockSpec((None, 1, tn), lambda i, n, g=g: (l, 0, g * per + n))
              for g in range(N_BRANCH)]
    specs += [pl.BlockSpec((None, None, MIX_W, tn), lambda i, n, g=g: (l, g, 0, n))
              for g in range(N_BRANCH)]
    return pl.pallas_call(
        _mix_kernel,
        grid=(T // tm, per),
        in_specs=specs,
        out_specs=pl.BlockSpec((tm, tn), lambda i, n: (i, n)),
        out_shape=jax.ShapeDtypeStruct((T, D), jnp.bfloat16),
        name="gated_mix",
        compiler_params=_cparams(("parallel", "parallel")),
    )(h, *ys, *([w_gate] * N_BRANCH), *([b_gate] * N_BRANCH), *([w_branch] * N_BRANCH))


def _ffn_kernel(h_ref, w1_ref, w2_ref, o_ref, u_ref, *, nf):
    f = pl.program_id(1)

    def first(slot):
        u = jnp.dot(h_ref[...], w1_ref[...], preferred_element_type=jnp.float32)
        u_ref[slot] = jnp.square(jnp.maximum(u, 0.0)).astype(u_ref.dtype)

    def second(slot):
        return jnp.dot(u_ref[slot], w2_ref[...], preferred_element_type=jnp.float32)

    @pl.when(f == 0)
    def _():
        o_ref[...] = jnp.zeros(o_ref.shape, o_ref.dtype)
        first(0)

    for parity in range(2):
        @pl.when((f > 0) & (f < nf) & (f % 2 == parity))
        def _(parity=parity):
            part = second(1 - parity)
            first(parity)
            o_ref[...] += part

    @pl.when(f == nf)
    def _():
        o_ref[...] += second((nf - 1) % 2)


def _ffn(h, w1, w2, l):
    T, D = h.shape
    F = w1.shape[2]
    tm = _tile(T, 1024)
    tf = _tile(F, 512)
    nf = F // tf
    return pl.pallas_call(
        functools.partial(_ffn_kernel, nf=nf),
        grid=(T // tm, nf + 1),
        in_specs=[pl.BlockSpec((tm, D), lambda i, f: (i, 0)),
                  pl.BlockSpec((None, D, tf), lambda i, f: (l, 0, jnp.minimum(f, nf - 1))),
                  pl.BlockSpec((None, tf, D), lambda i, f: (l, jnp.maximum(f - 1, 0), 0))],
        out_specs=pl.BlockSpec((tm, D), lambda i, f: (i, 0)),
        out_shape=jax.ShapeDtypeStruct((T, D), jnp.float32),
        scratch_shapes=[pltpu.VMEM((2, tm, tf), jnp.bfloat16)],
        name="ffn",
        compiler_params=_cparams(("parallel", "arbitrary")),
    )(h, w1, w2)


def _split_in_proj(w_in):
    W = N_HEADS * HEAD_DIM
    sizes = (3 * W,
             IDX_HEADS * IDX_DIM + IDX_DIM,
             IDX_HEADS,
             3 * W,
             3 * CONV_CH,
             3 * W,
             N_HEADS)
    offs = np.concatenate([[0], np.cumsum(sizes)])
    a, idx, iw, b, cv, d, fd = [w_in[:, :, offs[j]:offs[j + 1]] for j in range(7)]
    L, D, _ = w_in.shape
    zeros = lambda n: jnp.zeros((L, D, n), w_in.dtype)
    main = jnp.concatenate([a, b, d, idx, zeros(N_MAIN - 9 * W - sizes[1])], axis=-1)
    f32grp = jnp.concatenate([cv, iw, fd, zeros(LANE - IDX_HEADS - N_HEADS)], axis=-1)
    return main.astype(jnp.bfloat16), f32grp.astype(jnp.bfloat16)


def kernel(x, c, w_ada, b_ada, g_norm1, w_in, b_f, conv_w, lam, g_subln, rel_bias,
           w_gate, b_gate, w_branch, w_o, g_norm2, w_ff1, w_ff2, g_final):
    Bn, S, D = x.shape
    L = w_ada.shape[0]
    T = Bn * S
    k_top = min(TOPK_MAX, S // 4)
    tq = _tile(S, 256)
    bf = jnp.bfloat16

    w_main, w_f32 = _split_in_proj(w_in)
    w_gate_b, w_branch_b, w_o_b = w_gate.astype(bf), w_branch.astype(bf), w_o.astype(bf)
    w_ff1_b, w_ff2_b = w_ff1.astype(bf), w_ff2.astype(bf)

    mod = _modulation(c, w_ada, b_ada).reshape(L, Bn, 6, 1, D)
    xt = x.reshape(T, D)
    delta, gate = None, None
    for l in range(L):
        sh1, sc1, gt1, sh2, sc2, gt2 = [mod[l, :, j] for j in range(6)]
        x_new, h, main = _norm_mm(xt, g_norm1[l], sc1, sh1, w_main, l, seq=S,
                                  delta=delta, gate=gate)
        xt = xt if x_new is None else x_new
        main = main.reshape(Bn, S, N_MAIN)
        f32grp = _mm(h, w_f32, l, jnp.float32, N_F32).reshape(Bn, S, N_F32)

        b_pad = jnp.zeros((1, LANE), jnp.float32).at[0, COL_FD:COL_FD + N_HEADS].set(b_f[l])
        fcol = _forget_cumsum(f32grp, b_pad)
        frow = jnp.transpose(fcol[:, :, COL_FD:COL_FD + N_HEADS], (0, 2, 1))

        mask = _select(main, f32grp, k_top, tq)
        lam_init = 0.8 - 0.6 * math.exp(-0.3 * l)
        y_a = _attention(main, "dsa", tq, mask=mask, tab=rel_bias)
        y_b = _attention(main, "diff", tq, tab=rel_bias, lam=lam[l], g_subln=g_subln[l],
                         lam_init=lam_init)
        y_c = _short_conv(f32grp, conv_w[l])
        y_d = _attention(main, "fox", tq, fcol=fcol, frow=frow)
        ys = [y.reshape(T, MIX_W) for y in (y_a, y_b, y_c, y_d)]

        merged = _mix(h, ys, w_gate_b, b_gate.reshape(L, 1, N_BRANCH * D), w_branch_b, l)
        xt, h2 = _mm_norm(merged, w_o_b, l, xt, gt1, g_norm2[l], sc2, sh2, seq=S)
        delta = _ffn(h2, w_ff1_b, w_ff2_b, l)
        gate = gt2
    _, out = _norm(xt, g_final, seq=S, delta=delta, gate=gate, out_dtype=jnp.float32,
                   keep_x=False)
    return out.reshape(Bn, S, D)
```

```python
import functools
import math

import numpy as np
import jax
import jax.numpy as jnp
from jax import lax
from jax.experimental import pallas as pl
from jax.experimental.pallas import tpu as pltpu

HEAD_DIM = 128
N_HEADS = 4
DIFF_DIM = HEAD_DIM // 2
CONV_CH = 512
CONV_K = 3
IDX_HEADS = 4
IDX_DIM = 64
TOPK_MAX = 256
MIX_W = 512
N_BRANCH = 4
N_BUCKETS = 32
MAX_DIST = 128
EPS = 1e-6
NEG = -1e30
IDX_SCALE = (IDX_DIM ** -0.5) * (IDX_HEADS ** -0.5)
INT_MIN = -(2 ** 31)
LOG2E = 1.0 / math.log(2.0)

LANE = 128
VMEM_LIMIT = 56 * 1024 * 1024

BLK_QA, BLK_KA, BLK_VA = 0, 4, 8
BLK_QB, BLK_KB, BLK_VB = 12, 16, 20
BLK_QD, BLK_KD, BLK_VD = 24, 28, 32
BLK_IQ = 36
BLK_IK = 38
N_MAIN = 40 * LANE
N_F32 = 3 * CONV_CH + LANE
BLK_SMALL = 3 * CONV_CH // LANE
COL_IW, COL_FD = 0, 4


def _cparams(sem):
    return pltpu.CompilerParams(dimension_semantics=sem, vmem_limit_bytes=VMEM_LIMIT)


def _tile(n, pref):
    t = min(n, pref)
    while n % t:
        t -= 1
    return t


def _mod_kernel(c_ref, w_ref, b_ref, o_ref):
    c = c_ref[...]
    ca = (c * jax.nn.sigmoid(c)).astype(jnp.bfloat16)
    w = w_ref[0].astype(jnp.bfloat16)
    o_ref[0] = jnp.dot(ca, w, preferred_element_type=jnp.float32) + b_ref[0]


def _modulation(c, w_ada, b_ada):
    L, D, N = w_ada.shape
    Bn = c.shape[0]
    tn = _tile(N, 1024)
    return pl.pallas_call(
        _mod_kernel,
        grid=(L, N // tn),
        in_specs=[
            pl.BlockSpec((Bn, D), lambda l, j: (0, 0)),
            pl.BlockSpec((1, D, tn), lambda l, j: (l, 0, j)),
            pl.BlockSpec((1, 1, tn), lambda l, j: (l, 0, j)),
        ],
        out_specs=pl.BlockSpec((1, Bn, tn), lambda l, j: (l, 0, j)),
        out_shape=jax.ShapeDtypeStruct((L, Bn, N), jnp.float32),
        name="ada_mod",
        compiler_params=_cparams(("parallel", "parallel")),
    )(c, w_ada, b_ada.reshape(L, 1, N))


def _norm_kernel(*refs, has_delta, modulate, keep_x):
    it = iter(refs)
    x_ref = next(it)
    d_ref = next(it) if has_delta else None
    gate_ref = next(it) if has_delta else None
    g_ref = next(it)
    sc_ref = next(it) if modulate else None
    sh_ref = next(it) if modulate else None
    xo_ref = next(it) if keep_x else None
    h_ref = next(it)

    x = x_ref[...]
    if has_delta:
        x = x + gate_ref[0] * d_ref[...]
    if keep_x:
        xo_ref[...] = x
    y = x * lax.rsqrt(jnp.mean(x * x, axis=-1, keepdims=True) + EPS) * g_ref[...]
    if modulate:
        y = y * (1.0 + sc_ref[0]) + sh_ref[0]
    h_ref[...] = y.astype(h_ref.dtype)


def _norm(x, g, *, seq, delta=None, gate=None, sc=None, sh=None, out_dtype=jnp.bfloat16,
          keep_x=True):
    T, D = x.shape
    tm = _tile(seq, 512)
    per_seq = seq // tm
    has_delta = delta is not None
    keep_x = keep_x and has_delta
    modulate = sc is not None
    row = pl.BlockSpec((tm, D), lambda i: (i, 0))
    per_b = pl.BlockSpec((1, 1, D), lambda i: (i // per_seq, 0, 0))
    args, specs = [x], [row]
    if has_delta:
        args += [delta, gate]
        specs += [row, per_b]
    args.append(g.reshape(1, D))
    specs.append(pl.BlockSpec((1, D), lambda i: (0, 0)))
    if modulate:
        args += [sc, sh]
        specs += [per_b, per_b]
    out_shape, out_specs = [], []
    if keep_x:
        out_shape.append(jax.ShapeDtypeStruct((T, D), jnp.float32))
        out_specs.append(row)
    out_shape.append(jax.ShapeDtypeStruct((T, D), out_dtype))
    out_specs.append(row)
    res = pl.pallas_call(
        functools.partial(_norm_kernel, has_delta=has_delta, modulate=modulate, keep_x=keep_x),
        grid=(T // tm,),
        in_specs=specs,
        out_specs=out_specs,
        out_shape=out_shape,
        name="norm",
        compiler_params=_cparams(("parallel",)),
    )(*args)
    if keep_x:
        return res[0], res[1]
    return None, res[0]


def _rms_mod(x, g, sc, sh):
    y = x * lax.rsqrt(jnp.mean(x * x, axis=-1, keepdims=True) + EPS) * g
    return y * (1.0 + sc) + sh


def _mm_norm_kernel(a_ref, w_ref, x_ref, gate_ref, g_ref, sc_ref, sh_ref, xo_ref, h_ref):
    acc = jnp.dot(a_ref[...], w_ref[...], preferred_element_type=jnp.float32)
    x = x_ref[...] + gate_ref[0] * acc
    xo_ref[...] = x
    h_ref[...] = _rms_mod(x, g_ref[...], sc_ref[0], sh_ref[0]).astype(h_ref.dtype)


def _mm_norm(a, w, l, x, gate, g, sc, sh, *, seq):
    T, D = x.shape
    K = a.shape[1]
    tm = _tile(seq, 512)
    per_seq = seq // tm
    row = pl.BlockSpec((tm, D), lambda i: (i, 0))
    per_b = pl.BlockSpec((1, 1, D), lambda i: (i // per_seq, 0, 0))
    return pl.pallas_call(
        _mm_norm_kernel,
        grid=(T // tm,),
        in_specs=[pl.BlockSpec((tm, K), lambda i: (i, 0)),
                  pl.BlockSpec((None, K, D), lambda i: (l, 0, 0)),
                  row, per_b, pl.BlockSpec((1, D), lambda i: (0, 0)), per_b, per_b],
        out_specs=[row, row],
        out_shape=[jax.ShapeDtypeStruct((T, D), jnp.float32),
                   jax.ShapeDtypeStruct((T, D), jnp.bfloat16)],
        name="outproj_norm",
        compiler_params=_cparams(("parallel",)),
    )(a, w, x, gate, g.reshape(1, D), sc, sh)


def _mm_kernel(a_ref, w_ref, o_ref):
    o_ref[...] = jnp.dot(a_ref[...], w_ref[...],
                         preferred_element_type=jnp.float32).astype(o_ref.dtype)


def _mm(a, w, l, out_dtype, tn_pref):
    M, K = a.shape
    N = w.shape[2]
    tm = _tile(M, 1024)
    tn = _tile(N, tn_pref)
    return pl.pallas_call(
        _mm_kernel,
        grid=(M // tm, N // tn),
        in_specs=[pl.BlockSpec((tm, K), lambda i, j: (i, 0)),
                  pl.BlockSpec((None, K, tn), lambda i, j: (l, 0, j))],
        out_specs=pl.BlockSpec((tm, tn), lambda i, j: (i, j)),
        out_shape=jax.ShapeDtypeStruct((M, N), out_dtype),
        name="matmul",
        compiler_params=_cparams(("parallel", "parallel")),
    )(a, w)


def _cumsum_kernel(s_ref, b_ref, o_ref):
    S = s_ref.shape[1]
    x = s_ref[0] + b_ref[...]
    ls = jnp.minimum(x, 0.0) - jnp.log(1.0 + jnp.exp(-jnp.abs(x)))
    r = lax.broadcasted_iota(jnp.int32, (LANE, LANE), 0)
    c = lax.broadcasted_iota(jnp.int32, (LANE, LANE), 1)
    tri = jnp.where(r >= c, 1.0, 0.0).astype(jnp.bfloat16)
    hi = ls.astype(jnp.bfloat16)
    r1 = ls - hi.astype(jnp.float32)
    mid = r1.astype(jnp.bfloat16)
    lo = (r1 - mid.astype(jnp.float32)).astype(jnp.bfloat16)
    carry = jnp.zeros((1, LANE), jnp.float32)
    for j in range(S // LANE):
        rows = slice(j * LANE, (j + 1) * LANE)
        blk = (jnp.dot(tri, hi[rows], preferred_element_type=jnp.float32)
               + jnp.dot(tri, mid[rows], preferred_element_type=jnp.float32)
               + jnp.dot(tri, lo[rows], preferred_element_type=jnp.float32)) + carry
        o_ref[0, j * LANE:(j + 1) * LANE, :] = blk
        carry = blk[LANE - 1:LANE, :]


def _forget_cumsum(small, b_pad):
    Bn, S, _ = small.shape
    return pl.pallas_call(
        _cumsum_kernel,
        grid=(Bn,),
        in_specs=[pl.BlockSpec((1, S, LANE), lambda b: (b, 0, BLK_SMALL)),
                  pl.BlockSpec((1, LANE), lambda b: (0, 0))],
        out_specs=pl.BlockSpec((1, S, LANE), lambda b: (b, 0, 0)),
        out_shape=jax.ShapeDtypeStruct((Bn, S, LANE), jnp.float32),
        name="forget_cumsum",
        compiler_params=_cparams(("parallel",)),
    )(small, b_pad)


def _select_block(iq_ref, ik_ref, sm_ref, mask_ref, key_ref, *, k_top, blk):
    tq = iq_ref.shape[1]
    S = ik_ref.shape[1]
    W = (blk + 1) * tq
    iq = iq_ref[0]
    ik = ik_ref[0, :W, :IDX_DIM]
    iw = sm_ref[0][:, COL_IW:COL_IW + IDX_HEADS] * IDX_SCALE
    score = jnp.zeros((tq, W), jnp.float32)
    for h in range(IDX_HEADS):
        d = lax.dot_general(iq[:, h * IDX_DIM:(h + 1) * IDX_DIM], ik,
                            (((1,), (1,)), ((), ())), preferred_element_type=jnp.float32)
        score = score + jnp.where(d > 0.0, d * iw[:, h:h + 1], 0.0)
    causal = (lax.broadcasted_iota(jnp.int32, (tq, W), 1)
              <= blk * tq + lax.broadcasted_iota(jnp.int32, (tq, W), 0))
    score = jnp.where(causal, score, NEG)
    bits = lax.bitcast_convert_type(score, jnp.int32)
    key_ref[:, :W] = bits ^ ((bits >> 31) & 0x7FFFFFFF)
    kf = jnp.float32(k_top)

    def count_ge(cand):
        return jnp.sum(jnp.where(key_ref[:, :W] >= cand, 1.0, 0.0), axis=1, keepdims=True)

    thr = jnp.where(count_ge(jnp.zeros((tq, 1), jnp.int32)) >= kf, 0, INT_MIN).astype(jnp.int32)

    def value_bit(it, thr):
        cand = thr | jnp.left_shift(jnp.int32(1), 30 - it)
        return jnp.where(count_ge(cand) >= kf, cand, thr)

    thr = lax.fori_loop(0, 31, value_bit, thr)

    key = key_ref[:, :W]
    above = key > thr
    need = kf - jnp.sum(jnp.where(above, 1.0, 0.0), axis=1, keepdims=True)
    upper = jnp.where(lax.broadcasted_iota(jnp.int32, (tq, tq), 0)
                      <= lax.broadcasted_iota(jnp.int32, (tq, tq), 1), 1.0, 0.0).astype(jnp.bfloat16)
    seen = jnp.zeros((tq, 1), jnp.float32)
    for c in range(blk + 1):
        cs = slice(c * tq, (c + 1) * tq)
        tie = key[:, cs] == thr
        incl = seen + jnp.dot(jnp.where(tie, 1.0, 0.0).astype(jnp.bfloat16), upper,
                              preferred_element_type=jnp.float32)
        sel = above[:, cs] | (tie & (incl <= need))
        if c == blk:
            sel = sel & causal[:, cs]
        mask_ref[0, :, cs] = jnp.where(sel, 1.0, 0.0).astype(mask_ref.dtype)
        seen = incl[:, tq - 1:tq]
    if W < S:
        mask_ref[0, :, W:] = jnp.zeros((tq, S - W), mask_ref.dtype)


def _select_kernel(iq_ref, ik_ref, sm_ref, mask_ref, key_ref, *, k_top):
    i = pl.program_id(1)
    for blk in range(ik_ref.shape[1] // iq_ref.shape[1]):
        @pl.when(i == blk)
        def _(blk=blk):
            _select_block(iq_ref, ik_ref, sm_ref, mask_ref, key_ref, k_top=k_top, blk=blk)


def _select(main, f32grp, k_top, tq):
    Bn, S, _ = main.shape
    return pl.pallas_call(
        functools.partial(_select_kernel, k_top=k_top),
        grid=(Bn, S // tq),
        in_specs=[pl.BlockSpec((1, tq, 2 * LANE), lambda b, i: (b, i, BLK_IQ // 2)),
                  pl.BlockSpec((1, S, LANE), lambda b, i: (b, 0, BLK_IK)),
                  pl.BlockSpec((1, tq, LANE), lambda b, i: (b, i, BLK_SMALL))],
        out_specs=pl.BlockSpec((1, tq, S), lambda b, i: (b, i, 0)),
        out_shape=jax.ShapeDtypeStruct((Bn, S, S), jnp.bfloat16),
        scratch_shapes=[pltpu.VMEM((tq, S), jnp.int32)],
        name="dsa_select",
        compiler_params=_cparams(("parallel", "parallel")),
    )(main, main, f32grp)


def _t5_thresholds():
    max_exact = N_BUCKETS // 2
    rel = np.arange(MAX_DIST)
    nf = np.maximum(rel, 1).astype(np.float64)
    large = max_exact + (np.log(nf / max_exact) / math.log(MAX_DIST / max_exact)
                         * (N_BUCKETS - max_exact)).astype(np.int64)
    bucket = np.where(rel < max_exact, rel, np.minimum(large, N_BUCKETS - 1))
    out = []
    for b in range(1, N_BUCKETS):
        hit = np.nonzero(bucket >= b)[0]
        out.append(int(hit[0]) if hit.size else MAX_DIST)
    return out


def _attn_kernel(*refs, mode, head_off, lam_init):
    it = iter(refs)
    q_ref, k_ref, v_ref = next(it), next(it), next(it)
    if mode == "dsa":
        mask_ref = next(it)
    if mode in ("dsa", "diff"):
        tab_ref = next(it)
    if mode == "diff":
        lam_ref, gs_ref = next(it), next(it)
    if mode == "fox":
        fc_ref, fr_ref = next(it), next(it)
    o_ref = next(it)
    m_ref, acc_ref = next(it), next(it)
    if mode in ("dsa", "diff"):
        band_ref = next(it)

    i = pl.program_id(1)
    tq = q_ref.shape[1]
    nmap = 2 if mode == "diff" else 1
    dq = HEAD_DIM // nmap
    scale = dq ** -0.5
    biased = mode in ("dsa", "diff")

    c1 = scale * LOG2E
    if biased:
        @pl.when(i == 0)
        def _():
            r = lax.broadcasted_iota(jnp.int32, (tq, 2 * tq), 0)
            c = lax.broadcasted_iota(jnp.int32, (tq, 2 * tq), 1)
            rel = r - c + tq
            vals = [jnp.full((tq, 2 * tq), tab_ref[0, head_off + h] * LOG2E, jnp.float32)
                    for h in range(N_HEADS)]
            for b, first in enumerate(_t5_thresholds(), start=1):
                hit = rel >= first
                vals = [jnp.where(hit, tab_ref[b, head_off + h] * LOG2E, vals[h])
                        for h in range(N_HEADS)]
            for h in range(N_HEADS):
                band_ref[h] = vals[h]

    m_ref[...] = jnp.full(m_ref.shape, NEG, jnp.float32)
    acc_ref[...] = jnp.zeros(acc_ref.shape, jnp.float32)

    def chunk(off, kind, tk):
        ones = jnp.ones((tk, LANE), jnp.bfloat16)
        if mode == "dsa":
            keep = mask_ref[0, :, pl.ds(off, tk)].astype(jnp.float32) > 0.5
        elif kind == "far":
            keep = None
        else:
            keep = (lax.broadcasted_iota(jnp.int32, (tq, tk), 1) - (tk - tq)
                    <= lax.broadcasted_iota(jnp.int32, (tq, tk), 0))
        units = [(h, mp) for h in range(N_HEADS) for mp in range(nmap)]
        logits = []
        for h, mp in units:
            cols = slice(h * HEAD_DIM + mp * dq, h * HEAD_DIM + (mp + 1) * dq)
            logits.append(lax.dot_general(q_ref[0, :, cols], k_ref[0, pl.ds(off, tk), cols],
                                          (((1,), (1,)), ((), ())),
                                          preferred_element_type=jnp.float32))
        m_olds, m_news = [], []
        for u, (h, mp) in enumerate(units):
            t = logits[u] * c1
            if biased:
                if kind == "far":
                    t = t + tab_ref[N_BUCKETS - 1, head_off + h] * LOG2E
                elif kind == "near":
                    t = t + band_ref[h]
                else:
                    t = t + band_ref[h, :, tq:]
            else:
                t = (t + fc_ref[0, :, COL_FD + h:COL_FD + h + 1] * LOG2E
                     - fr_ref[0, h:h + 1, pl.ds(off, tk)] * LOG2E)
            if keep is not None:
                t = jnp.where(keep, t, NEG)
            logits[u] = t
            m_old = m_ref[u]
            m_olds.append(m_old)
            m_news.append(jnp.maximum(m_old, jnp.max(t, axis=1, keepdims=True)))
        probs = []
        for u in range(len(units)):
            probs.append(jnp.exp2(logits[u] - jnp.tile(m_news[u], (1, tk // LANE))).astype(v_ref.dtype))
            m_ref[u] = m_news[u]
        for u, (h, mp) in enumerate(units):
            vc = jnp.concatenate(
                [v_ref[0, pl.ds(off, tk), h * HEAD_DIM:(h + 1) * HEAD_DIM], ones], axis=1)
            alpha = jnp.exp2(m_olds[u] - m_news[u])
            acc_ref[u] = (jnp.tile(alpha, (1, 2)) * acc_ref[u]
                          + jnp.dot(probs[u], vc, preferred_element_type=jnp.float32))

    n_far = jnp.maximum(i - 1, 0)

    def far_body(c, carry):
        chunk(pl.multiple_of(c * 2 * tq, 2 * tq), "far", 2 * tq)
        return carry

    lax.fori_loop(0, n_far // 2, far_body, 0)

    @pl.when(n_far % 2 == 1)
    def _():
        chunk(pl.multiple_of((n_far - 1) * tq, tq), "far", tq)

    @pl.when(i >= 1)
    def _():
        chunk(pl.multiple_of((i - 1) * tq, tq), "near", 2 * tq)

    @pl.when(i == 0)
    def _():
        chunk(0, "diag", tq)

    if mode == "diff":
        lq = lam_ref[...]
        lam_val = (jnp.exp(jnp.sum(lq[0:1] * lq[1:2], axis=1, keepdims=True))
                   - jnp.exp(jnp.sum(lq[2:3] * lq[3:4], axis=1, keepdims=True)) + lam_init)
    for h in range(N_HEADS):
        a = acc_ref[h * nmap]
        o = a[:, :HEAD_DIM] / a[:, HEAD_DIM:]
        if mode == "diff":
            a1 = acc_ref[h * nmap + 1]
            o = o - lam_val * (a1[:, :HEAD_DIM] / a1[:, HEAD_DIM:])
            o = o * lax.rsqrt(jnp.mean(o * o, axis=-1, keepdims=True) + EPS) * gs_ref[...]
            o = o * (1.0 - lam_init)
        o_ref[0, :, h * HEAD_DIM:(h + 1) * HEAD_DIM] = o.astype(o_ref.dtype)


def _attention(main, mode, tq, *, mask=None, tab=None, lam=None, g_subln=None,
               fcol=None, frow=None, lam_init=0.0):
    Bn, S, _ = main.shape
    blk_q, blk_k, blk_v, head_off = {
        "dsa": (BLK_QA, BLK_KA, BLK_VA, 0),
        "diff": (BLK_QB, BLK_KB, BLK_VB, N_HEADS),
        "fox": (BLK_QD, BLK_KD, BLK_VD, 0),
    }[mode]
    nmap = 2 if mode == "diff" else 1
    W = N_HEADS * HEAD_DIM
    args = [main, main, main]
    specs = [pl.BlockSpec((1, tq, W), lambda b, i: (b, i, blk_q // N_HEADS)),
             pl.BlockSpec((1, S, W), lambda b, i: (b, 0, blk_k // N_HEADS)),
             pl.BlockSpec((1, S, W), lambda b, i: (b, 0, blk_v // N_HEADS))]
    scratch = [pltpu.VMEM((N_HEADS * nmap, tq, LANE), jnp.float32),
               pltpu.VMEM((N_HEADS * nmap, tq, 2 * HEAD_DIM), jnp.float32)]
    if mode == "dsa":
        args.append(mask)
        specs.append(pl.BlockSpec((1, tq, S), lambda b, i: (b, i, 0)))
    if mode in ("dsa", "diff"):
        args.append(tab)
        specs.append(pl.BlockSpec(memory_space=pltpu.SMEM))
        scratch.append(pltpu.VMEM((N_HEADS, tq, 2 * tq), jnp.float32))
    if mode == "diff":
        args += [lam, g_subln.reshape(1, HEAD_DIM)]
        specs += [pl.BlockSpec((4, DIFF_DIM), lambda b, i: (0, 0)),
                  pl.BlockSpec((1, HEAD_DIM), lambda b, i: (0, 0))]
    if mode == "fox":
        args += [fcol, frow]
        specs += [pl.BlockSpec((1, tq, LANE), lambda b, i: (b, i, 0)),
                  pl.BlockSpec((1, N_HEADS, S), lambda b, i: (b, 0, 0))]
    return pl.pallas_call(
        functools.partial(_attn_kernel, mode=mode, head_off=head_off, lam_init=lam_init),
        grid=(Bn, S // tq),
        in_specs=specs,
        out_specs=pl.BlockSpec((1, tq, W), lambda b, i: (b, i, 0)),
        out_shape=jax.ShapeDtypeStruct((Bn, S, MIX_W), jnp.bfloat16),
        scratch_shapes=scratch,
        name="attn_" + mode,
        compiler_params=_cparams(("parallel", "arbitrary")),
    )(*args)


def _conv_kernel(bg_ref, cg_ref, xin_ref, w_ref, o_ref):
    u = cg_ref[0] * xin_ref[0]
    row = lax.broadcasted_iota(jnp.int32, u.shape, 0)
    w = w_ref[...]
    y = u * w[CONV_K - 1:CONV_K, :]
    for d in range(1, CONV_K):
        shifted = jnp.where(row >= d, pltpu.roll(u, d, 0), 0.0)
        y = y + shifted * w[CONV_K - 1 - d:CONV_K - d, :]
    o_ref[0] = (bg_ref[0] * y).astype(o_ref.dtype)


def _short_conv(f32grp, conv_w):
    Bn, S, _ = f32grp.shape
    tc = LANE
    per = CONV_CH // tc
    return pl.pallas_call(
        _conv_kernel,
        grid=(Bn, per),
        in_specs=[pl.BlockSpec((1, S, tc), lambda b, j: (b, 0, j)),
                  pl.BlockSpec((1, S, tc), lambda b, j: (b, 0, per + j)),
                  pl.BlockSpec((1, S, tc), lambda b, j: (b, 0, 2 * per + j)),
                  pl.BlockSpec((CONV_K, tc), lambda b, j: (0, j))],
        out_specs=pl.BlockSpec((1, S, tc), lambda b, j: (b, 0, j)),
        out_shape=jax.ShapeDtypeStruct((Bn, S, CONV_CH), jnp.bfloat16),
        name="short_conv",
        compiler_params=_cparams(("parallel", "parallel")),
    )(f32grp, f32grp, f32grp, conv_w)


def _mix_kernel(*refs):
    h_ref = refs[0]
    y_refs = refs[1:1 + N_BRANCH]
    wg_refs = refs[1 + N_BRANCH:1 + 2 * N_BRANCH]
    bg_refs = refs[1 + 2 * N_BRANCH:1 + 3 * N_BRANCH]
    wb_refs = refs[1 + 3 * N_BRANCH:1 + 4 * N_BRANCH]
    o_ref = refs[1 + 4 * N_BRANCH]
    h = h_ref[...]
    merged = None
    for g in range(N_BRANCH):
        gate = jax.nn.sigmoid(jnp.dot(h, wg_refs[g][...], preferred_element_type=jnp.float32)
                              + bg_refs[g][...])
        proj = jnp.dot(y_refs[g][...], wb_refs[g][...], preferred_element_type=jnp.float32)
        merged = gate * proj if merged is None else merged + gate * proj
    o_ref[...] = merged.astype(o_ref.dtype)


def _mix(h, ys, w_gate, b_gate, w_branch, l):
    T, D = h.shape
    tm = _tile(T, 1024)
    tn = _tile(D, 512)
    per = D // tn
    specs = [pl.BlockSpec((tm, D), lambda i, n: (i, 0))]
    specs += [pl.BlockSpec((tm, MIX_W), lambda i, n: (i, 0)) for _ in range(N_BRANCH)]
    specs += [pl.BlockSpec((None, D, tn), lambda i, n, g=g: (l, 0, g * per + n))
              for g in range(N_BRANCH)]
    specs += [pl.BlockSpec((None, 1, tn), lambda i, n, g=g: (l, 0, g * per + n))
              for g in range(N_BRANCH)]
    specs += [pl.BlockSpec((None, None, MIX_W, tn), lambda i, n, g=g: (l, g, 0, n))
              for g in range(N_BRANCH)]
    return pl.pallas_call(
        _mix_kernel,
        grid=(T // tm, per),
        in_specs=specs,
        out_specs=pl.BlockSpec((tm, tn), lambda i, n: (i, n)),
        out_shape=jax.ShapeDtypeStruct((T, D), jnp.bfloat16),
        name="gated_mix",
        compiler_params=_cparams(("parallel", "parallel")),
    )(h, *ys, *([w_gate] * N_BRANCH), *([b_gate] * N_BRANCH), *([w_branch] * N_BRANCH))


def _ffn_kernel(h_ref, w1_ref, w2_ref, o_ref, u_ref, *, nf):
    f = pl.program_id(1)

    def first(slot):
        u = jnp.dot(h_ref[...], w1_ref[...], preferred_element_type=jnp.float32)
        u_ref[slot] = jnp.square(jnp.maximum(u, 0.0)).astype(u_ref.dtype)

    def second(slot):
        return jnp.dot(u_ref[slot], w2_ref[...], preferred_element_type=jnp.float32)

    @pl.when(f == 0)
    def _():
        o_ref[...] = jnp.zeros(o_ref.shape, o_ref.dtype)
        first(0)

    for parity in range(2):
        @pl.when((f > 0) & (f < nf) & (f % 2 == parity))
        def _(parity=parity):
            part = second(1 - parity)
            first(parity)
            o_ref[...] += part

    @pl.when(f == nf)
    def _():
        o_ref[...] += second((nf - 1) % 2)


def _ffn(h, w1, w2, l):
    T, D = h.shape
    F = w1.shape[2]
    tm = _tile(T, 1024)
    tf = _tile(F, 512)
    nf = F // tf
    return pl.pallas_call(
        functools.partial(_ffn_kernel, nf=nf),
        grid=(T // tm, nf + 1),
        in_specs=[pl.BlockSpec((tm, D), lambda i, f: (i, 0)),
                  pl.BlockSpec((None, D, tf), lambda i, f: (l, 0, jnp.minimum(f, nf - 1))),
                  pl.BlockSpec((None, tf, D), lambda i, f: (l, jnp.maximum(f - 1, 0), 0))],
        out_specs=pl.BlockSpec((tm, D), lambda i, f: (i, 0)),
        out_shape=jax.ShapeDtypeStruct((T, D), jnp.float32),
        scratch_shapes=[pltpu.VMEM((2, tm, tf), jnp.bfloat16)],
        name="ffn",
        compiler_params=_cparams(("parallel", "arbitrary")),
    )(h, w1, w2)


def _split_in_proj(w_in):
    W = N_HEADS * HEAD_DIM
    sizes = (3 * W,
             IDX_HEADS * IDX_DIM + IDX_DIM,
             IDX_HEADS,
             3 * W,
             3 * CONV_CH,
             3 * W,
             N_HEADS)
    offs = np.concatenate([[0], np.cumsum(sizes)])
    a, idx, iw, b, cv, d, fd = [w_in[:, :, offs[j]:offs[j + 1]] for j in range(7)]
    L, D, _ = w_in.shape
    zeros = lambda n: jnp.zeros((L, D, n), w_in.dtype)
    main = jnp.concatenate([a, b, d, idx, zeros(N_MAIN - 9 * W - sizes[1])], axis=-1)
    f32grp = jnp.concatenate([cv, iw, fd, zeros(LANE - IDX_HEADS - N_HEADS)], axis=-1)
    return main.astype(jnp.bfloat16), f32grp.astype(jnp.bfloat16)


def kernel(x, c, w_ada, b_ada, g_norm1, w_in, b_f, conv_w, lam, g_subln, rel_bias,
           w_gate, b_gate, w_branch, w_o, g_norm2, w_ff1, w_ff2, g_final):
    Bn, S, D = x.shape
    L = w_ada.shape[0]
    T = Bn * S
    k_top = min(TOPK_MAX, S // 4)
    tq = _tile(S, 256)
    bf = jnp.bfloat16

    w_main, w_f32 = _split_in_proj(w_in)
    w_gate_b, w_branch_b, w_o_b = w_gate.astype(bf), w_branch.astype(bf), w_o.astype(bf)
    w_ff1_b, w_ff2_b = w_ff1.astype(bf), w_ff2.astype(bf)

    mod = _modulation(c, w_ada, b_ada).reshape(L, Bn, 6, 1, D)
    xt = x.reshape(T, D)
    delta, gate = None, None
    for l in range(L):
        sh1, sc1, gt1, sh2, sc2, gt2 = [mod[l, :, j] for j in range(6)]
        x_new, h = _norm(xt, g_norm1[l], seq=S, delta=delta, gate=gate, sc=sc1, sh=sh1)
        xt = xt if x_new is None else x_new
        main = _mm(h, w_main, l, bf, 1024).reshape(Bn, S, N_MAIN)
        f32grp = _mm(h, w_f32, l, jnp.float32, N_F32).reshape(Bn, S, N_F32)

        b_pad = jnp.zeros((1, LANE), jnp.float32).at[0, COL_FD:COL_FD + N_HEADS].set(b_f[l])
        fcol = _forget_cumsum(f32grp, b_pad)
        frow = jnp.transpose(fcol[:, :, COL_FD:COL_FD + N_HEADS], (0, 2, 1))

        mask = _select(main, f32grp, k_top, _tile(S, 512))
        lam_init = 0.8 - 0.6 * math.exp(-0.3 * l)
        y_a = _attention(main, "dsa", tq, mask=mask, tab=rel_bias)
        y_b = _attention(main, "diff", tq, tab=rel_bias, lam=lam[l], g_subln=g_subln[l],
                         lam_init=lam_init)
        y_c = _short_conv(f32grp, conv_w[l])
        y_d = _attention(main, "fox", tq, fcol=fcol, frow=frow)
        ys = [y.reshape(T, MIX_W) for y in (y_a, y_b, y_c, y_d)]

        merged = _mix(h, ys, w_gate_b, b_gate.reshape(L, 1, N_BRANCH * D), w_branch_b, l)
        xt, h2 = _mm_norm(merged, w_o_b, l, xt, gt1, g_norm2[l], sc2, sh2, seq=S)
        delta = _ffn(h2, w_ff1_b, w_ff2_b, l)
        gate = gt2
    _, out = _norm(xt, g_final, seq=S, delta=delta, gate=gate, out_dtype=jnp.float32,
                   keep_x=False)
    return out.reshape(Bn, S, D)
```

```python
import functools
import math

import numpy as np
import jax
import jax.numpy as jnp
from jax import lax
from jax.experimental import pallas as pl
from jax.experimental.pallas import tpu as pltpu

HEAD_DIM = 128
N_HEADS = 4
DIFF_DIM = HEAD_DIM // 2
CONV_CH = 512
CONV_K = 3
IDX_HEADS = 4
IDX_DIM = 64
TOPK_MAX = 256
MIX_W = 512
N_BRANCH = 4
N_BUCKETS = 32
MAX_DIST = 128
EPS = 1e-6
NEG = -1e30
IDX_SCALE = (IDX_DIM ** -0.5) * (IDX_HEADS ** -0.5)
INT_MIN = -(2 ** 31)
LOG2E = 1.0 / math.log(2.0)

LANE = 128
VMEM_LIMIT = 56 * 1024 * 1024

BLK_QA, BLK_KA, BLK_VA = 0, 4, 8
BLK_QB, BLK_KB, BLK_VB = 12, 16, 20
BLK_QD, BLK_KD, BLK_VD = 24, 28, 32
BLK_IQ = 36
BLK_IK = 38
N_MAIN = 40 * LANE
N_F32 = 3 * CONV_CH + LANE
BLK_SMALL = 3 * CONV_CH // LANE
COL_IW, COL_FD = 0, 4


def _cparams(sem):
    return pltpu.CompilerParams(dimension_semantics=sem, vmem_limit_bytes=VMEM_LIMIT)


def _tile(n, pref):
    t = min(n, pref)
    while n % t:
        t -= 1
    return t


def _mod_kernel(c_ref, w_ref, b_ref, o_ref):
    c = c_ref[...]
    ca = (c * jax.nn.sigmoid(c)).astype(jnp.bfloat16)
    w = w_ref[0].astype(jnp.bfloat16)
    o_ref[0] = jnp.dot(ca, w, preferred_element_type=jnp.float32) + b_ref[0]


def _modulation(c, w_ada, b_ada):
    L, D, N = w_ada.shape
    Bn = c.shape[0]
    tn = _tile(N, 1024)
    return pl.pallas_call(
        _mod_kernel,
        grid=(L, N // tn),
        in_specs=[
            pl.BlockSpec((Bn, D), lambda l, j: (0, 0)),
            pl.BlockSpec((1, D, tn), lambda l, j: (l, 0, j)),
            pl.BlockSpec((1, 1, tn), lambda l, j: (l, 0, j)),
        ],
        out_specs=pl.BlockSpec((1, Bn, tn), lambda l, j: (l, 0, j)),
        out_shape=jax.ShapeDtypeStruct((L, Bn, N), jnp.float32),
        name="ada_mod",
        compiler_params=_cparams(("parallel", "parallel")),
    )(c, w_ada, b_ada.reshape(L, 1, N))


def _norm_kernel(*refs, has_delta, modulate, keep_x):
    it = iter(refs)
    x_ref = next(it)
    d_ref = next(it) if has_delta else None
    gate_ref = next(it) if has_delta else None
    g_ref = next(it)
    sc_ref = next(it) if modulate else None
    sh_ref = next(it) if modulate else None
    xo_ref = next(it) if keep_x else None
    h_ref = next(it)

    x = x_ref[...]
    if has_delta:
        x = x + gate_ref[0] * d_ref[...]
    if keep_x:
        xo_ref[...] = x
    y = x * lax.rsqrt(jnp.mean(x * x, axis=-1, keepdims=True) + EPS) * g_ref[...]
    if modulate:
        y = y * (1.0 + sc_ref[0]) + sh_ref[0]
    h_ref[...] = y.astype(h_ref.dtype)


def _norm(x, g, *, seq, delta=None, gate=None, sc=None, sh=None, out_dtype=jnp.bfloat16,
          keep_x=True):
    T, D = x.shape
    tm = _tile(seq, 512)
    per_seq = seq // tm
    has_delta = delta is not None
    keep_x = keep_x and has_delta
    modulate = sc is not None
    row = pl.BlockSpec((tm, D), lambda i: (i, 0))
    per_b = pl.BlockSpec((1, 1, D), lambda i: (i // per_seq, 0, 0))
    args, specs = [x], [row]
    if has_delta:
        args += [delta, gate]
        specs += [row, per_b]
    args.append(g.reshape(1, D))
    specs.append(pl.BlockSpec((1, D), lambda i: (0, 0)))
    if modulate:
        args += [sc, sh]
        specs += [per_b, per_b]
    out_shape, out_specs = [], []
    if keep_x:
        out_shape.append(jax.ShapeDtypeStruct((T, D), jnp.float32))
        out_specs.append(row)
    out_shape.append(jax.ShapeDtypeStruct((T, D), out_dtype))
    out_specs.append(row)
    res = pl.pallas_call(
        functools.partial(_norm_kernel, has_delta=has_delta, modulate=modulate, keep_x=keep_x),
        grid=(T // tm,),
        in_specs=specs,
        out_specs=out_specs,
        out_shape=out_shape,
        name="norm",
        compiler_params=_cparams(("parallel",)),
    )(*args)
    if keep_x:
        return res[0], res[1]
    return None, res[0]


def _rms_mod(x, g, sc, sh):
    y = x * lax.rsqrt(jnp.mean(x * x, axis=-1, keepdims=True) + EPS) * g
    return y * (1.0 + sc) + sh


def _mm_norm_kernel(a_ref, w_ref, x_ref, gate_ref, g_ref, sc_ref, sh_ref, xo_ref, h_ref):
    acc = jnp.dot(a_ref[...], w_ref[...], preferred_element_type=jnp.float32)
    x = x_ref[...] + gate_ref[0] * acc
    xo_ref[...] = x
    h_ref[...] = _rms_mod(x, g_ref[...], sc_ref[0], sh_ref[0]).astype(h_ref.dtype)


def _mm_norm(a, w, l, x, gate, g, sc, sh, *, seq):
    T, D = x.shape
    K = a.shape[1]
    tm = _tile(seq, 512)
    per_seq = seq // tm
    row = pl.BlockSpec((tm, D), lambda i: (i, 0))
    per_b = pl.BlockSpec((1, 1, D), lambda i: (i // per_seq, 0, 0))
    return pl.pallas_call(
        _mm_norm_kernel,
        grid=(T // tm,),
        in_specs=[pl.BlockSpec((tm, K), lambda i: (i, 0)),
                  pl.BlockSpec((None, K, D), lambda i: (l, 0, 0)),
                  row, per_b, pl.BlockSpec((1, D), lambda i: (0, 0)), per_b, per_b],
        out_specs=[row, row],
        out_shape=[jax.ShapeDtypeStruct((T, D), jnp.float32),
                   jax.ShapeDtypeStruct((T, D), jnp.bfloat16)],
        name="outproj_norm",
        compiler_params=_cparams(("parallel",)),
    )(a, w, x, gate, g.reshape(1, D), sc, sh)


def _mm_kernel(a_ref, w_ref, o_ref):
    o_ref[...] = jnp.dot(a_ref[...], w_ref[...],
                         preferred_element_type=jnp.float32).astype(o_ref.dtype)


def _mm(a, w, l, out_dtype, tn_pref):
    M, K = a.shape
    N = w.shape[2]
    tm = _tile(M, 1024)
    tn = _tile(N, tn_pref)
    return pl.pallas_call(
        _mm_kernel,
        grid=(M // tm, N // tn),
        in_specs=[pl.BlockSpec((tm, K), lambda i, j: (i, 0)),
                  pl.BlockSpec((None, K, tn), lambda i, j: (l, 0, j))],
        out_specs=pl.BlockSpec((tm, tn), lambda i, j: (i, j)),
        out_shape=jax.ShapeDtypeStruct((M, N), out_dtype),
        name="matmul",
        compiler_params=_cparams(("parallel", "parallel")),
    )(a, w)


def _cumsum_kernel(s_ref, b_ref, o_ref):
    S = s_ref.shape[1]
    x = s_ref[0] + b_ref[...]
    ls = jnp.minimum(x, 0.0) - jnp.log(1.0 + jnp.exp(-jnp.abs(x)))
    r = lax.broadcasted_iota(jnp.int32, (LANE, LANE), 0)
    c = lax.broadcasted_iota(jnp.int32, (LANE, LANE), 1)
    tri = jnp.where(r >= c, 1.0, 0.0).astype(jnp.bfloat16)
    hi = ls.astype(jnp.bfloat16)
    r1 = ls - hi.astype(jnp.float32)
    mid = r1.astype(jnp.bfloat16)
    lo = (r1 - mid.astype(jnp.float32)).astype(jnp.bfloat16)
    carry = jnp.zeros((1, LANE), jnp.float32)
    for j in range(S // LANE):
        rows = slice(j * LANE, (j + 1) * LANE)
        blk = (jnp.dot(tri, hi[rows], preferred_element_type=jnp.float32)
               + jnp.dot(tri, mid[rows], preferred_element_type=jnp.float32)
               + jnp.dot(tri, lo[rows], preferred_element_type=jnp.float32)) + carry
        o_ref[0, j * LANE:(j + 1) * LANE, :] = blk
        carry = blk[LANE - 1:LANE, :]


def _forget_cumsum(small, b_pad):
    Bn, S, _ = small.shape
    return pl.pallas_call(
        _cumsum_kernel,
        grid=(Bn,),
        in_specs=[pl.BlockSpec((1, S, LANE), lambda b: (b, 0, BLK_SMALL)),
                  pl.BlockSpec((1, LANE), lambda b: (0, 0))],
        out_specs=pl.BlockSpec((1, S, LANE), lambda b: (b, 0, 0)),
        out_shape=jax.ShapeDtypeStruct((Bn, S, LANE), jnp.float32),
        name="forget_cumsum",
        compiler_params=_cparams(("parallel",)),
    )(small, b_pad)


N_KEY_BITS = 32


def _select_block(iq_ref, ik_ref, sm_ref, h_ref, wg_ref, bg_ref, mask_ref, z_ref, key_ref, thr_ref,
                  *, k_top, blk, n_chunks):
    c = pl.program_id(2)
    tq = iq_ref.shape[1]
    S = ik_ref.shape[1]
    W = (blk + 1) * tq
    kf = jnp.float32(k_top)
    causal = (lax.broadcasted_iota(jnp.int32, (tq, W), 1)
              <= blk * tq + lax.broadcasted_iota(jnp.int32, (tq, W), 0))

    @pl.when(c == 0)
    def _():
        iq = iq_ref[0]
        ik = ik_ref[0, :W, :IDX_DIM]
        iw = sm_ref[0][:, COL_IW:COL_IW + IDX_HEADS] * IDX_SCALE
        score = jnp.zeros((tq, W), jnp.float32)
        for h in range(IDX_HEADS):
            d = lax.dot_general(iq[:, h * IDX_DIM:(h + 1) * IDX_DIM], ik,
                                (((1,), (1,)), ((), ())), preferred_element_type=jnp.float32)
            score = score + jnp.where(d > 0.0, d * iw[:, h:h + 1], 0.0)
        score = jnp.where(causal, score, NEG)
        bits = lax.bitcast_convert_type(score, jnp.int32)
        key_ref[:, :W] = bits ^ ((bits >> 31) & 0x7FFFFFFF)
        thr_ref[...] = jnp.full(thr_ref.shape, INT_MIN, jnp.int32)

    per_step = N_KEY_BITS // n_chunks
    tz = z_ref.shape[1] // per_step
    thr = thr_ref[...]
    for j in range(per_step):
        zs = slice(j * tz, (j + 1) * tz)
        z = jnp.dot(h_ref[...], wg_ref[:, zs], preferred_element_type=jnp.float32) + bg_ref[:, zs]
        p = c * per_step + j
        cand = jnp.where(p == 0, 0, thr | jnp.left_shift(jnp.int32(1), N_KEY_BITS - 1 - p))
        cnt = jnp.sum(jnp.where(key_ref[:, :W] >= cand, 1.0, 0.0), axis=1, keepdims=True)
        thr = jnp.where(cnt >= kf, cand, thr)
        z_ref[:, zs] = z.astype(z_ref.dtype)
    thr_ref[...] = thr

    @pl.when(c == n_chunks - 1)
    def _():
        _select_write(mask_ref, key_ref, thr_ref[...], causal, kf, blk)


def _select_write(mask_ref, key_ref, thr, causal, kf, blk):
    tq, S = mask_ref.shape[1], mask_ref.shape[2]
    W = (blk + 1) * tq
    key = key_ref[:, :W]
    above = key > thr
    need = kf - jnp.sum(jnp.where(above, 1.0, 0.0), axis=1, keepdims=True)
    upper = jnp.where(lax.broadcasted_iota(jnp.int32, (tq, tq), 0)
                      <= lax.broadcasted_iota(jnp.int32, (tq, tq), 1), 1.0, 0.0).astype(jnp.bfloat16)
    seen = jnp.zeros((tq, 1), jnp.float32)
    for c in range(blk + 1):
        cs = slice(c * tq, (c + 1) * tq)
        tie = key[:, cs] == thr
        incl = seen + jnp.dot(jnp.where(tie, 1.0, 0.0).astype(jnp.bfloat16), upper,
                              preferred_element_type=jnp.float32)
        sel = above[:, cs] | (tie & (incl <= need))
        if c == blk:
            sel = sel & causal[:, cs]
        mask_ref[0, :, cs] = jnp.where(sel, 1.0, 0.0).astype(mask_ref.dtype)
        seen = incl[:, tq - 1:tq]
    if W < S:
        mask_ref[0, :, W:] = jnp.zeros((tq, S - W), mask_ref.dtype)


def _select_kernel(*refs, k_top, n_chunks):
    iq_ref, ik_ref = refs[0], refs[1]
    i = pl.program_id(1)
    for blk in range(ik_ref.shape[1] // iq_ref.shape[1]):
        @pl.when(i == blk)
        def _(blk=blk):
            _select_block(*refs, k_top=k_top, blk=blk, n_chunks=n_chunks)


def _select_and_gates(main, f32grp, h, w_gate, b_gate, l, k_top, tq):
    Bn, S, _ = main.shape
    T, D = h.shape
    N = w_gate.shape[2]
    nq = S // tq
    tc = _tile(N, 1024)
    n_chunks = N // tc
    assert N_KEY_BITS % n_chunks == 0
    return pl.pallas_call(
        functools.partial(_select_kernel, k_top=k_top, n_chunks=n_chunks),
        grid=(Bn, nq, n_chunks),
        in_specs=[pl.BlockSpec((1, tq, 2 * LANE), lambda b, i, c: (b, i, BLK_IQ // 2)),
                  pl.BlockSpec((1, S, LANE), lambda b, i, c: (b, 0, BLK_IK)),
                  pl.BlockSpec((1, tq, LANE), lambda b, i, c: (b, i, BLK_SMALL)),
                  pl.BlockSpec((tq, D), lambda b, i, c: (b * nq + i, 0)),
                  pl.BlockSpec((None, D, tc), lambda b, i, c: (l, 0, c)),
                  pl.BlockSpec((None, 1, tc), lambda b, i, c: (l, 0, c))],
        out_specs=[pl.BlockSpec((1, tq, S), lambda b, i, c: (b, i, 0)),
                   pl.BlockSpec((tq, tc), lambda b, i, c: (b * nq + i, c))],
        out_shape=[jax.ShapeDtypeStruct((Bn, S, S), jnp.bfloat16),
                   jax.ShapeDtypeStruct((T, N), jnp.bfloat16)],
        scratch_shapes=[pltpu.VMEM((tq, S), jnp.int32), pltpu.VMEM((tq, 1), jnp.int32)],
        name="dsa_select_gates",
        compiler_params=_cparams(("parallel", "parallel", "arbitrary")),
    )(main, main, f32grp, h, w_gate, b_gate)


def _t5_thresholds():
    max_exact = N_BUCKETS // 2
    rel = np.arange(MAX_DIST)
    nf = np.maximum(rel, 1).astype(np.float64)
    large = max_exact + (np.log(nf / max_exact) / math.log(MAX_DIST / max_exact)
                         * (N_BUCKETS - max_exact)).astype(np.int64)
    bucket = np.where(rel < max_exact, rel, np.minimum(large, N_BUCKETS - 1))
    out = []
    for b in range(1, N_BUCKETS):
        hit = np.nonzero(bucket >= b)[0]
        out.append(int(hit[0]) if hit.size else MAX_DIST)
    return out


def _attn_kernel(*refs, mode, head_off, lam_init):
    it = iter(refs)
    q_ref, k_ref, v_ref = next(it), next(it), next(it)
    if mode == "dsa":
        mask_ref = next(it)
    if mode in ("dsa", "diff"):
        tab_ref = next(it)
    if mode == "diff":
        lam_ref, gs_ref = next(it), next(it)
    if mode == "fox":
        fc_ref, fr_ref = next(it), next(it)
    o_ref = next(it)
    m_ref, acc_ref = next(it), next(it)
    if mode in ("dsa", "diff"):
        band_ref = next(it)

    i = pl.program_id(1)
    tq = q_ref.shape[1]
    nmap = 2 if mode == "diff" else 1
    dq = HEAD_DIM // nmap
    scale = dq ** -0.5
    biased = mode in ("dsa", "diff")

    c1 = scale * LOG2E
    if biased:
        @pl.when(i == 0)
        def _():
            r = lax.broadcasted_iota(jnp.int32, (tq, 2 * tq), 0)
            c = lax.broadcasted_iota(jnp.int32, (tq, 2 * tq), 1)
            rel = r - c + tq
            vals = [jnp.full((tq, 2 * tq), tab_ref[0, head_off + h] * LOG2E, jnp.float32)
                    for h in range(N_HEADS)]
            for b, first in enumerate(_t5_thresholds(), start=1):
                hit = rel >= first
                vals = [jnp.where(hit, tab_ref[b, head_off + h] * LOG2E, vals[h])
                        for h in range(N_HEADS)]
            for h in range(N_HEADS):
                band_ref[h] = vals[h]

    m_ref[...] = jnp.full(m_ref.shape, NEG, jnp.float32)
    acc_ref[...] = jnp.zeros(acc_ref.shape, jnp.float32)

    def chunk(off, kind, tk):
        ones = jnp.ones((tk, LANE), jnp.bfloat16)
        if mode == "dsa":
            keep = mask_ref[0, :, pl.ds(off, tk)].astype(jnp.float32) > 0.5
        elif kind == "far":
            keep = None
        else:
            keep = (lax.broadcasted_iota(jnp.int32, (tq, tk), 1) - (tk - tq)
                    <= lax.broadcasted_iota(jnp.int32, (tq, tk), 0))
        units = [(h, mp) for h in range(N_HEADS) for mp in range(nmap)]
        logits = []
        for h, mp in units:
            cols = slice(h * HEAD_DIM + mp * dq, h * HEAD_DIM + (mp + 1) * dq)
            logits.append(lax.dot_general(q_ref[0, :, cols], k_ref[0, pl.ds(off, tk), cols],
                                          (((1,), (1,)), ((), ())),
                                          preferred_element_type=jnp.float32))
        m_olds, m_news = [], []
        for u, (h, mp) in enumerate(units):
            t = logits[u] * c1
            if biased:
                if kind == "far":
                    t = t + tab_ref[N_BUCKETS - 1, head_off + h] * LOG2E
                elif kind == "near":
                    t = t + band_ref[h]
                else:
                    t = t + band_ref[h, :, tq:]
            else:
                t = (t + fc_ref[0, :, COL_FD + h:COL_FD + h + 1] * LOG2E
                     - fr_ref[0, h:h + 1, pl.ds(off, tk)] * LOG2E)
            if keep is not None:
                t = jnp.where(keep, t, NEG)
            logits[u] = t
            m_old = m_ref[u]
            m_olds.append(m_old)
            m_news.append(jnp.maximum(m_old, jnp.max(t, axis=1, keepdims=True)))
        probs = []
        for u in range(len(units)):
            probs.append(jnp.exp2(logits[u] - jnp.tile(m_news[u], (1, tk // LANE))).astype(v_ref.dtype))
            m_ref[u] = m_news[u]
        for u, (h, mp) in enumerate(units):
            vc = jnp.concatenate(
                [v_ref[0, pl.ds(off, tk), h * HEAD_DIM:(h + 1) * HEAD_DIM], ones], axis=1)
            alpha = jnp.exp2(m_olds[u] - m_news[u])
            acc_ref[u] = (jnp.tile(alpha, (1, 2)) * acc_ref[u]
                          + jnp.dot(probs[u], vc, preferred_element_type=jnp.float32))

    n_far = jnp.maximum(i - 1, 0)

    def far_body(c, carry):
        chunk(pl.multiple_of(c * 2 * tq, 2 * tq), "far", 2 * tq)
        return carry

    lax.fori_loop(0, n_far // 2, far_body, 0)

    @pl.when(n_far % 2 == 1)
    def _():
        chunk(pl.multiple_of((n_far - 1) * tq, tq), "far", tq)

    @pl.when(i >= 1)
    def _():
        chunk(pl.multiple_of((i - 1) * tq, tq), "near", 2 * tq)

    @pl.when(i == 0)
    def _():
        chunk(0, "diag", tq)

    if mode == "diff":
        lq = lam_ref[...]
        lam_val = (jnp.exp(jnp.sum(lq[0:1] * lq[1:2], axis=1, keepdims=True))
                   - jnp.exp(jnp.sum(lq[2:3] * lq[3:4], axis=1, keepdims=True)) + lam_init)
    for h in range(N_HEADS):
        a = acc_ref[h * nmap]
        o = a[:, :HEAD_DIM] / a[:, HEAD_DIM:]
        if mode == "diff":
            a1 = acc_ref[h * nmap + 1]
            o = o - lam_val * (a1[:, :HEAD_DIM] / a1[:, HEAD_DIM:])
            o = o * lax.rsqrt(jnp.mean(o * o, axis=-1, keepdims=True) + EPS) * gs_ref[...]
            o = o * (1.0 - lam_init)
        o_ref[0, :, h * HEAD_DIM:(h + 1) * HEAD_DIM] = o.astype(o_ref.dtype)


def _attention(main, mode, tq, *, mask=None, tab=None, lam=None, g_subln=None,
               fcol=None, frow=None, lam_init=0.0):
    Bn, S, _ = main.shape
    blk_q, blk_k, blk_v, head_off = {
        "dsa": (BLK_QA, BLK_KA, BLK_VA, 0),
        "diff": (BLK_QB, BLK_KB, BLK_VB, N_HEADS),
        "fox": (BLK_QD, BLK_KD, BLK_VD, 0),
    }[mode]
    nmap = 2 if mode == "diff" else 1
    W = N_HEADS * HEAD_DIM
    args = [main, main, main]
    specs = [pl.BlockSpec((1, tq, W), lambda b, i: (b, i, blk_q // N_HEADS)),
             pl.BlockSpec((1, S, W), lambda b, i: (b, 0, blk_k // N_HEADS)),
             pl.BlockSpec((1, S, W), lambda b, i: (b, 0, blk_v // N_HEADS))]
    scratch = [pltpu.VMEM((N_HEADS * nmap, tq, LANE), jnp.float32),
               pltpu.VMEM((N_HEADS * nmap, tq, 2 * HEAD_DIM), jnp.float32)]
    if mode == "dsa":
        args.append(mask)
        specs.append(pl.BlockSpec((1, tq, S), lambda b, i: (b, i, 0)))
    if mode in ("dsa", "diff"):
        args.append(tab)
        specs.append(pl.BlockSpec(memory_space=pltpu.SMEM))
        scratch.append(pltpu.VMEM((N_HEADS, tq, 2 * tq), jnp.float32))
    if mode == "diff":
        args += [lam, g_subln.reshape(1, HEAD_DIM)]
        specs += [pl.BlockSpec((4, DIFF_DIM), lambda b, i: (0, 0)),
                  pl.BlockSpec((1, HEAD_DIM), lambda b, i: (0, 0))]
    if mode == "fox":
        args += [fcol, frow]
        specs += [pl.BlockSpec((1, tq, LANE), lambda b, i: (b, i, 0)),
                  pl.BlockSpec((1, N_HEADS, S), lambda b, i: (b, 0, 0))]
    return pl.pallas_call(
        functools.partial(_attn_kernel, mode=mode, head_off=head_off, lam_init=lam_init),
        grid=(Bn, S // tq),
        in_specs=specs,
        out_specs=pl.BlockSpec((1, tq, W), lambda b, i: (b, i, 0)),
        out_shape=jax.ShapeDtypeStruct((Bn, S, MIX_W), jnp.bfloat16),
        scratch_shapes=scratch,
        name="attn_" + mode,
        compiler_params=_cparams(("parallel", "arbitrary")),
    )(*args)


def _conv_kernel(bg_ref, cg_ref, xin_ref, w_ref, o_ref):
    u = cg_ref[0] * xin_ref[0]
    row = lax.broadcasted_iota(jnp.int32, u.shape, 0)
    w = w_ref[...]
    y = u * w[CONV_K - 1:CONV_K, :]
    for d in range(1, CONV_K):
        shifted = jnp.where(row >= d, pltpu.roll(u, d, 0), 0.0)
        y = y + shifted * w[CONV_K - 1 - d:CONV_K - d, :]
    o_ref[0] = (bg_ref[0] * y).astype(o_ref.dtype)


def _short_conv(f32grp, conv_w):
    Bn, S, _ = f32grp.shape
    tc = LANE
    per = CONV_CH // tc
    return pl.pallas_call(
        _conv_kernel,
        grid=(Bn, per),
        in_specs=[pl.BlockSpec((1, S, tc), lambda b, j: (b, 0, j)),
                  pl.BlockSpec((1, S, tc), lambda b, j: (b, 0, per + j)),
                  pl.BlockSpec((1, S, tc), lambda b, j: (b, 0, 2 * per + j)),
                  pl.BlockSpec((CONV_K, tc), lambda b, j: (0, j))],
        out_specs=pl.BlockSpec((1, S, tc), lambda b, j: (b, 0, j)),
        out_shape=jax.ShapeDtypeStruct((Bn, S, CONV_CH), jnp.bfloat16),
        name="short_conv",
        compiler_params=_cparams(("parallel", "parallel")),
    )(f32grp, f32grp, f32grp, conv_w)


def _mix_kernel(*refs):
    z_refs = refs[0:N_BRANCH]
    y_refs = refs[N_BRANCH:2 * N_BRANCH]
    wb_refs = refs[2 * N_BRANCH:3 * N_BRANCH]
    o_ref = refs[3 * N_BRANCH]
    merged = None
    for g in range(N_BRANCH):
        gate = jax.nn.sigmoid(z_refs[g][...].astype(jnp.float32))
        proj = jnp.dot(y_refs[g][...], wb_refs[g][...], preferred_element_type=jnp.float32)
        merged = gate * proj if merged is None else merged + gate * proj
    o_ref[...] = merged.astype(o_ref.dtype)


def _mix(z, ys, w_branch, l):
    T = z.shape[0]
    D = w_branch.shape[3]
    tm = _tile(T, 1024)
    tn = _tile(D, 512)
    per = D // tn
    specs = [pl.BlockSpec((tm, tn), lambda i, n, g=g: (i, g * per + n)) for g in range(N_BRANCH)]
    specs += [pl.BlockSpec((tm, MIX_W), lambda i, n: (i, 0)) for _ in range(N_BRANCH)]
    specs += [pl.BlockSpec((None, None, MIX_W, tn), lambda i, n, g=g: (l, g, 0, n))
              for g in range(N_BRANCH)]
    return pl.pallas_call(
        _mix_kernel,
        grid=(T // tm, per),
        in_specs=specs,
        out_specs=pl.BlockSpec((tm, tn), lambda i, n: (i, n)),
        out_shape=jax.ShapeDtypeStruct((T, D), jnp.bfloat16),
        name="gated_mix",
        compiler_params=_cparams(("parallel", "parallel")),
    )(*([z] * N_BRANCH), *ys, *([w_branch] * N_BRANCH))


def _ffn_kernel(h_ref, w1_ref, w2_ref, o_ref, u_ref, *, nf):
    f = pl.program_id(1)

    def first(slot):
        u = jnp.dot(h_ref[...], w1_ref[...], preferred_element_type=jnp.float32)
        u_ref[slot] = jnp.square(jnp.maximum(u, 0.0)).astype(u_ref.dtype)

    def second(slot):
        return jnp.dot(u_ref[slot], w2_ref[...], preferred_element_type=jnp.float32)

    @pl.when(f == 0)
    def _():
        o_ref[...] = jnp.zeros(o_ref.shape, o_ref.dtype)
        first(0)

    for parity in range(2):
        @pl.when((f > 0) & (f < nf) & (f % 2 == parity))
        def _(parity=parity):
            part = second(1 - parity)
            first(parity)
            o_ref[...] += part

    @pl.when(f == nf)
    def _():
        o_ref[...] += second((nf - 1) % 2)


def _ffn(h, w1, w2, l):
    T, D = h.shape
    F = w1.shape[2]
    tm = _tile(T, 1024)
    tf = _tile(F, 512)
    nf = F // tf
    return pl.pallas_call(
        functools.partial(_ffn_kernel, nf=nf),
        grid=(T // tm, nf + 1),
        in_specs=[pl.BlockSpec((tm, D), lambda i, f: (i, 0)),
                  pl.BlockSpec((None, D, tf), lambda i, f: (l, 0, jnp.minimum(f, nf - 1))),
                  pl.BlockSpec((None, tf, D), lambda i, f: (l, jnp.maximum(f - 1, 0), 0))],
        out_specs=pl.BlockSpec((tm, D), lambda i, f: (i, 0)),
        out_shape=jax.ShapeDtypeStruct((T, D), jnp.float32),
        scratch_shapes=[pltpu.VMEM((2, tm, tf), jnp.bfloat16)],
        name="ffn",
        compiler_params=_cparams(("parallel", "arbitrary")),
    )(h, w1, w2)


def _split_in_proj(w_in):
    W = N_HEADS * HEAD_DIM
    sizes = (3 * W,
             IDX_HEADS * IDX_DIM + IDX_DIM,
             IDX_HEADS,
             3 * W,
             3 * CONV_CH,
             3 * W,
             N_HEADS)
    offs = np.concatenate([[0], np.cumsum(sizes)])
    a, idx, iw, b, cv, d, fd = [w_in[:, :, offs[j]:offs[j + 1]] for j in range(7)]
    L, D, _ = w_in.shape
    zeros = lambda n: jnp.zeros((L, D, n), w_in.dtype)
    main = jnp.concatenate([a, b, d, idx, zeros(N_MAIN - 9 * W - sizes[1])], axis=-1)
    f32grp = jnp.concatenate([cv, iw, fd, zeros(LANE - IDX_HEADS - N_HEADS)], axis=-1)
    return main.astype(jnp.bfloat16), f32grp.astype(jnp.bfloat16)


def kernel(x, c, w_ada, b_ada, g_norm1, w_in, b_f, conv_w, lam, g_subln, rel_bias,
           w_gate, b_gate, w_branch, w_o, g_norm2, w_ff1, w_ff2, g_final):
    Bn, S, D = x.shape
    L = w_ada.shape[0]
    T = Bn * S
    k_top = min(TOPK_MAX, S // 4)
    tq = _tile(S, 256)
    bf = jnp.bfloat16

    w_main, w_f32 = _split_in_proj(w_in)
    w_gate_b, w_branch_b, w_o_b = w_gate.astype(bf), w_branch.astype(bf), w_o.astype(bf)
    w_ff1_b, w_ff2_b = w_ff1.astype(bf), w_ff2.astype(bf)

    mod = _modulation(c, w_ada, b_ada).reshape(L, Bn, 6, 1, D)
    xt = x.reshape(T, D)
    delta, gate = None, None
    for l in range(L):
        sh1, sc1, gt1, sh2, sc2, gt2 = [mod[l, :, j] for j in range(6)]
        x_new, h = _norm(xt, g_norm1[l], seq=S, delta=delta, gate=gate, sc=sc1, sh=sh1)
        xt = xt if x_new is None else x_new
        main = _mm(h, w_main, l, bf, 1024).reshape(Bn, S, N_MAIN)
        f32grp = _mm(h, w_f32, l, jnp.float32, N_F32).reshape(Bn, S, N_F32)

        b_pad = jnp.zeros((1, LANE), jnp.float32).at[0, COL_FD:COL_FD + N_HEADS].set(b_f[l])
        fcol = _forget_cumsum(f32grp, b_pad)
        frow = jnp.transpose(fcol[:, :, COL_FD:COL_FD + N_HEADS], (0, 2, 1))

        mask, z = _select_and_gates(main, f32grp, h, w_gate_b, b_gate.reshape(L, 1, N_BRANCH * D),
                                    l, k_top, _tile(S, 512))
        lam_init = 0.8 - 0.6 * math.exp(-0.3 * l)
        y_a = _attention(main, "dsa", tq, mask=mask, tab=rel_bias)
        y_b = _attention(main, "diff", tq, tab=rel_bias, lam=lam[l], g_subln=g_subln[l],
                         lam_init=lam_init)
        y_c = _short_conv(f32grp, conv_w[l])
        y_d = _attention(main, "fox", tq, fcol=fcol, frow=frow)
        ys = [y.reshape(T, MIX_W) for y in (y_a, y_b, y_c, y_d)]

        merged = _mix(z, ys, w_branch_b, l)
        xt, h2 = _mm_norm(merged, w_o_b, l, xt, gt1, g_norm2[l], sc2, sh2, seq=S)
        delta = _ffn(h2, w_ff1_b, w_ff2_b, l)
        gate = gt2
    _, out = _norm(xt, g_final, seq=S, delta=delta, gate=gate, out_dtype=jnp.float32,
                   keep_x=False)
    return out.reshape(Bn, S, D)
```

```python
import functools
import math

import numpy as np
import jax
import jax.numpy as jnp
from jax import lax
from jax.experimental import pallas as pl
from jax.experimental.pallas import tpu as pltpu

HEAD_DIM = 128
N_HEADS = 4
DIFF_DIM = HEAD_DIM // 2
CONV_CH = 512
CONV_K = 3
IDX_HEADS = 4
IDX_DIM = 64
TOPK_MAX = 256
MIX_W = 512
N_BRANCH = 4
N_BUCKETS = 32
MAX_DIST = 128
EPS = 1e-6
NEG = -1e30
IDX_SCALE = (IDX_DIM ** -0.5) * (IDX_HEADS ** -0.5)
INT_MIN = -(2 ** 31)
LOG2E = 1.0 / math.log(2.0)

LANE = 128
VMEM_LIMIT = 56 * 1024 * 1024

BLK_QA, BLK_KA, BLK_VA = 0, 4, 8
BLK_QB, BLK_KB, BLK_VB = 12, 16, 20
BLK_QD, BLK_KD, BLK_VD = 24, 28, 32
BLK_IQ = 36
BLK_IK = 38
N_MAIN = 40 * LANE
N_F32 = 3 * CONV_CH + LANE
BLK_SMALL = 3 * CONV_CH // LANE
COL_IW, COL_FD = 0, 4


def _cparams(sem):
    return pltpu.CompilerParams(dimension_semantics=sem, vmem_limit_bytes=VMEM_LIMIT)


def _tile(n, pref):
    t = min(n, pref)
    while n % t:
        t -= 1
    return t


def _mod_kernel(c_ref, w_ref, b_ref, o_ref):
    c = c_ref[...]
    ca = (c * jax.nn.sigmoid(c)).astype(jnp.bfloat16)
    w = w_ref[0].astype(jnp.bfloat16)
    o_ref[0] = jnp.dot(ca, w, preferred_element_type=jnp.float32) + b_ref[0]


def _modulation(c, w_ada, b_ada):
    L, D, N = w_ada.shape
    Bn = c.shape[0]
    tn = _tile(N, 1024)
    return pl.pallas_call(
        _mod_kernel,
        grid=(L, N // tn),
        in_specs=[
            pl.BlockSpec((Bn, D), lambda l, j: (0, 0)),
            pl.BlockSpec((1, D, tn), lambda l, j: (l, 0, j)),
            pl.BlockSpec((1, 1, tn), lambda l, j: (l, 0, j)),
        ],
        out_specs=pl.BlockSpec((1, Bn, tn), lambda l, j: (l, 0, j)),
        out_shape=jax.ShapeDtypeStruct((L, Bn, N), jnp.float32),
        name="ada_mod",
        compiler_params=_cparams(("parallel", "parallel")),
    )(c, w_ada, b_ada.reshape(L, 1, N))


def _norm_kernel(*refs, has_delta):
    it = iter(refs)
    x_ref = next(it)
    d_ref = next(it) if has_delta else None
    gate_ref = next(it) if has_delta else None
    g_ref, sc_ref, sh_ref = next(it), next(it), next(it)
    xo_ref = next(it) if has_delta else None
    h_ref = next(it)

    x = x_ref[...]
    if has_delta:
        x = x + gate_ref[0] * d_ref[...]
        xo_ref[...] = x
    h_ref[...] = _rms_mod(x, g_ref[...], sc_ref[0], sh_ref[0]).astype(h_ref.dtype)


def _norm(x, g, sc, sh, *, seq, delta=None, gate=None):
    T, D = x.shape
    tm = _tile(seq, 512)
    per_seq = seq // tm
    has_delta = delta is not None
    row = pl.BlockSpec((tm, D), lambda i: (i, 0))
    per_b = pl.BlockSpec((1, 1, D), lambda i: (i // per_seq, 0, 0))
    args, specs = [x], [row]
    if has_delta:
        args += [delta, gate]
        specs += [row, per_b]
    args += [g.reshape(1, D), sc, sh]
    specs += [pl.BlockSpec((1, D), lambda i: (0, 0)), per_b, per_b]
    out_shape, out_specs = [], []
    if has_delta:
        out_shape.append(jax.ShapeDtypeStruct((T, D), jnp.float32))
        out_specs.append(row)
    out_shape.append(jax.ShapeDtypeStruct((T, D), jnp.bfloat16))
    out_specs.append(row)
    res = pl.pallas_call(
        functools.partial(_norm_kernel, has_delta=has_delta),
        grid=(T // tm,),
        in_specs=specs,
        out_specs=out_specs,
        out_shape=out_shape,
        name="norm",
        compiler_params=_cparams(("parallel",)),
    )(*args)
    if has_delta:
        return res[0], res[1]
    return None, res[0]


def _rms_mod(x, g, sc, sh):
    y = x * lax.rsqrt(jnp.mean(x * x, axis=-1, keepdims=True) + EPS) * g
    return y * (1.0 + sc) + sh


def _mm_norm_kernel(a_ref, w_ref, x_ref, gate_ref, g_ref, sc_ref, sh_ref, xo_ref, h_ref):
    acc = jnp.dot(a_ref[...], w_ref[...], preferred_element_type=jnp.float32)
    x = x_ref[...] + gate_ref[0] * acc
    xo_ref[...] = x
    h_ref[...] = _rms_mod(x, g_ref[...], sc_ref[0], sh_ref[0]).astype(h_ref.dtype)


def _mm_norm(a, w, l, x, gate, g, sc, sh, *, seq):
    T, D = x.shape
    K = a.shape[1]
    tm = _tile(seq, 512)
    per_seq = seq // tm
    row = pl.BlockSpec((tm, D), lambda i: (i, 0))
    per_b = pl.BlockSpec((1, 1, D), lambda i: (i // per_seq, 0, 0))
    return pl.pallas_call(
        _mm_norm_kernel,
        grid=(T // tm,),
        in_specs=[pl.BlockSpec((tm, K), lambda i: (i, 0)),
                  pl.BlockSpec((None, K, D), lambda i: (l, 0, 0)),
                  row, per_b, pl.BlockSpec((1, D), lambda i: (0, 0)), per_b, per_b],
        out_specs=[row, row],
        out_shape=[jax.ShapeDtypeStruct((T, D), jnp.float32),
                   jax.ShapeDtypeStruct((T, D), jnp.bfloat16)],
        name="outproj_norm",
        compiler_params=_cparams(("parallel",)),
    )(a, w, x, gate, g.reshape(1, D), sc, sh)


def _mm_kernel(a_ref, w_ref, o_ref):
    o_ref[...] = jnp.dot(a_ref[...], w_ref[...],
                         preferred_element_type=jnp.float32).astype(o_ref.dtype)


def _mm(a, w, l, out_dtype, tn_pref):
    M, K = a.shape
    N = w.shape[2]
    tm = _tile(M, 1024)
    tn = _tile(N, tn_pref)
    return pl.pallas_call(
        _mm_kernel,
        grid=(M // tm, N // tn),
        in_specs=[pl.BlockSpec((tm, K), lambda i, j: (i, 0)),
                  pl.BlockSpec((None, K, tn), lambda i, j: (l, 0, j))],
        out_specs=pl.BlockSpec((tm, tn), lambda i, j: (i, j)),
        out_shape=jax.ShapeDtypeStruct((M, N), out_dtype),
        name="matmul",
        compiler_params=_cparams(("parallel", "parallel")),
    )(a, w)


def _cumsum_kernel(s_ref, b_ref, o_ref):
    S = s_ref.shape[1]
    x = s_ref[0] + b_ref[...]
    ls = jnp.minimum(x, 0.0) - jnp.log(1.0 + jnp.exp(-jnp.abs(x)))
    r = lax.broadcasted_iota(jnp.int32, (LANE, LANE), 0)
    c = lax.broadcasted_iota(jnp.int32, (LANE, LANE), 1)
    tri = jnp.where(r >= c, 1.0, 0.0).astype(jnp.bfloat16)
    hi = ls.astype(jnp.bfloat16)
    r1 = ls - hi.astype(jnp.float32)
    mid = r1.astype(jnp.bfloat16)
    lo = (r1 - mid.astype(jnp.float32)).astype(jnp.bfloat16)
    carry = jnp.zeros((1, LANE), jnp.float32)
    for j in range(S // LANE):
        rows = slice(j * LANE, (j + 1) * LANE)
        blk = (jnp.dot(tri, hi[rows], preferred_element_type=jnp.float32)
               + jnp.dot(tri, mid[rows], preferred_element_type=jnp.float32)
               + jnp.dot(tri, lo[rows], preferred_element_type=jnp.float32)) + carry
        o_ref[0, j * LANE:(j + 1) * LANE, :] = blk
        carry = blk[LANE - 1:LANE, :]


def _forget_cumsum(small, b_pad):
    Bn, S, _ = small.shape
    return pl.pallas_call(
        _cumsum_kernel,
        grid=(Bn,),
        in_specs=[pl.BlockSpec((1, S, LANE), lambda b: (b, 0, BLK_SMALL)),
                  pl.BlockSpec((1, LANE), lambda b: (0, 0))],
        out_specs=pl.BlockSpec((1, S, LANE), lambda b: (b, 0, 0)),
        out_shape=jax.ShapeDtypeStruct((Bn, S, LANE), jnp.float32),
        name="forget_cumsum",
        compiler_params=_cparams(("parallel",)),
    )(small, b_pad)


N_KEY_BITS = 32


def _select_block(iq_ref, ik_ref, sm_ref, h_ref, wg_ref, bg_ref, mask_ref, z_ref, key_ref, thr_ref,
                  *, k_top, blk, n_chunks):
    c = pl.program_id(2)
    tq = iq_ref.shape[1]
    S = ik_ref.shape[1]
    W = (blk + 1) * tq
    kf = jnp.float32(k_top)
    causal = (lax.broadcasted_iota(jnp.int32, (tq, W), 1)
              <= blk * tq + lax.broadcasted_iota(jnp.int32, (tq, W), 0))

    @pl.when(c == 0)
    def _():
        iq = iq_ref[0]
        ik = ik_ref[0, :W, :IDX_DIM]
        iw = sm_ref[0][:, COL_IW:COL_IW + IDX_HEADS] * IDX_SCALE
        score = jnp.zeros((tq, W), jnp.float32)
        for h in range(IDX_HEADS):
            d = lax.dot_general(iq[:, h * IDX_DIM:(h + 1) * IDX_DIM], ik,
                                (((1,), (1,)), ((), ())), preferred_element_type=jnp.float32)
            score = score + jnp.where(d > 0.0, d * iw[:, h:h + 1], 0.0)
        score = jnp.where(causal, score, NEG)
        bits = lax.bitcast_convert_type(score, jnp.int32)
        key_ref[:, :W] = bits ^ ((bits >> 31) & 0x7FFFFFFF)
        thr_ref[...] = jnp.full(thr_ref.shape, INT_MIN, jnp.int32)

    per_step = N_KEY_BITS // n_chunks
    tz = z_ref.shape[1] // per_step
    thr = thr_ref[...]
    for j in range(per_step):
        zs = slice(j * tz, (j + 1) * tz)
        z = jnp.dot(h_ref[...], wg_ref[:, zs], preferred_element_type=jnp.float32) + bg_ref[:, zs]
        p = c * per_step + j
        cand = jnp.where(p == 0, 0, thr | jnp.left_shift(jnp.int32(1), N_KEY_BITS - 1 - p))
        cnt = jnp.sum(jnp.where(key_ref[:, :W] >= cand, 1.0, 0.0), axis=1, keepdims=True)
        thr = jnp.where(cnt >= kf, cand, thr)
        z_ref[:, zs] = z.astype(z_ref.dtype)
    thr_ref[...] = thr

    @pl.when(c == n_chunks - 1)
    def _():
        _select_write(mask_ref, key_ref, thr_ref[...], causal, kf, blk)


def _select_write(mask_ref, key_ref, thr, causal, kf, blk):
    tq, S = mask_ref.shape[1], mask_ref.shape[2]
    W = (blk + 1) * tq
    key = key_ref[:, :W]
    above = key > thr
    need = kf - jnp.sum(jnp.where(above, 1.0, 0.0), axis=1, keepdims=True)
    upper = jnp.where(lax.broadcasted_iota(jnp.int32, (tq, tq), 0)
                      <= lax.broadcasted_iota(jnp.int32, (tq, tq), 1), 1.0, 0.0).astype(jnp.bfloat16)
    seen = jnp.zeros((tq, 1), jnp.float32)
    for c in range(blk + 1):
        cs = slice(c * tq, (c + 1) * tq)
        tie = key[:, cs] == thr
        incl = seen + jnp.dot(jnp.where(tie, 1.0, 0.0).astype(jnp.bfloat16), upper,
                              preferred_element_type=jnp.float32)
        sel = above[:, cs] | (tie & (incl <= need))
        if c == blk:
            sel = sel & causal[:, cs]
        mask_ref[0, :, cs] = jnp.where(sel, 1.0, 0.0).astype(mask_ref.dtype)
        seen = incl[:, tq - 1:tq]
    if W < S:
        mask_ref[0, :, W:] = jnp.zeros((tq, S - W), mask_ref.dtype)


def _select_kernel(*refs, k_top, n_chunks):
    iq_ref, ik_ref = refs[0], refs[1]
    i = pl.program_id(1)
    for blk in range(ik_ref.shape[1] // iq_ref.shape[1]):
        @pl.when(i == blk)
        def _(blk=blk):
            _select_block(*refs, k_top=k_top, blk=blk, n_chunks=n_chunks)


def _select_and_gates(main, f32grp, h, w_gate, b_gate, l, k_top, tq):
    Bn, S, _ = main.shape
    T, D = h.shape
    N = w_gate.shape[2]
    nq = S // tq
    tc = _tile(N, 1024)
    n_chunks = N // tc
    assert N_KEY_BITS % n_chunks == 0
    return pl.pallas_call(
        functools.partial(_select_kernel, k_top=k_top, n_chunks=n_chunks),
        grid=(Bn, nq, n_chunks),
        in_specs=[pl.BlockSpec((1, tq, 2 * LANE), lambda b, i, c: (b, i, BLK_IQ // 2)),
                  pl.BlockSpec((1, S, LANE), lambda b, i, c: (b, 0, BLK_IK)),
                  pl.BlockSpec((1, tq, LANE), lambda b, i, c: (b, i, BLK_SMALL)),
                  pl.BlockSpec((tq, D), lambda b, i, c: (b * nq + i, 0)),
                  pl.BlockSpec((None, D, tc), lambda b, i, c: (l, 0, c)),
                  pl.BlockSpec((None, 1, tc), lambda b, i, c: (l, 0, c))],
        out_specs=[pl.BlockSpec((1, tq, S), lambda b, i, c: (b, i, 0)),
                   pl.BlockSpec((tq, tc), lambda b, i, c: (b * nq + i, c))],
        out_shape=[jax.ShapeDtypeStruct((Bn, S, S), jnp.bfloat16),
                   jax.ShapeDtypeStruct((T, N), jnp.bfloat16)],
        scratch_shapes=[pltpu.VMEM((tq, S), jnp.int32), pltpu.VMEM((tq, 1), jnp.int32)],
        name="dsa_select_gates",
        compiler_params=_cparams(("parallel", "parallel", "arbitrary")),
    )(main, main, f32grp, h, w_gate, b_gate)


def _t5_thresholds():
    max_exact = N_BUCKETS // 2
    rel = np.arange(MAX_DIST)
    nf = np.maximum(rel, 1).astype(np.float64)
    large = max_exact + (np.log(nf / max_exact) / math.log(MAX_DIST / max_exact)
                         * (N_BUCKETS - max_exact)).astype(np.int64)
    bucket = np.where(rel < max_exact, rel, np.minimum(large, N_BUCKETS - 1))
    out = []
    for b in range(1, N_BUCKETS):
        hit = np.nonzero(bucket >= b)[0]
        out.append(int(hit[0]) if hit.size else MAX_DIST)
    return out


def _attn_kernel(*refs, mode, head_off, lam_init):
    it = iter(refs)
    q_ref, k_ref, v_ref = next(it), next(it), next(it)
    if mode == "dsa":
        mask_ref = next(it)
    if mode in ("dsa", "diff"):
        tab_ref = next(it)
    if mode == "diff":
        lam_ref, gs_ref = next(it), next(it)
    if mode == "fox":
        fc_ref, fr_ref = next(it), next(it)
    o_ref = next(it)
    m_ref, acc_ref = next(it), next(it)
    if mode in ("dsa", "diff"):
        band_ref = next(it)

    i = pl.program_id(1)
    tq = q_ref.shape[1]
    nmap = 2 if mode == "diff" else 1
    dq = HEAD_DIM // nmap
    scale = dq ** -0.5
    biased = mode in ("dsa", "diff")

    c1 = scale * LOG2E
    if biased:
        @pl.when(i == 0)
        def _():
            r = lax.broadcasted_iota(jnp.int32, (tq, 2 * tq), 0)
            c = lax.broadcasted_iota(jnp.int32, (tq, 2 * tq), 1)
            rel = r - c + tq
            vals = [jnp.full((tq, 2 * tq), tab_ref[0, head_off + h] * LOG2E, jnp.float32)
                    for h in range(N_HEADS)]
            for b, first in enumerate(_t5_thresholds(), start=1):
                hit = rel >= first
                vals = [jnp.where(hit, tab_ref[b, head_off + h] * LOG2E, vals[h])
                        for h in range(N_HEADS)]
            for h in range(N_HEADS):
                band_ref[h] = vals[h]

    m_ref[...] = jnp.full(m_ref.shape, NEG, jnp.float32)
    acc_ref[...] = jnp.zeros(acc_ref.shape, jnp.float32)

    def chunk(off, kind, tk):
        ones = jnp.ones((tk, LANE), jnp.bfloat16)
        if mode == "dsa":
            keep = mask_ref[0, :, pl.ds(off, tk)].astype(jnp.float32) > 0.5
        elif kind == "far":
            keep = None
        else:
            keep = (lax.broadcasted_iota(jnp.int32, (tq, tk), 1) - (tk - tq)
                    <= lax.broadcasted_iota(jnp.int32, (tq, tk), 0))
        units = [(h, mp) for h in range(N_HEADS) for mp in range(nmap)]
        logits = []
        for h, mp in units:
            cols = slice(h * HEAD_DIM + mp * dq, h * HEAD_DIM + (mp + 1) * dq)
            logits.append(lax.dot_general(q_ref[0, :, cols], k_ref[0, pl.ds(off, tk), cols],
                                          (((1,), (1,)), ((), ())),
                                          preferred_element_type=jnp.float32))
        m_olds, m_news = [], []
        for u, (h, mp) in enumerate(units):
            t = logits[u] * c1
            if biased:
                if kind == "far":
                    t = t + tab_ref[N_BUCKETS - 1, head_off + h] * LOG2E
                elif kind == "near":
                    t = t + band_ref[h]
                else:
                    t = t + band_ref[h, :, tq:]
            else:
                t = (t + fc_ref[0, :, COL_FD + h:COL_FD + h + 1] * LOG2E
                     - fr_ref[0, h:h + 1, pl.ds(off, tk)] * LOG2E)
            if keep is not None:
                t = jnp.where(keep, t, NEG)
            logits[u] = t
            m_old = m_ref[u]
            m_olds.append(m_old)
            m_news.append(jnp.maximum(m_old, jnp.max(t, axis=1, keepdims=True)))
        probs = []
        for u in range(len(units)):
            probs.append(jnp.exp2(logits[u] - jnp.tile(m_news[u], (1, tk // LANE))).astype(v_ref.dtype))
            m_ref[u] = m_news[u]
        for u, (h, mp) in enumerate(units):
            vc = jnp.concatenate(
                [v_ref[0, pl.ds(off, tk), h * HEAD_DIM:(h + 1) * HEAD_DIM], ones], axis=1)
            alpha = jnp.exp2(m_olds[u] - m_news[u])
            acc_ref[u] = (jnp.tile(alpha, (1, 2)) * acc_ref[u]
                          + jnp.dot(probs[u], vc, preferred_element_type=jnp.float32))

    n_far = jnp.maximum(i - 1, 0)

    def far_body(c, carry):
        chunk(pl.multiple_of(c * 2 * tq, 2 * tq), "far", 2 * tq)
        return carry

    lax.fori_loop(0, n_far // 2, far_body, 0)

    @pl.when(n_far % 2 == 1)
    def _():
        chunk(pl.multiple_of((n_far - 1) * tq, tq), "far", tq)

    @pl.when(i >= 1)
    def _():
        chunk(pl.multiple_of((i - 1) * tq, tq), "near", 2 * tq)

    @pl.when(i == 0)
    def _():
        chunk(0, "diag", tq)

    if mode == "diff":
        lq = lam_ref[...]
        lam_val = (jnp.exp(jnp.sum(lq[0:1] * lq[1:2], axis=1, keepdims=True))
                   - jnp.exp(jnp.sum(lq[2:3] * lq[3:4], axis=1, keepdims=True)) + lam_init)
    for h in range(N_HEADS):
        a = acc_ref[h * nmap]
        o = a[:, :HEAD_DIM] / a[:, HEAD_DIM:]
        if mode == "diff":
            a1 = acc_ref[h * nmap + 1]
            o = o - lam_val * (a1[:, :HEAD_DIM] / a1[:, HEAD_DIM:])
            o = o * lax.rsqrt(jnp.mean(o * o, axis=-1, keepdims=True) + EPS) * gs_ref[...]
            o = o * (1.0 - lam_init)
        o_ref[0, :, h * HEAD_DIM:(h + 1) * HEAD_DIM] = o.astype(o_ref.dtype)


def _attention(main, mode, tq, *, mask=None, tab=None, lam=None, g_subln=None,
               fcol=None, frow=None, lam_init=0.0):
    Bn, S, _ = main.shape
    blk_q, blk_k, blk_v, head_off = {
        "dsa": (BLK_QA, BLK_KA, BLK_VA, 0),
        "diff": (BLK_QB, BLK_KB, BLK_VB, N_HEADS),
        "fox": (BLK_QD, BLK_KD, BLK_VD, 0),
    }[mode]
    nmap = 2 if mode == "diff" else 1
    W = N_HEADS * HEAD_DIM
    args = [main, main, main]
    specs = [pl.BlockSpec((1, tq, W), lambda b, i: (b, i, blk_q // N_HEADS)),
             pl.BlockSpec((1, S, W), lambda b, i: (b, 0, blk_k // N_HEADS)),
             pl.BlockSpec((1, S, W), lambda b, i: (b, 0, blk_v // N_HEADS))]
    scratch = [pltpu.VMEM((N_HEADS * nmap, tq, LANE), jnp.float32),
               pltpu.VMEM((N_HEADS * nmap, tq, 2 * HEAD_DIM), jnp.float32)]
    if mode == "dsa":
        args.append(mask)
        specs.append(pl.BlockSpec((1, tq, S), lambda b, i: (b, i, 0)))
    if mode in ("dsa", "diff"):
        args.append(tab)
        specs.append(pl.BlockSpec(memory_space=pltpu.SMEM))
        scratch.append(pltpu.VMEM((N_HEADS, tq, 2 * tq), jnp.float32))
    if mode == "diff":
        args += [lam, g_subln.reshape(1, HEAD_DIM)]
        specs += [pl.BlockSpec((4, DIFF_DIM), lambda b, i: (0, 0)),
                  pl.BlockSpec((1, HEAD_DIM), lambda b, i: (0, 0))]
    if mode == "fox":
        args += [fcol, frow]
        specs += [pl.BlockSpec((1, tq, LANE), lambda b, i: (b, i, 0)),
                  pl.BlockSpec((1, N_HEADS, S), lambda b, i: (b, 0, 0))]
    return pl.pallas_call(
        functools.partial(_attn_kernel, mode=mode, head_off=head_off, lam_init=lam_init),
        grid=(Bn, S // tq),
        in_specs=specs,
        out_specs=pl.BlockSpec((1, tq, W), lambda b, i: (b, i, 0)),
        out_shape=jax.ShapeDtypeStruct((Bn, S, MIX_W), jnp.bfloat16),
        scratch_shapes=scratch,
        name="attn_" + mode,
        compiler_params=_cparams(("parallel", "arbitrary")),
    )(*args)


def _conv_kernel(bg_ref, cg_ref, xin_ref, w_ref, o_ref):
    u = cg_ref[0] * xin_ref[0]
    row = lax.broadcasted_iota(jnp.int32, u.shape, 0)
    w = w_ref[...]
    y = u * w[CONV_K - 1:CONV_K, :]
    for d in range(1, CONV_K):
        shifted = jnp.where(row >= d, pltpu.roll(u, d, 0), 0.0)
        y = y + shifted * w[CONV_K - 1 - d:CONV_K - d, :]
    o_ref[0] = (bg_ref[0] * y).astype(o_ref.dtype)


def _short_conv(f32grp, conv_w):
    Bn, S, _ = f32grp.shape
    tc = LANE
    per = CONV_CH // tc
    return pl.pallas_call(
        _conv_kernel,
        grid=(Bn, per),
        in_specs=[pl.BlockSpec((1, S, tc), lambda b, j: (b, 0, j)),
                  pl.BlockSpec((1, S, tc), lambda b, j: (b, 0, per + j)),
                  pl.BlockSpec((1, S, tc), lambda b, j: (b, 0, 2 * per + j)),
                  pl.BlockSpec((CONV_K, tc), lambda b, j: (0, j))],
        out_specs=pl.BlockSpec((1, S, tc), lambda b, j: (b, 0, j)),
        out_shape=jax.ShapeDtypeStruct((Bn, S, CONV_CH), jnp.bfloat16),
        name="short_conv",
        compiler_params=_cparams(("parallel", "parallel")),
    )(f32grp, f32grp, f32grp, conv_w)


def _mix_kernel(*refs):
    z_refs = refs[0:N_BRANCH]
    y_refs = refs[N_BRANCH:2 * N_BRANCH]
    wb_refs = refs[2 * N_BRANCH:3 * N_BRANCH]
    o_ref = refs[3 * N_BRANCH]
    merged = None
    for g in range(N_BRANCH):
        gate = jax.nn.sigmoid(z_refs[g][...].astype(jnp.float32))
        proj = jnp.dot(y_refs[g][...], wb_refs[g][...], preferred_element_type=jnp.float32)
        merged = gate * proj if merged is None else merged + gate * proj
    o_ref[...] = merged.astype(o_ref.dtype)


def _mix(z, ys, w_branch, l):
    T = z.shape[0]
    D = w_branch.shape[3]
    tm = _tile(T, 1024)
    tn = _tile(D, 512)
    per = D // tn
    specs = [pl.BlockSpec((tm, tn), lambda i, n, g=g: (i, g * per + n)) for g in range(N_BRANCH)]
    specs += [pl.BlockSpec((tm, MIX_W), lambda i, n: (i, 0)) for _ in range(N_BRANCH)]
    specs += [pl.BlockSpec((None, None, MIX_W, tn), lambda i, n, g=g: (l, g, 0, n))
              for g in range(N_BRANCH)]
    return pl.pallas_call(
        _mix_kernel,
        grid=(T // tm, per),
        in_specs=specs,
        out_specs=pl.BlockSpec((tm, tn), lambda i, n: (i, n)),
        out_shape=jax.ShapeDtypeStruct((T, D), jnp.bfloat16),
        name="gated_mix",
        compiler_params=_cparams(("parallel", "parallel")),
    )(*([z] * N_BRANCH), *ys, *([w_branch] * N_BRANCH))


def _ffn_kernel(*refs, nf, final):
    if final:
        h_ref, w1_ref, w2_ref, x_ref, gate_ref, g_ref, o_ref, u_ref = refs
    else:
        h_ref, w1_ref, w2_ref, o_ref, u_ref = refs
    f = pl.program_id(1)

    def first(slot):
        u = jnp.dot(h_ref[...], w1_ref[...], preferred_element_type=jnp.float32)
        u_ref[slot] = jnp.square(jnp.maximum(u, 0.0)).astype(u_ref.dtype)

    def second(slot):
        return jnp.dot(u_ref[slot], w2_ref[...], preferred_element_type=jnp.float32)

    @pl.when(f == 0)
    def _():
        o_ref[...] = jnp.zeros(o_ref.shape, o_ref.dtype)
        first(0)

    for parity in range(2):
        @pl.when((f > 0) & (f < nf) & (f % 2 == parity))
        def _(parity=parity):
            part = second(1 - parity)
            first(parity)
            o_ref[...] += part

    @pl.when(f == nf)
    def _():
        acc = o_ref[...] + second((nf - 1) % 2)
        if final:
            x = x_ref[...] + gate_ref[0] * acc
            acc = x * lax.rsqrt(jnp.mean(x * x, axis=-1, keepdims=True) + EPS) * g_ref[...]
        o_ref[...] = acc


def _ffn(h, w1, w2, l, *, seq, x=None, gate=None, g=None):
    T, D = h.shape
    F = w1.shape[2]
    final = x is not None
    tm = _tile(seq, 1024)
    tf = _tile(F, 512 if final else 1024)
    nf = F // tf
    per_seq = seq // tm
    row = pl.BlockSpec((tm, D), lambda i, f: (i, 0))
    args = [h, w1, w2]
    specs = [row,
             pl.BlockSpec((None, D, tf), lambda i, f: (l, 0, jnp.minimum(f, nf - 1))),
             pl.BlockSpec((None, tf, D), lambda i, f: (l, jnp.maximum(f - 1, 0), 0))]
    if final:
        args += [x, gate, g.reshape(1, D)]
        specs += [pl.BlockSpec((tm, D), lambda i, f: (i, 0), pipeline_mode=pl.Buffered(1)),
                  pl.BlockSpec((1, 1, D), lambda i, f: (i // per_seq, 0, 0)),
                  pl.BlockSpec((1, D), lambda i, f: (0, 0))]
    return pl.pallas_call(
        functools.partial(_ffn_kernel, nf=nf, final=final),
        grid=(T // tm, nf + 1),
        in_specs=specs,
        out_specs=row,
        out_shape=jax.ShapeDtypeStruct((T, D), jnp.float32),
        scratch_shapes=[pltpu.VMEM((2, tm, tf), jnp.bfloat16)],
        name="ffn",
        compiler_params=_cparams(("parallel", "arbitrary")),
    )(*args)


def _split_in_proj(w_in):
    W = N_HEADS * HEAD_DIM
    sizes = (3 * W,
             IDX_HEADS * IDX_DIM + IDX_DIM,
             IDX_HEADS,
             3 * W,
             3 * CONV_CH,
             3 * W,
             N_HEADS)
    offs = np.concatenate([[0], np.cumsum(sizes)])
    a, idx, iw, b, cv, d, fd = [w_in[:, :, offs[j]:offs[j + 1]] for j in range(7)]
    L, D, _ = w_in.shape
    zeros = lambda n: jnp.zeros((L, D, n), w_in.dtype)
    main = jnp.concatenate([a, b, d, idx, zeros(N_MAIN - 9 * W - sizes[1])], axis=-1)
    f32grp = jnp.concatenate([cv, iw, fd, zeros(LANE - IDX_HEADS - N_HEADS)], axis=-1)
    return main.astype(jnp.bfloat16), f32grp.astype(jnp.bfloat16)


def kernel(x, c, w_ada, b_ada, g_norm1, w_in, b_f, conv_w, lam, g_subln, rel_bias,
           w_gate, b_gate, w_branch, w_o, g_norm2, w_ff1, w_ff2, g_final):
    Bn, S, D = x.shape
    L = w_ada.shape[0]
    T = Bn * S
    k_top = min(TOPK_MAX, S // 4)
    tq = _tile(S, 256)
    bf = jnp.bfloat16

    w_main, w_f32 = _split_in_proj(w_in)
    w_gate_b, w_branch_b, w_o_b = w_gate.astype(bf), w_branch.astype(bf), w_o.astype(bf)
    w_ff1_b, w_ff2_b = w_ff1.astype(bf), w_ff2.astype(bf)

    mod = _modulation(c, w_ada, b_ada).reshape(L, Bn, 6, 1, D)
    xt = x.reshape(T, D)
    delta, gate = None, None
    for l in range(L):
        sh1, sc1, gt1, sh2, sc2, gt2 = [mod[l, :, j] for j in range(6)]
        x_new, h = _norm(xt, g_norm1[l], sc1, sh1, seq=S, delta=delta, gate=gate)
        xt = xt if x_new is None else x_new
        main = _mm(h, w_main, l, bf, 1024).reshape(Bn, S, N_MAIN)
        f32grp = _mm(h, w_f32, l, jnp.float32, N_F32).reshape(Bn, S, N_F32)

        b_pad = jnp.zeros((1, LANE), jnp.float32).at[0, COL_FD:COL_FD + N_HEADS].set(b_f[l])
        fcol = _forget_cumsum(f32grp, b_pad)
        frow = jnp.transpose(fcol[:, :, COL_FD:COL_FD + N_HEADS], (0, 2, 1))

        mask, z = _select_and_gates(main, f32grp, h, w_gate_b, b_gate.reshape(L, 1, N_BRANCH * D),
                                    l, k_top, _tile(S, 512))
        lam_init = 0.8 - 0.6 * math.exp(-0.3 * l)
        y_a = _attention(main, "dsa", tq, mask=mask, tab=rel_bias)
        y_b = _attention(main, "diff", tq, tab=rel_bias, lam=lam[l], g_subln=g_subln[l],
                         lam_init=lam_init)
        y_c = _short_conv(f32grp, conv_w[l])
        y_d = _attention(main, "fox", tq, fcol=fcol, frow=frow)
        ys = [y.reshape(T, MIX_W) for y in (y_a, y_b, y_c, y_d)]

        merged = _mix(z, ys, w_branch_b, l)
        xt, h2 = _mm_norm(merged, w_o_b, l, xt, gt1, g_norm2[l], sc2, sh2, seq=S)
        if l + 1 < L:
            delta = _ffn(h2, w_ff1_b, w_ff2_b, l, seq=S)
            gate = gt2
        else:
            out = _ffn(h2, w_ff1_b, w_ff2_b, l, seq=S, x=xt, gate=gt2, g=g_final)
    return out.reshape(Bn, S, D)
```

```python
import functools
import math

import numpy as np
import jax
import jax.numpy as jnp
from jax import lax
from jax.experimental import pallas as pl
from jax.experimental.pallas import tpu as pltpu

HEAD_DIM = 128
N_HEADS = 4
DIFF_DIM = HEAD_DIM // 2
CONV_CH = 512
CONV_K = 3
IDX_HEADS = 4
IDX_DIM = 64
TOPK_MAX = 256
MIX_W = 512
N_BRANCH = 4
N_BUCKETS = 32
MAX_DIST = 128
EPS = 1e-6
NEG = -1e30
IDX_SCALE = (IDX_DIM ** -0.5) * (IDX_HEADS ** -0.5)
INT_MIN = -(2 ** 31)
LOG2E = 1.0 / math.log(2.0)

LANE = 128
VMEM_LIMIT = 56 * 1024 * 1024

BLK_QA, BLK_KA, BLK_VA = 0, 4, 8
BLK_QB, BLK_KB, BLK_VB = 12, 16, 20
BLK_QD, BLK_KD, BLK_VD = 24, 28, 32
BLK_IQ = 36
BLK_IK = 38
N_MAIN = 40 * LANE
N_F32 = 3 * CONV_CH + LANE
BLK_SMALL = 3 * CONV_CH // LANE
COL_IW, COL_FD = 0, 4


def _cparams(sem):
    return pltpu.CompilerParams(dimension_semantics=sem, vmem_limit_bytes=VMEM_LIMIT)


def _tile(n, pref):
    t = min(n, pref)
    while n % t:
        t -= 1
    return t


def _mod_kernel(c_ref, w_ref, b_ref, o_ref):
    c = c_ref[...]
    ca = (c * jax.nn.sigmoid(c)).astype(jnp.bfloat16)
    w = w_ref[0].astype(jnp.bfloat16)
    o_ref[0] = jnp.dot(ca, w, preferred_element_type=jnp.float32) + b_ref[0]


def _modulation(c, w_ada, b_ada):
    L, D, N = w_ada.shape
    Bn = c.shape[0]
    tn = _tile(N, 1024)
    return pl.pallas_call(
        _mod_kernel,
        grid=(L, N // tn),
        in_specs=[
            pl.BlockSpec((Bn, D), lambda l, j: (0, 0)),
            pl.BlockSpec((1, D, tn), lambda l, j: (l, 0, j)),
            pl.BlockSpec((1, 1, tn), lambda l, j: (l, 0, j)),
        ],
        out_specs=pl.BlockSpec((1, Bn, tn), lambda l, j: (l, 0, j)),
        out_shape=jax.ShapeDtypeStruct((L, Bn, N), jnp.float32),
        name="ada_mod",
        compiler_params=_cparams(("parallel", "parallel")),
    )(c, w_ada, b_ada.reshape(L, 1, N))


def _norm_kernel(*refs, has_delta, closing):
    it = iter(refs)
    x_ref = next(it)
    d_ref = next(it) if has_delta else None
    gate_ref = next(it) if has_delta else None
    g_ref = next(it)
    sc_ref, sh_ref = (None, None) if closing else (next(it), next(it))
    xo_ref = next(it) if has_delta and not closing else None
    h_ref = next(it)

    x = x_ref[...]
    if has_delta:
        x = x + gate_ref[0] * d_ref[...]
    if xo_ref is not None:
        xo_ref[...] = x
    if closing:
        y = x * lax.rsqrt(jnp.mean(x * x, axis=-1, keepdims=True) + EPS) * g_ref[...]
    else:
        y = _rms_mod(x, g_ref[...], sc_ref[0], sh_ref[0])
    h_ref[...] = y.astype(h_ref.dtype)


def _norm(x, g, sc=None, sh=None, *, seq, delta=None, gate=None):
    T, D = x.shape
    tm = _tile(seq, 512)
    per_seq = seq // tm
    has_delta = delta is not None
    closing = sc is None
    row = pl.BlockSpec((tm, D), lambda i: (i, 0))
    per_b = pl.BlockSpec((1, 1, D), lambda i: (i // per_seq, 0, 0))
    args, specs = [x], [row]
    if has_delta:
        args += [delta, gate]
        specs += [row, per_b]
    args.append(g.reshape(1, D))
    specs.append(pl.BlockSpec((1, D), lambda i: (0, 0)))
    if not closing:
        args += [sc, sh]
        specs += [per_b, per_b]
    out_shape, out_specs = [], []
    if has_delta and not closing:
        out_shape.append(jax.ShapeDtypeStruct((T, D), jnp.float32))
        out_specs.append(row)
    out_shape.append(jax.ShapeDtypeStruct((T, D), jnp.float32 if closing else jnp.bfloat16))
    out_specs.append(row)
    res = pl.pallas_call(
        functools.partial(_norm_kernel, has_delta=has_delta, closing=closing),
        grid=(T // tm,),
        in_specs=specs,
        out_specs=out_specs,
        out_shape=out_shape,
        name="norm",
        compiler_params=_cparams(("parallel",)),
    )(*args)
    if closing:
        return res[0]
    if has_delta:
        return res[0], res[1]
    return None, res[0]


def _rms_mod(x, g, sc, sh):
    y = x * lax.rsqrt(jnp.mean(x * x, axis=-1, keepdims=True) + EPS) * g
    return y * (1.0 + sc) + sh


def _mm_norm_kernel(a_ref, w_ref, x_ref, gate_ref, g_ref, sc_ref, sh_ref, xo_ref, h_ref):
    acc = jnp.dot(a_ref[...], w_ref[...], preferred_element_type=jnp.float32)
    x = x_ref[...] + gate_ref[0] * acc
    xo_ref[...] = x
    h_ref[...] = _rms_mod(x, g_ref[...], sc_ref[0], sh_ref[0]).astype(h_ref.dtype)


def _mm_norm(a, w, l, x, gate, g, sc, sh, *, seq):
    T, D = x.shape
    K = a.shape[1]
    tm = _tile(seq, 512)
    per_seq = seq // tm
    row = pl.BlockSpec((tm, D), lambda i: (i, 0))
    per_b = pl.BlockSpec((1, 1, D), lambda i: (i // per_seq, 0, 0))
    return pl.pallas_call(
        _mm_norm_kernel,
        grid=(T // tm,),
        in_specs=[pl.BlockSpec((tm, K), lambda i: (i, 0)),
                  pl.BlockSpec((None, K, D), lambda i: (l, 0, 0)),
                  row, per_b, pl.BlockSpec((1, D), lambda i: (0, 0)), per_b, per_b],
        out_specs=[row, row],
        out_shape=[jax.ShapeDtypeStruct((T, D), jnp.float32),
                   jax.ShapeDtypeStruct((T, D), jnp.bfloat16)],
        name="outproj_norm",
        compiler_params=_cparams(("parallel",)),
    )(a, w, x, gate, g.reshape(1, D), sc, sh)


def _mm_kernel(a_ref, w_ref, o_ref):
    o_ref[...] = jnp.dot(a_ref[...], w_ref[...],
                         preferred_element_type=jnp.float32).astype(o_ref.dtype)


def _mm(a, w, l, out_dtype, tn_pref):
    M, K = a.shape
    N = w.shape[2]
    tm = _tile(M, 1024)
    tn = _tile(N, tn_pref)
    return pl.pallas_call(
        _mm_kernel,
        grid=(M // tm, N // tn),
        in_specs=[pl.BlockSpec((tm, K), lambda i, j: (i, 0)),
                  pl.BlockSpec((None, K, tn), lambda i, j: (l, 0, j))],
        out_specs=pl.BlockSpec((tm, tn), lambda i, j: (i, j)),
        out_shape=jax.ShapeDtypeStruct((M, N), out_dtype),
        name="matmul",
        compiler_params=_cparams(("parallel", "parallel")),
    )(a, w)


def _cumsum_kernel(s_ref, b_ref, o_ref):
    S = s_ref.shape[1]
    x = s_ref[0] + b_ref[...]
    ls = jnp.minimum(x, 0.0) - jnp.log(1.0 + jnp.exp(-jnp.abs(x)))
    r = lax.broadcasted_iota(jnp.int32, (LANE, LANE), 0)
    c = lax.broadcasted_iota(jnp.int32, (LANE, LANE), 1)
    tri = jnp.where(r >= c, 1.0, 0.0).astype(jnp.bfloat16)
    hi = ls.astype(jnp.bfloat16)
    r1 = ls - hi.astype(jnp.float32)
    mid = r1.astype(jnp.bfloat16)
    lo = (r1 - mid.astype(jnp.float32)).astype(jnp.bfloat16)
    carry = jnp.zeros((1, LANE), jnp.float32)
    for j in range(S // LANE):
        rows = slice(j * LANE, (j + 1) * LANE)
        blk = (jnp.dot(tri, hi[rows], preferred_element_type=jnp.float32)
               + jnp.dot(tri, mid[rows], preferred_element_type=jnp.float32)
               + jnp.dot(tri, lo[rows], preferred_element_type=jnp.float32)) + carry
        o_ref[0, j * LANE:(j + 1) * LANE, :] = blk
        carry = blk[LANE - 1:LANE, :]


def _forget_cumsum(small, b_pad):
    Bn, S, _ = small.shape
    return pl.pallas_call(
        _cumsum_kernel,
        grid=(Bn,),
        in_specs=[pl.BlockSpec((1, S, LANE), lambda b: (b, 0, BLK_SMALL)),
                  pl.BlockSpec((1, LANE), lambda b: (0, 0))],
        out_specs=pl.BlockSpec((1, S, LANE), lambda b: (b, 0, 0)),
        out_shape=jax.ShapeDtypeStruct((Bn, S, LANE), jnp.float32),
        name="forget_cumsum",
        compiler_params=_cparams(("parallel",)),
    )(small, b_pad)


N_KEY_BITS = 32


def _select_block(iq_ref, ik_ref, sm_ref, h_ref, wg_ref, bg_ref, mask_ref, z_ref, key_ref, thr_ref,
                  *, k_top, blk, n_chunks):
    c = pl.program_id(2)
    tq = iq_ref.shape[1]
    S = ik_ref.shape[1]
    W = (blk + 1) * tq
    kf = jnp.float32(k_top)
    causal = (lax.broadcasted_iota(jnp.int32, (tq, W), 1)
              <= blk * tq + lax.broadcasted_iota(jnp.int32, (tq, W), 0))

    @pl.when(c == 0)
    def _():
        iq = iq_ref[0]
        ik = ik_ref[0, :W, :IDX_DIM]
        iw = sm_ref[0][:, COL_IW:COL_IW + IDX_HEADS] * IDX_SCALE
        score = jnp.zeros((tq, W), jnp.float32)
        for h in range(IDX_HEADS):
            d = lax.dot_general(iq[:, h * IDX_DIM:(h + 1) * IDX_DIM], ik,
                                (((1,), (1,)), ((), ())), preferred_element_type=jnp.float32)
            score = score + jnp.where(d > 0.0, d * iw[:, h:h + 1], 0.0)
        score = jnp.where(causal, score, NEG)
        bits = lax.bitcast_convert_type(score, jnp.int32)
        key_ref[:, :W] = bits ^ ((bits >> 31) & 0x7FFFFFFF)
        thr_ref[...] = jnp.full(thr_ref.shape, INT_MIN, jnp.int32)

    per_step = N_KEY_BITS // n_chunks
    tz = z_ref.shape[1] // per_step
    thr = thr_ref[...]
    for j in range(per_step):
        zs = slice(j * tz, (j + 1) * tz)
        z = jnp.dot(h_ref[...], wg_ref[:, zs], preferred_element_type=jnp.float32) + bg_ref[:, zs]
        p = c * per_step + j
        cand = jnp.where(p == 0, 0, thr | jnp.left_shift(jnp.int32(1), N_KEY_BITS - 1 - p))
        cnt = jnp.sum(jnp.where(key_ref[:, :W] >= cand, 1.0, 0.0), axis=1, keepdims=True)
        thr = jnp.where(cnt >= kf, cand, thr)
        z_ref[:, zs] = z.astype(z_ref.dtype)
    thr_ref[...] = thr

    @pl.when(c == n_chunks - 1)
    def _():
        _select_write(mask_ref, key_ref, thr_ref[...], causal, kf, blk)


def _select_write(mask_ref, key_ref, thr, causal, kf, blk):
    tq, S = mask_ref.shape[1], mask_ref.shape[2]
    W = (blk + 1) * tq
    key = key_ref[:, :W]
    above = key > thr
    need = kf - jnp.sum(jnp.where(above, 1.0, 0.0), axis=1, keepdims=True)
    upper = jnp.where(lax.broadcasted_iota(jnp.int32, (tq, tq), 0)
                      <= lax.broadcasted_iota(jnp.int32, (tq, tq), 1), 1.0, 0.0).astype(jnp.bfloat16)
    seen = jnp.zeros((tq, 1), jnp.float32)
    for c in range(blk + 1):
        cs = slice(c * tq, (c + 1) * tq)
        tie = key[:, cs] == thr
        incl = seen + jnp.dot(jnp.where(tie, 1.0, 0.0).astype(jnp.bfloat16), upper,
                              preferred_element_type=jnp.float32)
        sel = above[:, cs] | (tie & (incl <= need))
        if c == blk:
            sel = sel & causal[:, cs]
        mask_ref[0, :, cs] = jnp.where(sel, 1.0, 0.0).astype(mask_ref.dtype)
        seen = incl[:, tq - 1:tq]
    if W < S:
        mask_ref[0, :, W:] = jnp.zeros((tq, S - W), mask_ref.dtype)


def _select_kernel(*refs, k_top, n_chunks):
    iq_ref, ik_ref = refs[0], refs[1]
    i = pl.program_id(1)
    for blk in range(ik_ref.shape[1] // iq_ref.shape[1]):
        @pl.when(i == blk)
        def _(blk=blk):
            _select_block(*refs, k_top=k_top, blk=blk, n_chunks=n_chunks)


def _select_and_gates(main, f32grp, h, w_gate, b_gate, l, k_top, tq):
    Bn, S, _ = main.shape
    T, D = h.shape
    N = w_gate.shape[2]
    nq = S // tq
    tc = _tile(N, 2048)
    n_chunks = N // tc
    assert N_KEY_BITS % n_chunks == 0
    return pl.pallas_call(
        functools.partial(_select_kernel, k_top=k_top, n_chunks=n_chunks),
        grid=(Bn, nq, n_chunks),
        in_specs=[pl.BlockSpec((1, tq, 2 * LANE), lambda b, i, c: (b, i, BLK_IQ // 2)),
                  pl.BlockSpec((1, S, LANE), lambda b, i, c: (b, 0, BLK_IK)),
                  pl.BlockSpec((1, tq, LANE), lambda b, i, c: (b, i, BLK_SMALL)),
                  pl.BlockSpec((tq, D), lambda b, i, c: (b * nq + i, 0)),
                  pl.BlockSpec((None, D, tc), lambda b, i, c: (l, 0, c)),
                  pl.BlockSpec((None, 1, tc), lambda b, i, c: (l, 0, c))],
        out_specs=[pl.BlockSpec((1, tq, S), lambda b, i, c: (b, i, 0)),
                   pl.BlockSpec((tq, tc), lambda b, i, c: (b * nq + i, c))],
        out_shape=[jax.ShapeDtypeStruct((Bn, S, S), jnp.bfloat16),
                   jax.ShapeDtypeStruct((T, N), jnp.bfloat16)],
        scratch_shapes=[pltpu.VMEM((tq, S), jnp.int32), pltpu.VMEM((tq, 1), jnp.int32)],
        name="dsa_select_gates",
        compiler_params=_cparams(("parallel", "parallel", "arbitrary")),
    )(main, main, f32grp, h, w_gate, b_gate)


def _t5_thresholds():
    max_exact = N_BUCKETS // 2
    rel = np.arange(MAX_DIST)
    nf = np.maximum(rel, 1).astype(np.float64)
    large = max_exact + (np.log(nf / max_exact) / math.log(MAX_DIST / max_exact)
                         * (N_BUCKETS - max_exact)).astype(np.int64)
    bucket = np.where(rel < max_exact, rel, np.minimum(large, N_BUCKETS - 1))
    out = []
    for b in range(1, N_BUCKETS):
        hit = np.nonzero(bucket >= b)[0]
        out.append(int(hit[0]) if hit.size else MAX_DIST)
    return out


def _attn_kernel(*refs, mode, head_off, lam_init):
    it = iter(refs)
    q_ref, k_ref, v_ref = next(it), next(it), next(it)
    if mode == "dsa":
        mask_ref = next(it)
    if mode in ("dsa", "diff"):
        tab_ref = next(it)
    if mode == "diff":
        lam_ref, gs_ref = next(it), next(it)
    if mode == "fox":
        fc_ref, fr_ref = next(it), next(it)
    o_ref = next(it)
    m_ref, acc_ref = next(it), next(it)
    if mode in ("dsa", "diff"):
        band_ref = next(it)

    i = pl.program_id(1)
    tq = q_ref.shape[1]
    nmap = 2 if mode == "diff" else 1
    dq = HEAD_DIM // nmap
    scale = dq ** -0.5
    biased = mode in ("dsa", "diff")

    c1 = scale * LOG2E
    if biased:
        @pl.when(i == 0)
        def _():
            r = lax.broadcasted_iota(jnp.int32, (tq, 2 * tq), 0)
            c = lax.broadcasted_iota(jnp.int32, (tq, 2 * tq), 1)
            rel = r - c + tq
            vals = [jnp.full((tq, 2 * tq), tab_ref[0, head_off + h] * LOG2E, jnp.float32)
                    for h in range(N_HEADS)]
            for b, first in enumerate(_t5_thresholds(), start=1):
                hit = rel >= first
                vals = [jnp.where(hit, tab_ref[b, head_off + h] * LOG2E, vals[h])
                        for h in range(N_HEADS)]
            for h in range(N_HEADS):
                band_ref[h] = vals[h]

    m_ref[...] = jnp.full(m_ref.shape, NEG, jnp.float32)
    acc_ref[...] = jnp.zeros(acc_ref.shape, jnp.float32)

    def chunk(off, kind, tk):
        ones = jnp.ones((tk, LANE), jnp.bfloat16)
        if mode == "dsa":
            keep = mask_ref[0, :, pl.ds(off, tk)].astype(jnp.float32) > 0.5
        elif kind == "far":
            keep = None
        else:
            keep = (lax.broadcasted_iota(jnp.int32, (tq, tk), 1) - (tk - tq)
                    <= lax.broadcasted_iota(jnp.int32, (tq, tk), 0))
        units = [(h, mp) for h in range(N_HEADS) for mp in range(nmap)]
        logits = []
        for h, mp in units:
            cols = slice(h * HEAD_DIM + mp * dq, h * HEAD_DIM + (mp + 1) * dq)
            logits.append(lax.dot_general(q_ref[0, :, cols], k_ref[0, pl.ds(off, tk), cols],
                                          (((1,), (1,)), ((), ())),
                                          preferred_element_type=jnp.float32))
        m_olds, m_news = [], []
        for u, (h, mp) in enumerate(units):
            t = logits[u] * c1
            if biased:
                if kind == "far":
                    t = t + tab_ref[N_BUCKETS - 1, head_off + h] * LOG2E
                elif kind == "near":
                    t = t + band_ref[h]
                else:
                    t = t + band_ref[h, :, tq:]
            else:
                t = (t + fc_ref[0, :, COL_FD + h:COL_FD + h + 1] * LOG2E
                     - fr_ref[0, h:h + 1, pl.ds(off, tk)] * LOG2E)
            if keep is not None:
                t = jnp.where(keep, t, NEG)
            logits[u] = t
            m_old = m_ref[u]
            m_olds.append(m_old)
            m_news.append(jnp.maximum(m_old, jnp.max(t, axis=1, keepdims=True)))
        probs = []
        for u in range(len(units)):
            probs.append(jnp.exp2(logits[u] - jnp.tile(m_news[u], (1, tk // LANE))).astype(v_ref.dtype))
            m_ref[u] = m_news[u]
        for u, (h, mp) in enumerate(units):
            vc = jnp.concatenate(
                [v_ref[0, pl.ds(off, tk), h * HEAD_DIM:(h + 1) * HEAD_DIM], ones], axis=1)
            alpha = jnp.exp2(m_olds[u] - m_news[u])
            acc_ref[u] = (jnp.tile(alpha, (1, 2)) * acc_ref[u]
                          + jnp.dot(probs[u], vc, preferred_element_type=jnp.float32))

    n_far = jnp.maximum(i - 1, 0)

    def far_body(c, carry):
        chunk(pl.multiple_of(c * 2 * tq, 2 * tq), "far", 2 * tq)
        return carry

    lax.fori_loop(0, n_far // 2, far_body, 0)

    @pl.when(n_far % 2 == 1)
    def _():
        chunk(pl.multiple_of((n_far - 1) * tq, tq), "far", tq)

    @pl.when(i >= 1)
    def _():
        chunk(pl.multiple_of((i - 1) * tq, tq), "near", 2 * tq)

    @pl.when(i == 0)
    def _():
        chunk(0, "diag", tq)

    if mode == "diff":
        lq = lam_ref[...]
        lam_val = (jnp.exp(jnp.sum(lq[0:1] * lq[1:2], axis=1, keepdims=True))
                   - jnp.exp(jnp.sum(lq[2:3] * lq[3:4], axis=1, keepdims=True)) + lam_init)
    for h in range(N_HEADS):
        a = acc_ref[h * nmap]
        o = a[:, :HEAD_DIM] / a[:, HEAD_DIM:]
        if mode == "diff":
            a1 = acc_ref[h * nmap + 1]
            o = o - lam_val * (a1[:, :HEAD_DIM] / a1[:, HEAD_DIM:])
            o = o * lax.rsqrt(jnp.mean(o * o, axis=-1, keepdims=True) + EPS) * gs_ref[...]
            o = o * (1.0 - lam_init)
        o_ref[0, :, h * HEAD_DIM:(h + 1) * HEAD_DIM] = o.astype(o_ref.dtype)


def _attention(main, mode, tq, *, mask=None, tab=None, lam=None, g_subln=None,
               fcol=None, frow=None, lam_init=0.0):
    Bn, S, _ = main.shape
    blk_q, blk_k, blk_v, head_off = {
        "dsa": (BLK_QA, BLK_KA, BLK_VA, 0),
        "diff": (BLK_QB, BLK_KB, BLK_VB, N_HEADS),
        "fox": (BLK_QD, BLK_KD, BLK_VD, 0),
    }[mode]
    nmap = 2 if mode == "diff" else 1
    W = N_HEADS * HEAD_DIM
    args = [main, main, main]
    specs = [pl.BlockSpec((1, tq, W), lambda b, i: (b, i, blk_q // N_HEADS)),
             pl.BlockSpec((1, S, W), lambda b, i: (b, 0, blk_k // N_HEADS)),
             pl.BlockSpec((1, S, W), lambda b, i: (b, 0, blk_v // N_HEADS))]
    scratch = [pltpu.VMEM((N_HEADS * nmap, tq, LANE), jnp.float32),
               pltpu.VMEM((N_HEADS * nmap, tq, 2 * HEAD_DIM), jnp.float32)]
    if mode == "dsa":
        args.append(mask)
        specs.append(pl.BlockSpec((1, tq, S), lambda b, i: (b, i, 0)))
    if mode in ("dsa", "diff"):
        args.append(tab)
        specs.append(pl.BlockSpec(memory_space=pltpu.SMEM))
        scratch.append(pltpu.VMEM((N_HEADS, tq, 2 * tq), jnp.float32))
    if mode == "diff":
        args += [lam, g_subln.reshape(1, HEAD_DIM)]
        specs += [pl.BlockSpec((4, DIFF_DIM), lambda b, i: (0, 0)),
                  pl.BlockSpec((1, HEAD_DIM), lambda b, i: (0, 0))]
    if mode == "fox":
        args += [fcol, frow]
        specs += [pl.BlockSpec((1, tq, LANE), lambda b, i: (b, i, 0)),
                  pl.BlockSpec((1, N_HEADS, S), lambda b, i: (b, 0, 0))]
    return pl.pallas_call(
        functools.partial(_attn_kernel, mode=mode, head_off=head_off, lam_init=lam_init),
        grid=(Bn, S // tq),
        in_specs=specs,
        out_specs=pl.BlockSpec((1, tq, W), lambda b, i: (b, i, 0)),
        out_shape=jax.ShapeDtypeStruct((Bn, S, MIX_W), jnp.bfloat16),
        scratch_shapes=scratch,
        name="attn_" + mode,
        compiler_params=_cparams(("parallel", "arbitrary")),
    )(*args)


def _conv_kernel(bg_ref, cg_ref, xin_ref, w_ref, o_ref):
    u = cg_ref[0] * xin_ref[0]
    row = lax.broadcasted_iota(jnp.int32, u.shape, 0)
    w = w_ref[...]
    y = u * w[CONV_K - 1:CONV_K, :]
    for d in range(1, CONV_K):
        shifted = jnp.where(row >= d, pltpu.roll(u, d, 0), 0.0)
        y = y + shifted * w[CONV_K - 1 - d:CONV_K - d, :]
    o_ref[0] = (bg_ref[0] * y).astype(o_ref.dtype)


def _short_conv(f32grp, conv_w):
    Bn, S, _ = f32grp.shape
    tc = LANE
    per = CONV_CH // tc
    return pl.pallas_call(
        _conv_kernel,
        grid=(Bn, per),
        in_specs=[pl.BlockSpec((1, S, tc), lambda b, j: (b, 0, j)),
                  pl.BlockSpec((1, S, tc), lambda b, j: (b, 0, per + j)),
                  pl.BlockSpec((1, S, tc), lambda b, j: (b, 0, 2 * per + j)),
                  pl.BlockSpec((CONV_K, tc), lambda b, j: (0, j))],
        out_specs=pl.BlockSpec((1, S, tc), lambda b, j: (b, 0, j)),
        out_shape=jax.ShapeDtypeStruct((Bn, S, CONV_CH), jnp.bfloat16),
        name="short_conv",
        compiler_params=_cparams(("parallel", "parallel")),
    )(f32grp, f32grp, f32grp, conv_w)


def _mix_kernel(*refs):
    z_refs = refs[0:N_BRANCH]
    y_refs = refs[N_BRANCH:2 * N_BRANCH]
    wb_refs = refs[2 * N_BRANCH:3 * N_BRANCH]
    o_ref = refs[3 * N_BRANCH]
    merged = None
    for g in range(N_BRANCH):
        gate = jax.nn.sigmoid(z_refs[g][...].astype(jnp.float32))
        proj = jnp.dot(y_refs[g][...], wb_refs[g][...], preferred_element_type=jnp.float32)
        merged = gate * proj if merged is None else merged + gate * proj
    o_ref[...] = merged.astype(o_ref.dtype)


def _mix(z, ys, w_branch, l):
    T = z.shape[0]
    D = w_branch.shape[3]
    tm = _tile(T, 1024)
    tn = _tile(D, 512)
    per = D // tn
    specs = [pl.BlockSpec((tm, tn), lambda i, n, g=g: (i, g * per + n)) for g in range(N_BRANCH)]
    specs += [pl.BlockSpec((tm, MIX_W), lambda i, n: (i, 0)) for _ in range(N_BRANCH)]
    specs += [pl.BlockSpec((None, None, MIX_W, tn), lambda i, n, g=g: (l, g, 0, n))
              for g in range(N_BRANCH)]
    return pl.pallas_call(
        _mix_kernel,
        grid=(T // tm, per),
        in_specs=specs,
        out_specs=pl.BlockSpec((tm, tn), lambda i, n: (i, n)),
        out_shape=jax.ShapeDtypeStruct((T, D), jnp.bfloat16),
        name="gated_mix",
        compiler_params=_cparams(("parallel", "parallel")),
    )(*([z] * N_BRANCH), *ys, *([w_branch] * N_BRANCH))


def _ffn_kernel(h_ref, w1_ref, w2_ref, o_ref, u_ref, *, nf):
    f = pl.program_id(1)

    def first(slot):
        u = jnp.dot(h_ref[...], w1_ref[...], preferred_element_type=jnp.float32)
        u_ref[slot] = jnp.square(jnp.maximum(u, 0.0)).astype(u_ref.dtype)

    def second(slot):
        return jnp.dot(u_ref[slot], w2_ref[...], preferred_element_type=jnp.float32)

    @pl.when(f == 0)
    def _():
        o_ref[...] = jnp.zeros(o_ref.shape, o_ref.dtype)
        first(0)

    for parity in range(2):
        @pl.when((f > 0) & (f < nf) & (f % 2 == parity))
        def _(parity=parity):
            part = second(1 - parity)
            first(parity)
            o_ref[...] += part

    @pl.when(f == nf)
    def _():
        o_ref[...] += second((nf - 1) % 2)


def _ffn(h, w1, w2, l):
    T, D = h.shape
    F = w1.shape[2]
    tm = _tile(T, 1024)
    tf = _tile(F, 1024)
    nf = F // tf
    row = pl.BlockSpec((tm, D), lambda i, f: (i, 0))
    return pl.pallas_call(
        functools.partial(_ffn_kernel, nf=nf),
        grid=(T // tm, nf + 1),
        in_specs=[row,
                  pl.BlockSpec((None, D, tf), lambda i, f: (l, 0, jnp.minimum(f, nf - 1))),
                  pl.BlockSpec((None, tf, D), lambda i, f: (l, jnp.maximum(f - 1, 0), 0))],
        out_specs=row,
        out_shape=jax.ShapeDtypeStruct((T, D), jnp.float32),
        scratch_shapes=[pltpu.VMEM((2, tm, tf), jnp.bfloat16)],
        name="ffn",
        compiler_params=_cparams(("parallel", "arbitrary")),
    )(h, w1, w2)


def _split_in_proj(w_in):
    W = N_HEADS * HEAD_DIM
    sizes = (3 * W,
             IDX_HEADS * IDX_DIM + IDX_DIM,
             IDX_HEADS,
             3 * W,
             3 * CONV_CH,
             3 * W,
             N_HEADS)
    offs = np.concatenate([[0], np.cumsum(sizes)])
    a, idx, iw, b, cv, d, fd = [w_in[:, :, offs[j]:offs[j + 1]] for j in range(7)]
    L, D, _ = w_in.shape
    zeros = lambda n: jnp.zeros((L, D, n), w_in.dtype)
    main = jnp.concatenate([a, b, d, idx, zeros(N_MAIN - 9 * W - sizes[1])], axis=-1)
    f32grp = jnp.concatenate([cv, iw, fd, zeros(LANE - IDX_HEADS - N_HEADS)], axis=-1)
    return main.astype(jnp.bfloat16), f32grp.astype(jnp.bfloat16)


def kernel(x, c, w_ada, b_ada, g_norm1, w_in, b_f, conv_w, lam, g_subln, rel_bias,
           w_gate, b_gate, w_branch, w_o, g_norm2, w_ff1, w_ff2, g_final):
    Bn, S, D = x.shape
    L = w_ada.shape[0]
    T = Bn * S
    k_top = min(TOPK_MAX, S // 4)
    tq = _tile(S, 256)
    bf = jnp.bfloat16

    w_main, w_f32 = _split_in_proj(w_in)
    w_gate_b, w_branch_b, w_o_b = w_gate.astype(bf), w_branch.astype(bf), w_o.astype(bf)
    w_ff1_b, w_ff2_b = w_ff1.astype(bf), w_ff2.astype(bf)

    mod = _modulation(c, w_ada, b_ada).reshape(L, Bn, 6, 1, D)
    xt = x.reshape(T, D)
    delta, gate = None, None
    for l in range(L):
        sh1, sc1, gt1, sh2, sc2, gt2 = [mod[l, :, j] for j in range(6)]
        x_new, h = _norm(xt, g_norm1[l], sc1, sh1, seq=S, delta=delta, gate=gate)
        xt = xt if x_new is None else x_new
        main = _mm(h, w_main, l, bf, 1024).reshape(Bn, S, N_MAIN)
        f32grp = _mm(h, w_f32, l, jnp.float32, N_F32).reshape(Bn, S, N_F32)

        b_pad = jnp.zeros((1, LANE), jnp.float32).at[0, COL_FD:COL_FD + N_HEADS].set(b_f[l])
        fcol = _forget_cumsum(f32grp, b_pad)
        frow = jnp.transpose(fcol[:, :, COL_FD:COL_FD + N_HEADS], (0, 2, 1))

        mask, z = _select_and_gates(main, f32grp, h, w_gate_b, b_gate.reshape(L, 1, N_BRANCH * D),
                                    l, k_top, _tile(S, 512))
        lam_init = 0.8 - 0.6 * math.exp(-0.3 * l)
        y_a = _attention(main, "dsa", tq, mask=mask, tab=rel_bias)
        y_b = _attention(main, "diff", tq, tab=rel_bias, lam=lam[l], g_subln=g_subln[l],
                         lam_init=lam_init)
        y_c = _short_conv(f32grp, conv_w[l])
        y_d = _attention(main, "fox", tq, fcol=fcol, frow=frow)
        ys = [y.reshape(T, MIX_W) for y in (y_a, y_b, y_c, y_d)]

        merged = _mix(z, ys, w_branch_b, l)
        xt, h2 = _mm_norm(merged, w_o_b, l, xt, gt1, g_norm2[l], sc2, sh2, seq=S)
        delta = _ffn(h2, w_ff1_b, w_ff2_b, l)
        gate = gt2
    out = _norm(xt, g_final, seq=S, delta=delta, gate=gate)
    return out.reshape(Bn, S, D)
```

```python
import functools
import math

import numpy as np
import jax
import jax.numpy as jnp
from jax import lax
from jax.experimental import pallas as pl
from jax.experimental.pallas import tpu as pltpu

HEAD_DIM = 128
N_HEADS = 4
DIFF_DIM = HEAD_DIM // 2
CONV_CH = 512
CONV_K = 3
IDX_HEADS = 4
IDX_DIM = 64
TOPK_MAX = 256
MIX_W = 512
N_BRANCH = 4
N_BUCKETS = 32
MAX_DIST = 128
EPS = 1e-6
NEG = -1e30
IDX_SCALE = (IDX_DIM ** -0.5) * (IDX_HEADS ** -0.5)
INT_MIN = -(2 ** 31)
LOG2E = 1.0 / math.log(2.0)

LANE = 128
VMEM_LIMIT = 56 * 1024 * 1024

BLK_QA, BLK_KA, BLK_VA = 0, 4, 8
BLK_QB, BLK_KB, BLK_VB = 12, 16, 20
BLK_QD, BLK_KD, BLK_VD = 24, 28, 32
BLK_IQ = 36
BLK_IK = 38
N_MAIN = 40 * LANE
N_F32 = 3 * CONV_CH + LANE
BLK_SMALL = 3 * CONV_CH // LANE
COL_IW, COL_FD = 0, 4


def _cparams(sem):
    return pltpu.CompilerParams(dimension_semantics=sem, vmem_limit_bytes=VMEM_LIMIT)


def _tile(n, pref):
    t = min(n, pref)
    while n % t:
        t -= 1
    return t


def _mod_kernel(c_ref, w_ref, b_ref, o_ref):
    c = c_ref[...]
    ca = (c * jax.nn.sigmoid(c)).astype(jnp.bfloat16)
    w = w_ref[0].astype(jnp.bfloat16)
    o_ref[0] = jnp.dot(ca, w, preferred_element_type=jnp.float32) + b_ref[0]


def _modulation(c, w_ada, b_ada):
    L, D, N = w_ada.shape
    Bn = c.shape[0]
    tn = _tile(N, 1024)
    return pl.pallas_call(
        _mod_kernel,
        grid=(L, N // tn),
        in_specs=[
            pl.BlockSpec((Bn, D), lambda l, j: (0, 0)),
            pl.BlockSpec((1, D, tn), lambda l, j: (l, 0, j)),
            pl.BlockSpec((1, 1, tn), lambda l, j: (l, 0, j)),
        ],
        out_specs=pl.BlockSpec((1, Bn, tn), lambda l, j: (l, 0, j)),
        out_shape=jax.ShapeDtypeStruct((L, Bn, N), jnp.float32),
        name="ada_mod",
        compiler_params=_cparams(("parallel", "parallel")),
    )(c, w_ada, b_ada.reshape(L, 1, N))


def _norm_kernel(*refs, has_delta, closing):
    it = iter(refs)
    x_ref = next(it)
    d_ref = next(it) if has_delta else None
    gate_ref = next(it) if has_delta else None
    g_ref = next(it)
    sc_ref, sh_ref = (None, None) if closing else (next(it), next(it))
    xo_ref = next(it) if has_delta and not closing else None
    h_ref = next(it)

    x = x_ref[...]
    if has_delta:
        x = x + gate_ref[0] * d_ref[...]
    if xo_ref is not None:
        xo_ref[...] = x
    if closing:
        y = x * lax.rsqrt(jnp.mean(x * x, axis=-1, keepdims=True) + EPS) * g_ref[...]
    else:
        y = _rms_mod(x, g_ref[...], sc_ref[0], sh_ref[0])
    h_ref[...] = y.astype(h_ref.dtype)


def _norm(x, g, sc=None, sh=None, *, seq, delta=None, gate=None):
    T, D = x.shape
    tm = _tile(seq, 512)
    per_seq = seq // tm
    has_delta = delta is not None
    closing = sc is None
    row = pl.BlockSpec((tm, D), lambda i: (i, 0))
    per_b = pl.BlockSpec((1, 1, D), lambda i: (i // per_seq, 0, 0))
    args, specs = [x], [row]
    if has_delta:
        args += [delta, gate]
        specs += [row, per_b]
    args.append(g.reshape(1, D))
    specs.append(pl.BlockSpec((1, D), lambda i: (0, 0)))
    if not closing:
        args += [sc, sh]
        specs += [per_b, per_b]
    out_shape, out_specs = [], []
    if has_delta and not closing:
        out_shape.append(jax.ShapeDtypeStruct((T, D), jnp.float32))
        out_specs.append(row)
    out_shape.append(jax.ShapeDtypeStruct((T, D), jnp.float32 if closing else jnp.bfloat16))
    out_specs.append(row)
    res = pl.pallas_call(
        functools.partial(_norm_kernel, has_delta=has_delta, closing=closing),
        grid=(T // tm,),
        in_specs=specs,
        out_specs=out_specs,
        out_shape=out_shape,
        name="norm",
        compiler_params=_cparams(("parallel",)),
    )(*args)
    if closing:
        return res[0]
    if has_delta:
        return res[0], res[1]
    return None, res[0]


def _rms_mod(x, g, sc, sh):
    y = x * lax.rsqrt(jnp.mean(x * x, axis=-1, keepdims=True) + EPS) * g
    return y * (1.0 + sc) + sh


def _mm_norm_kernel(a_ref, w_ref, x_ref, gate_ref, g_ref, sc_ref, sh_ref, xo_ref, h_ref):
    acc = jnp.dot(a_ref[...], w_ref[...], preferred_element_type=jnp.float32)
    x = x_ref[...] + gate_ref[0] * acc
    xo_ref[...] = x
    h_ref[...] = _rms_mod(x, g_ref[...], sc_ref[0], sh_ref[0]).astype(h_ref.dtype)


def _mm_norm(a, w, l, x, gate, g, sc, sh, *, seq):
    T, D = x.shape
    K = a.shape[1]
    tm = _tile(seq, 512)
    per_seq = seq // tm
    row = pl.BlockSpec((tm, D), lambda i: (i, 0))
    per_b = pl.BlockSpec((1, 1, D), lambda i: (i // per_seq, 0, 0))
    return pl.pallas_call(
        _mm_norm_kernel,
        grid=(T // tm,),
        in_specs=[pl.BlockSpec((tm, K), lambda i: (i, 0)),
                  pl.BlockSpec((None, K, D), lambda i: (l, 0, 0)),
                  row, per_b, pl.BlockSpec((1, D), lambda i: (0, 0)), per_b, per_b],
        out_specs=[row, row],
        out_shape=[jax.ShapeDtypeStruct((T, D), jnp.float32),
                   jax.ShapeDtypeStruct((T, D), jnp.bfloat16)],
        name="outproj_norm",
        compiler_params=_cparams(("parallel",)),
    )(a, w, x, gate, g.reshape(1, D), sc, sh)


def _mm_kernel(a_ref, w_ref, o_ref):
    o_ref[...] = jnp.dot(a_ref[...], w_ref[...],
                         preferred_element_type=jnp.float32).astype(o_ref.dtype)


def _mm(a, w, l, out_dtype, tn_pref):
    M, K = a.shape
    N = w.shape[2]
    tm = _tile(M, 1024)
    tn = _tile(N, tn_pref)
    return pl.pallas_call(
        _mm_kernel,
        grid=(M // tm, N // tn),
        in_specs=[pl.BlockSpec((tm, K), lambda i, j: (i, 0)),
                  pl.BlockSpec((None, K, tn), lambda i, j: (l, 0, j))],
        out_specs=pl.BlockSpec((tm, tn), lambda i, j: (i, j)),
        out_shape=jax.ShapeDtypeStruct((M, N), out_dtype),
        name="matmul",
        compiler_params=_cparams(("parallel", "parallel")),
    )(a, w)


def _cumsum_kernel(s_ref, b_ref, o_ref):
    S = s_ref.shape[1]
    x = s_ref[0] + b_ref[...]
    ls = jnp.minimum(x, 0.0) - jnp.log(1.0 + jnp.exp(-jnp.abs(x)))
    r = lax.broadcasted_iota(jnp.int32, (LANE, LANE), 0)
    c = lax.broadcasted_iota(jnp.int32, (LANE, LANE), 1)
    tri = jnp.where(r >= c, 1.0, 0.0).astype(jnp.bfloat16)
    hi = ls.astype(jnp.bfloat16)
    r1 = ls - hi.astype(jnp.float32)
    mid = r1.astype(jnp.bfloat16)
    lo = (r1 - mid.astype(jnp.float32)).astype(jnp.bfloat16)
    carry = jnp.zeros((1, LANE), jnp.float32)
    for j in range(S // LANE):
        rows = slice(j * LANE, (j + 1) * LANE)
        blk = (jnp.dot(tri, hi[rows], preferred_element_type=jnp.float32)
               + jnp.dot(tri, mid[rows], preferred_element_type=jnp.float32)
               + jnp.dot(tri, lo[rows], preferred_element_type=jnp.float32)) + carry
        o_ref[0, j * LANE:(j + 1) * LANE, :] = blk
        carry = blk[LANE - 1:LANE, :]


def _forget_cumsum(small, b_pad):
    Bn, S, _ = small.shape
    return pl.pallas_call(
        _cumsum_kernel,
        grid=(Bn,),
        in_specs=[pl.BlockSpec((1, S, LANE), lambda b: (b, 0, BLK_SMALL)),
                  pl.BlockSpec((1, LANE), lambda b: (0, 0))],
        out_specs=pl.BlockSpec((1, S, LANE), lambda b: (b, 0, 0)),
        out_shape=jax.ShapeDtypeStruct((Bn, S, LANE), jnp.float32),
        name="forget_cumsum",
        compiler_params=_cparams(("parallel",)),
    )(small, b_pad)


N_KEY_BITS = 32


def _select_block(iq_ref, ik_ref, sm_ref, h_ref, wg_ref, bg_ref, mask_ref, z_ref, key_ref, thr_ref,
                  *, k_top, blk, n_chunks):
    c = pl.program_id(2)
    tq = iq_ref.shape[1]
    S = ik_ref.shape[1]
    W = (blk + 1) * tq
    kf = jnp.float32(k_top)
    causal = (lax.broadcasted_iota(jnp.int32, (tq, W), 1)
              <= blk * tq + lax.broadcasted_iota(jnp.int32, (tq, W), 0))

    @pl.when(c == 0)
    def _():
        iq = iq_ref[0]
        ik = ik_ref[0, :W, :IDX_DIM]
        iw = sm_ref[0][:, COL_IW:COL_IW + IDX_HEADS] * IDX_SCALE
        score = jnp.zeros((tq, W), jnp.float32)
        for h in range(IDX_HEADS):
            d = lax.dot_general(iq[:, h * IDX_DIM:(h + 1) * IDX_DIM], ik,
                                (((1,), (1,)), ((), ())), preferred_element_type=jnp.float32)
            score = score + jnp.where(d > 0.0, d * iw[:, h:h + 1], 0.0)
        score = jnp.where(causal, score, NEG)
        bits = lax.bitcast_convert_type(score, jnp.int32)
        key_ref[:, :W] = bits ^ ((bits >> 31) & 0x7FFFFFFF)
        thr_ref[...] = jnp.full(thr_ref.shape, INT_MIN, jnp.int32)

    per_step = N_KEY_BITS // n_chunks
    tz = z_ref.shape[1] // per_step
    thr = thr_ref[...]
    for j in range(per_step):
        zs = slice(j * tz, (j + 1) * tz)
        z = jnp.dot(h_ref[...], wg_ref[:, zs], preferred_element_type=jnp.float32) + bg_ref[:, zs]
        p = c * per_step + j
        cand = jnp.where(p == 0, 0, thr | jnp.left_shift(jnp.int32(1), N_KEY_BITS - 1 - p))
        cnt = jnp.sum(jnp.where(key_ref[:, :W] >= cand, 1.0, 0.0), axis=1, keepdims=True)
        thr = jnp.where(cnt >= kf, cand, thr)
        z_ref[:, zs] = z.astype(z_ref.dtype)
    thr_ref[...] = thr

    @pl.when(c == n_chunks - 1)
    def _():
        _select_write(mask_ref, key_ref, thr_ref[...], causal, kf, blk)


def _select_write(mask_ref, key_ref, thr, causal, kf, blk):
    tq, S = mask_ref.shape[1], mask_ref.shape[2]
    W = (blk + 1) * tq
    key = key_ref[:, :W]
    above = key > thr
    need = kf - jnp.sum(jnp.where(above, 1.0, 0.0), axis=1, keepdims=True)
    upper = jnp.where(lax.broadcasted_iota(jnp.int32, (tq, tq), 0)
                      <= lax.broadcasted_iota(jnp.int32, (tq, tq), 1), 1.0, 0.0).astype(jnp.bfloat16)
    seen = jnp.zeros((tq, 1), jnp.float32)
    for c in range(blk + 1):
        cs = slice(c * tq, (c + 1) * tq)
        tie = key[:, cs] == thr
        incl = seen + jnp.dot(jnp.where(tie, 1.0, 0.0).astype(jnp.bfloat16), upper,
                              preferred_element_type=jnp.float32)
        sel = above[:, cs] | (tie & (incl <= need))
        if c == blk:
            sel = sel & causal[:, cs]
        mask_ref[0, :, cs] = jnp.where(sel, 1.0, 0.0).astype(mask_ref.dtype)
        seen = incl[:, tq - 1:tq]
    if W < S:
        mask_ref[0, :, W:] = jnp.zeros((tq, S - W), mask_ref.dtype)


def _select_kernel(*refs, k_top, n_chunks):
    iq_ref, ik_ref = refs[0], refs[1]
    i = pl.program_id(1)
    for blk in range(ik_ref.shape[1] // iq_ref.shape[1]):
        @pl.when(i == blk)
        def _(blk=blk):
            _select_block(*refs, k_top=k_top, blk=blk, n_chunks=n_chunks)


def _select_and_gates(main, f32grp, h, w_gate, b_gate, l, k_top, tq):
    Bn, S, _ = main.shape
    T, D = h.shape
    N = w_gate.shape[2]
    nq = S // tq
    tc = _tile(N, 2048)
    n_chunks = N // tc
    assert N_KEY_BITS % n_chunks == 0
    return pl.pallas_call(
        functools.partial(_select_kernel, k_top=k_top, n_chunks=n_chunks),
        grid=(Bn, nq, n_chunks),
        in_specs=[pl.BlockSpec((1, tq, 2 * LANE), lambda b, i, c: (b, i, BLK_IQ // 2)),
                  pl.BlockSpec((1, S, LANE), lambda b, i, c: (b, 0, BLK_IK)),
                  pl.BlockSpec((1, tq, LANE), lambda b, i, c: (b, i, BLK_SMALL)),
                  pl.BlockSpec((tq, D), lambda b, i, c: (b * nq + i, 0)),
                  pl.BlockSpec((None, D, tc), lambda b, i, c: (l, 0, c)),
                  pl.BlockSpec((None, 1, tc), lambda b, i, c: (l, 0, c))],
        out_specs=[pl.BlockSpec((1, tq, S), lambda b, i, c: (b, i, 0)),
                   pl.BlockSpec((tq, tc), lambda b, i, c: (b * nq + i, c))],
        out_shape=[jax.ShapeDtypeStruct((Bn, S, S), jnp.bfloat16),
                   jax.ShapeDtypeStruct((T, N), jnp.bfloat16)],
        scratch_shapes=[pltpu.VMEM((tq, S), jnp.int32), pltpu.VMEM((tq, 1), jnp.int32)],
        name="dsa_select_gates",
        compiler_params=_cparams(("parallel", "parallel", "arbitrary")),
    )(main, main, f32grp, h, w_gate, b_gate)


def _t5_thresholds():
    max_exact = N_BUCKETS // 2
    rel = np.arange(MAX_DIST)
    nf = np.maximum(rel, 1).astype(np.float64)
    large = max_exact + (np.log(nf / max_exact) / math.log(MAX_DIST / max_exact)
                         * (N_BUCKETS - max_exact)).astype(np.int64)
    bucket = np.where(rel < max_exact, rel, np.minimum(large, N_BUCKETS - 1))
    out = []
    for b in range(1, N_BUCKETS):
        hit = np.nonzero(bucket >= b)[0]
        out.append(int(hit[0]) if hit.size else MAX_DIST)
    return out


def _attn_kernel(*refs, mode, head_off, lam_init):
    it = iter(refs)
    q_ref, k_ref, v_ref = next(it), next(it), next(it)
    if mode == "dsa":
        mask_ref = next(it)
    if mode in ("dsa", "diff"):
        tab_ref = next(it)
    if mode == "diff":
        lam_ref, gs_ref = next(it), next(it)
    if mode == "fox":
        fc_ref, fr_ref = next(it), next(it)
    o_ref = next(it)
    m_ref, acc_ref = next(it), next(it)
    if mode in ("dsa", "diff"):
        band_ref = next(it)

    i = pl.program_id(1)
    tq = q_ref.shape[1]
    nmap = 2 if mode == "diff" else 1
    dq = HEAD_DIM // nmap
    scale = dq ** -0.5
    biased = mode in ("dsa", "diff")

    c1 = scale * LOG2E
    if biased:
        @pl.when(i == 0)
        def _():
            r = lax.broadcasted_iota(jnp.int32, (tq, 2 * tq), 0)
            c = lax.broadcasted_iota(jnp.int32, (tq, 2 * tq), 1)
            rel = r - c + tq
            vals = [jnp.full((tq, 2 * tq), tab_ref[0, head_off + h] * LOG2E, jnp.float32)
                    for h in range(N_HEADS)]
            for b, first in enumerate(_t5_thresholds(), start=1):
                hit = rel >= first
                vals = [jnp.where(hit, tab_ref[b, head_off + h] * LOG2E, vals[h])
                        for h in range(N_HEADS)]
            for h in range(N_HEADS):
                band_ref[h] = vals[h]

    m_ref[...] = jnp.full(m_ref.shape, NEG, jnp.float32)
    acc_ref[...] = jnp.zeros(acc_ref.shape, jnp.float32)

    def chunk(off, kind, tk):
        ones = jnp.ones((tk, LANE), jnp.bfloat16)
        if mode == "dsa":
            keep = mask_ref[0, :, pl.ds(off, tk)].astype(jnp.float32) > 0.5
        elif kind == "far":
            keep = None
        else:
            keep = (lax.broadcasted_iota(jnp.int32, (tq, tk), 1) - (tk - tq)
                    <= lax.broadcasted_iota(jnp.int32, (tq, tk), 0))
        units = [(h, mp) for h in range(N_HEADS) for mp in range(nmap)]
        logits = []
        for h, mp in units:
            cols = slice(h * HEAD_DIM + mp * dq, h * HEAD_DIM + (mp + 1) * dq)
            logits.append(lax.dot_general(q_ref[0, :, cols], k_ref[0, pl.ds(off, tk), cols],
                                          (((1,), (1,)), ((), ())),
                                          preferred_element_type=jnp.float32))
        m_olds, m_news = [], []
        for u, (h, mp) in enumerate(units):
            t = logits[u] * c1
            if biased:
                if kind == "far":
                    t = t + tab_ref[N_BUCKETS - 1, head_off + h] * LOG2E
                elif kind == "near":
                    t = t + band_ref[h]
                else:
                    t = t + band_ref[h, :, tq:]
            else:
                t = (t + fc_ref[0, :, COL_FD + h:COL_FD + h + 1] * LOG2E
                     - fr_ref[0, h:h + 1, pl.ds(off, tk)] * LOG2E)
            if keep is not None:
                t = jnp.where(keep, t, NEG)
            logits[u] = t
            m_old = m_ref[u]
            m_olds.append(m_old)
            m_news.append(jnp.maximum(m_old, jnp.max(t, axis=1, keepdims=True)))
        probs = []
        for u in range(len(units)):
            probs.append(jnp.exp2(logits[u] - jnp.tile(m_news[u], (1, tk // LANE))).astype(v_ref.dtype))
            m_ref[u] = m_news[u]
        for u, (h, mp) in enumerate(units):
            vc = jnp.concatenate(
                [v_ref[0, pl.ds(off, tk), h * HEAD_DIM:(h + 1) * HEAD_DIM], ones], axis=1)
            alpha = jnp.exp2(m_olds[u] - m_news[u])
            acc_ref[u] = (jnp.tile(alpha, (1, 2)) * acc_ref[u]
                          + jnp.dot(probs[u], vc, preferred_element_type=jnp.float32))

    n_far = jnp.maximum(i - 1, 0)

    def far_body(c, carry):
        chunk(pl.multiple_of(c * 2 * tq, 2 * tq), "far", 2 * tq)
        return carry

    lax.fori_loop(0, n_far // 2, far_body, 0)

    @pl.when(n_far % 2 == 1)
    def _():
        chunk(pl.multiple_of((n_far - 1) * tq, tq), "far", tq)

    @pl.when(i >= 1)
    def _():
        chunk(pl.multiple_of((i - 1) * tq, tq), "near", 2 * tq)

    @pl.when(i == 0)
    def _():
        chunk(0, "diag", tq)

    if mode == "diff":
        lq = lam_ref[...]
        lam_val = (jnp.exp(jnp.sum(lq[0:1] * lq[1:2], axis=1, keepdims=True))
                   - jnp.exp(jnp.sum(lq[2:3] * lq[3:4], axis=1, keepdims=True)) + lam_init)
    for h in range(N_HEADS):
        a = acc_ref[h * nmap]
        o = a[:, :HEAD_DIM] / a[:, HEAD_DIM:]
        if mode == "diff":
            a1 = acc_ref[h * nmap + 1]
            o = o - lam_val * (a1[:, :HEAD_DIM] / a1[:, HEAD_DIM:])
            o = o * lax.rsqrt(jnp.mean(o * o, axis=-1, keepdims=True) + EPS) * gs_ref[...]
            o = o * (1.0 - lam_init)
        o_ref[0, :, h * HEAD_DIM:(h + 1) * HEAD_DIM] = o.astype(o_ref.dtype)


def _attention(main, mode, *, mask=None, tab=None, lam=None, g_subln=None,
               fcol=None, frow=None, lam_init=0.0):
    Bn, S, _ = main.shape
    blk_q, blk_k, blk_v, head_off, tq_pref = {
        "dsa": (BLK_QA, BLK_KA, BLK_VA, 0, 256),
        "diff": (BLK_QB, BLK_KB, BLK_VB, N_HEADS, 512),
        "fox": (BLK_QD, BLK_KD, BLK_VD, 0, 512),
    }[mode]
    tq = _tile(S, tq_pref)
    nmap = 2 if mode == "diff" else 1
    W = N_HEADS * HEAD_DIM
    args = [main, main, main]
    specs = [pl.BlockSpec((1, tq, W), lambda b, i: (b, i, blk_q // N_HEADS)),
             pl.BlockSpec((1, S, W), lambda b, i: (b, 0, blk_k // N_HEADS)),
             pl.BlockSpec((1, S, W), lambda b, i: (b, 0, blk_v // N_HEADS))]
    scratch = [pltpu.VMEM((N_HEADS * nmap, tq, LANE), jnp.float32),
               pltpu.VMEM((N_HEADS * nmap, tq, 2 * HEAD_DIM), jnp.float32)]
    if mode == "dsa":
        args.append(mask)
        specs.append(pl.BlockSpec((1, tq, S), lambda b, i: (b, i, 0)))
    if mode in ("dsa", "diff"):
        args.append(tab)
        specs.append(pl.BlockSpec(memory_space=pltpu.SMEM))
        scratch.append(pltpu.VMEM((N_HEADS, tq, 2 * tq), jnp.float32))
    if mode == "diff":
        args += [lam, g_subln.reshape(1, HEAD_DIM)]
        specs += [pl.BlockSpec((4, DIFF_DIM), lambda b, i: (0, 0)),
                  pl.BlockSpec((1, HEAD_DIM), lambda b, i: (0, 0))]
    if mode == "fox":
        args += [fcol, frow]
        specs += [pl.BlockSpec((1, tq, LANE), lambda b, i: (b, i, 0)),
                  pl.BlockSpec((1, N_HEADS, S), lambda b, i: (b, 0, 0))]
    return pl.pallas_call(
        functools.partial(_attn_kernel, mode=mode, head_off=head_off, lam_init=lam_init),
        grid=(Bn, S // tq),
        in_specs=specs,
        out_specs=pl.BlockSpec((1, tq, W), lambda b, i: (b, i, 0)),
        out_shape=jax.ShapeDtypeStruct((Bn, S, MIX_W), jnp.bfloat16),
        scratch_shapes=scratch,
        name="attn_" + mode,
        compiler_params=_cparams(("parallel", "arbitrary")),
    )(*args)


def _conv_kernel(bg_ref, cg_ref, xin_ref, w_ref, o_ref):
    u = cg_ref[0] * xin_ref[0]
    row = lax.broadcasted_iota(jnp.int32, u.shape, 0)
    w = w_ref[...]
    y = u * w[CONV_K - 1:CONV_K, :]
    for d in range(1, CONV_K):
        shifted = jnp.where(row >= d, pltpu.roll(u, d, 0), 0.0)
        y = y + shifted * w[CONV_K - 1 - d:CONV_K - d, :]
    o_ref[0] = (bg_ref[0] * y).astype(o_ref.dtype)


def _short_conv(f32grp, conv_w):
    Bn, S, _ = f32grp.shape
    tc = LANE
    per = CONV_CH // tc
    return pl.pallas_call(
        _conv_kernel,
        grid=(Bn, per),
        in_specs=[pl.BlockSpec((1, S, tc), lambda b, j: (b, 0, j)),
                  pl.BlockSpec((1, S, tc), lambda b, j: (b, 0, per + j)),
                  pl.BlockSpec((1, S, tc), lambda b, j: (b, 0, 2 * per + j)),
                  pl.BlockSpec((CONV_K, tc), lambda b, j: (0, j))],
        out_specs=pl.BlockSpec((1, S, tc), lambda b, j: (b, 0, j)),
        out_shape=jax.ShapeDtypeStruct((Bn, S, CONV_CH), jnp.bfloat16),
        name="short_conv",
        compiler_params=_cparams(("parallel", "parallel")),
    )(f32grp, f32grp, f32grp, conv_w)


def _mix_kernel(*refs):
    z_refs = refs[0:N_BRANCH]
    y_refs = refs[N_BRANCH:2 * N_BRANCH]
    wb_refs = refs[2 * N_BRANCH:3 * N_BRANCH]
    o_ref = refs[3 * N_BRANCH]
    merged = None
    for g in range(N_BRANCH):
        gate = jax.nn.sigmoid(z_refs[g][...].astype(jnp.float32))
        proj = jnp.dot(y_refs[g][...], wb_refs[g][...], preferred_element_type=jnp.float32)
        merged = gate * proj if merged is None else merged + gate * proj
    o_ref[...] = merged.astype(o_ref.dtype)


def _mix(z, ys, w_branch, l):
    T = z.shape[0]
    D = w_branch.shape[3]
    tm = _tile(T, 1024)
    tn = _tile(D, 512)
    per = D // tn
    specs = [pl.BlockSpec((tm, tn), lambda i, n, g=g: (i, g * per + n)) for g in range(N_BRANCH)]
    specs += [pl.BlockSpec((tm, MIX_W), lambda i, n: (i, 0)) for _ in range(N_BRANCH)]
    specs += [pl.BlockSpec((None, None, MIX_W, tn), lambda i, n, g=g: (l, g, 0, n))
              for g in range(N_BRANCH)]
    return pl.pallas_call(
        _mix_kernel,
        grid=(T // tm, per),
        in_specs=specs,
        out_specs=pl.BlockSpec((tm, tn), lambda i, n: (i, n)),
        out_shape=jax.ShapeDtypeStruct((T, D), jnp.bfloat16),
        name="gated_mix",
        compiler_params=_cparams(("parallel", "parallel")),
    )(*([z] * N_BRANCH), *ys, *([w_branch] * N_BRANCH))


def _ffn_kernel(h_ref, w1_ref, w2_ref, o_ref, u_ref, *, nf):
    f = pl.program_id(1)

    def first(slot):
        u = jnp.dot(h_ref[...], w1_ref[...], preferred_element_type=jnp.float32)
        u_ref[slot] = jnp.square(jnp.maximum(u, 0.0)).astype(u_ref.dtype)

    def second(slot):
        return jnp.dot(u_ref[slot], w2_ref[...], preferred_element_type=jnp.float32)

    @pl.when(f == 0)
    def _():
        o_ref[...] = jnp.zeros(o_ref.shape, o_ref.dtype)
        first(0)

    for parity in range(2):
        @pl.when((f > 0) & (f < nf) & (f % 2 == parity))
        def _(parity=parity):
            part = second(1 - parity)
            first(parity)
            o_ref[...] += part

    @pl.when(f == nf)
    def _():
        o_ref[...] += second((nf - 1) % 2)


def _ffn(h, w1, w2, l):
    T, D = h.shape
    F = w1.shape[2]
    tm = _tile(T, 1024)
    tf = _tile(F, 1024)
    nf = F // tf
    row = pl.BlockSpec((tm, D), lambda i, f: (i, 0))
    return pl.pallas_call(
        functools.partial(_ffn_kernel, nf=nf),
        grid=(T // tm, nf + 1),
        in_specs=[row,
                  pl.BlockSpec((None, D, tf), lambda i, f: (l, 0, jnp.minimum(f, nf - 1))),
                  pl.BlockSpec((None, tf, D), lambda i, f: (l, jnp.maximum(f - 1, 0), 0))],
        out_specs=row,
        out_shape=jax.ShapeDtypeStruct((T, D), jnp.float32),
        scratch_shapes=[pltpu.VMEM((2, tm, tf), jnp.bfloat16)],
        name="ffn",
        compiler_params=_cparams(("parallel", "arbitrary")),
    )(h, w1, w2)


def _split_in_proj(w_in):
    W = N_HEADS * HEAD_DIM
    sizes = (3 * W,
             IDX_HEADS * IDX_DIM + IDX_DIM,
             IDX_HEADS,
             3 * W,
             3 * CONV_CH,
             3 * W,
             N_HEADS)
    offs = np.concatenate([[0], np.cumsum(sizes)])
    a, idx, iw, b, cv, d, fd = [w_in[:, :, offs[j]:offs[j + 1]] for j in range(7)]
    L, D, _ = w_in.shape
    zeros = lambda n: jnp.zeros((L, D, n), w_in.dtype)
    main = jnp.concatenate([a, b, d, idx, zeros(N_MAIN - 9 * W - sizes[1])], axis=-1)
    f32grp = jnp.concatenate([cv, iw, fd, zeros(LANE - IDX_HEADS - N_HEADS)], axis=-1)
    return main.astype(jnp.bfloat16), f32grp.astype(jnp.bfloat16)


def kernel(x, c, w_ada, b_ada, g_norm1, w_in, b_f, conv_w, lam, g_subln, rel_bias,
           w_gate, b_gate, w_branch, w_o, g_norm2, w_ff1, w_ff2, g_final):
    Bn, S, D = x.shape
    L = w_ada.shape[0]
    T = Bn * S
    k_top = min(TOPK_MAX, S // 4)
    bf = jnp.bfloat16

    w_main, w_f32 = _split_in_proj(w_in)
    w_gate_b, w_branch_b, w_o_b = w_gate.astype(bf), w_branch.astype(bf), w_o.astype(bf)
    w_ff1_b, w_ff2_b = w_ff1.astype(bf), w_ff2.astype(bf)

    mod = _modulation(c, w_ada, b_ada).reshape(L, Bn, 6, 1, D)
    xt = x.reshape(T, D)
    delta, gate = None, None
    for l in range(L):
        sh1, sc1, gt1, sh2, sc2, gt2 = [mod[l, :, j] for j in range(6)]
        x_new, h = _norm(xt, g_norm1[l], sc1, sh1, seq=S, delta=delta, gate=gate)
        xt = xt if x_new is None else x_new
        main = _mm(h, w_main, l, bf, 1024).reshape(Bn, S, N_MAIN)
        f32grp = _mm(h, w_f32, l, jnp.float32, N_F32).reshape(Bn, S, N_F32)

        b_pad = jnp.zeros((1, LANE), jnp.float32).at[0, COL_FD:COL_FD + N_HEADS].set(b_f[l])
        fcol = _forget_cumsum(f32grp, b_pad)
        frow = jnp.transpose(fcol[:, :, COL_FD:COL_FD + N_HEADS], (0, 2, 1))

        mask, z = _select_and_gates(main, f32grp, h, w_gate_b, b_gate.reshape(L, 1, N_BRANCH * D),
                                    l, k_top, _tile(S, 512))
        lam_init = 0.8 - 0.6 * math.exp(-0.3 * l)
        y_a = _attention(main, "dsa", mask=mask, tab=rel_bias)
        y_b = _attention(main, "diff", tab=rel_bias, lam=lam[l], g_subln=g_subln[l],
                         lam_init=lam_init)
        y_c = _short_conv(f32grp, conv_w[l])
        y_d = _attention(main, "fox", fcol=fcol, frow=frow)
        ys = [y.reshape(T, MIX_W) for y in (y_a, y_b, y_c, y_d)]

        merged = _mix(z, ys, w_branch_b, l)
        xt, h2 = _mm_norm(merged, w_o_b, l, xt, gt1, g_norm2[l], sc2, sh2, seq=S)
        delta = _ffn(h2, w_ff1_b, w_ff2_b, l)
        gate = gt2
    out = _norm(xt, g_final, seq=S, delta=delta, gate=gate)
    return out.reshape(Bn, S, D)
```

```python
import functools
import math

import numpy as np
import jax
import jax.numpy as jnp
from jax import lax
from jax.experimental import pallas as pl
from jax.experimental.pallas import tpu as pltpu

HEAD_DIM = 128
N_HEADS = 4
DIFF_DIM = HEAD_DIM // 2
CONV_CH = 512
CONV_K = 3
IDX_HEADS = 4
IDX_DIM = 64
TOPK_MAX = 256
MIX_W = 512
N_BRANCH = 4
N_BUCKETS = 32
MAX_DIST = 128
EPS = 1e-6
NEG = -1e30
IDX_SCALE = (IDX_DIM ** -0.5) * (IDX_HEADS ** -0.5)
INT_MIN = -(2 ** 31)
LOG2E = 1.0 / math.log(2.0)

LANE = 128
VMEM_LIMIT = 56 * 1024 * 1024

BLK_QA, BLK_KA, BLK_VA = 0, 4, 8
BLK_QB, BLK_KB, BLK_VB = 12, 16, 20
BLK_QD, BLK_KD, BLK_VD = 24, 28, 32
BLK_IQ = 36
BLK_IK = 38
N_MAIN = 40 * LANE
N_F32 = 3 * CONV_CH + LANE
BLK_SMALL = 3 * CONV_CH // LANE
COL_IW, COL_FD = 0, 4


def _cparams(sem):
    return pltpu.CompilerParams(dimension_semantics=sem, vmem_limit_bytes=VMEM_LIMIT)


def _tile(n, pref):
    t = min(n, pref)
    while n % t:
        t -= 1
    return t


def _mod_kernel(c_ref, w_ref, b_ref, o_ref):
    c = c_ref[...]
    ca = (c * jax.nn.sigmoid(c)).astype(jnp.bfloat16)
    w = w_ref[0].astype(jnp.bfloat16)
    o_ref[0] = jnp.dot(ca, w, preferred_element_type=jnp.float32) + b_ref[0]


def _modulation(c, w_ada, b_ada):
    L, D, N = w_ada.shape
    Bn = c.shape[0]
    tn = _tile(N, 1024)
    return pl.pallas_call(
        _mod_kernel,
        grid=(L, N // tn),
        in_specs=[
            pl.BlockSpec((Bn, D), lambda l, j: (0, 0)),
            pl.BlockSpec((1, D, tn), lambda l, j: (l, 0, j)),
            pl.BlockSpec((1, 1, tn), lambda l, j: (l, 0, j)),
        ],
        out_specs=pl.BlockSpec((1, Bn, tn), lambda l, j: (l, 0, j)),
        out_shape=jax.ShapeDtypeStruct((L, Bn, N), jnp.float32),
        name="ada_mod",
        compiler_params=_cparams(("parallel", "parallel")),
    )(c, w_ada, b_ada.reshape(L, 1, N))


def _norm_kernel(*refs, has_delta, closing):
    it = iter(refs)
    x_ref = next(it)
    d_ref = next(it) if has_delta else None
    gate_ref = next(it) if has_delta else None
    g_ref = next(it)
    sc_ref, sh_ref = (None, None) if closing else (next(it), next(it))
    xo_ref = next(it) if has_delta and not closing else None
    h_ref = next(it)

    x = x_ref[...]
    if has_delta:
        x = x + gate_ref[0] * d_ref[...]
    if xo_ref is not None:
        xo_ref[...] = x
    if closing:
        y = x * lax.rsqrt(jnp.mean(x * x, axis=-1, keepdims=True) + EPS) * g_ref[...]
    else:
        y = _rms_mod(x, g_ref[...], sc_ref[0], sh_ref[0])
    h_ref[...] = y.astype(h_ref.dtype)


def _norm(x, g, sc=None, sh=None, *, seq, delta=None, gate=None):
    T, D = x.shape
    tm = _tile(seq, 512)
    per_seq = seq // tm
    has_delta = delta is not None
    closing = sc is None
    row = pl.BlockSpec((tm, D), lambda i: (i, 0))
    per_b = pl.BlockSpec((1, 1, D), lambda i: (i // per_seq, 0, 0))
    args, specs = [x], [row]
    if has_delta:
        args += [delta, gate]
        specs += [row, per_b]
    args.append(g.reshape(1, D))
    specs.append(pl.BlockSpec((1, D), lambda i: (0, 0)))
    if not closing:
        args += [sc, sh]
        specs += [per_b, per_b]
    out_shape, out_specs = [], []
    if has_delta and not closing:
        out_shape.append(jax.ShapeDtypeStruct((T, D), jnp.float32))
        out_specs.append(row)
    out_shape.append(jax.ShapeDtypeStruct((T, D), jnp.float32 if closing else jnp.bfloat16))
    out_specs.append(row)
    res = pl.pallas_call(
        functools.partial(_norm_kernel, has_delta=has_delta, closing=closing),
        grid=(T // tm,),
        in_specs=specs,
        out_specs=out_specs,
        out_shape=out_shape,
        name="norm",
        compiler_params=_cparams(("parallel",)),
    )(*args)
    if closing:
        return res[0]
    if has_delta:
        return res[0], res[1]
    return None, res[0]


def _rms_mod(x, g, sc, sh):
    y = x * lax.rsqrt(jnp.mean(x * x, axis=-1, keepdims=True) + EPS) * g
    return y * (1.0 + sc) + sh


def _mm_norm_kernel(a_ref, w_ref, x_ref, gate_ref, g_ref, sc_ref, sh_ref, xo_ref, h_ref):
    acc = jnp.dot(a_ref[...], w_ref[...], preferred_element_type=jnp.float32)
    x = x_ref[...] + gate_ref[0] * acc
    xo_ref[...] = x
    h_ref[...] = _rms_mod(x, g_ref[...], sc_ref[0], sh_ref[0]).astype(h_ref.dtype)


def _mm_norm(a, w, l, x, gate, g, sc, sh, *, seq):
    T, D = x.shape
    K = a.shape[1]
    tm = _tile(seq, 512)
    per_seq = seq // tm
    row = pl.BlockSpec((tm, D), lambda i: (i, 0))
    per_b = pl.BlockSpec((1, 1, D), lambda i: (i // per_seq, 0, 0))
    return pl.pallas_call(
        _mm_norm_kernel,
        grid=(T // tm,),
        in_specs=[pl.BlockSpec((tm, K), lambda i: (i, 0)),
                  pl.BlockSpec((None, K, D), lambda i: (l, 0, 0)),
                  row, per_b, pl.BlockSpec((1, D), lambda i: (0, 0)), per_b, per_b],
        out_specs=[row, row],
        out_shape=[jax.ShapeDtypeStruct((T, D), jnp.float32),
                   jax.ShapeDtypeStruct((T, D), jnp.bfloat16)],
        name="outproj_norm",
        compiler_params=_cparams(("parallel",)),
    )(a, w, x, gate, g.reshape(1, D), sc, sh)


def _mm_kernel(a_ref, w_ref, o_ref):
    o_ref[...] = jnp.dot(a_ref[...], w_ref[...],
                         preferred_element_type=jnp.float32).astype(o_ref.dtype)


def _mm(a, w, l, out_dtype, tn_pref):
    M, K = a.shape
    N = w.shape[2]
    tm = _tile(M, 1024)
    tn = _tile(N, tn_pref)
    return pl.pallas_call(
        _mm_kernel,
        grid=(M // tm, N // tn),
        in_specs=[pl.BlockSpec((tm, K), lambda i, j: (i, 0)),
                  pl.BlockSpec((None, K, tn), lambda i, j: (l, 0, j))],
        out_specs=pl.BlockSpec((tm, tn), lambda i, j: (i, j)),
        out_shape=jax.ShapeDtypeStruct((M, N), out_dtype),
        name="matmul",
        compiler_params=_cparams(("parallel", "parallel")),
    )(a, w)


def _cumsum_kernel(s_ref, b_ref, o_ref):
    S = s_ref.shape[1]
    x = s_ref[0] + b_ref[...]
    ls = jnp.minimum(x, 0.0) - jnp.log(1.0 + jnp.exp(-jnp.abs(x)))
    r = lax.broadcasted_iota(jnp.int32, (LANE, LANE), 0)
    c = lax.broadcasted_iota(jnp.int32, (LANE, LANE), 1)
    tri = jnp.where(r >= c, 1.0, 0.0).astype(jnp.bfloat16)
    hi = ls.astype(jnp.bfloat16)
    r1 = ls - hi.astype(jnp.float32)
    mid = r1.astype(jnp.bfloat16)
    lo = (r1 - mid.astype(jnp.float32)).astype(jnp.bfloat16)
    carry = jnp.zeros((1, LANE), jnp.float32)
    for j in range(S // LANE):
        rows = slice(j * LANE, (j + 1) * LANE)
        blk = (jnp.dot(tri, hi[rows], preferred_element_type=jnp.float32)
               + jnp.dot(tri, mid[rows], preferred_element_type=jnp.float32)
               + jnp.dot(tri, lo[rows], preferred_element_type=jnp.float32)) + carry
        o_ref[0, j * LANE:(j + 1) * LANE, :] = blk
        carry = blk[LANE - 1:LANE, :]


def _forget_cumsum(small, b_pad):
    Bn, S, _ = small.shape
    return pl.pallas_call(
        _cumsum_kernel,
        grid=(Bn,),
        in_specs=[pl.BlockSpec((1, S, LANE), lambda b: (b, 0, BLK_SMALL)),
                  pl.BlockSpec((1, LANE), lambda b: (0, 0))],
        out_specs=pl.BlockSpec((1, S, LANE), lambda b: (b, 0, 0)),
        out_shape=jax.ShapeDtypeStruct((Bn, S, LANE), jnp.float32),
        name="forget_cumsum",
        compiler_params=_cparams(("parallel",)),
    )(small, b_pad)


N_KEY_BITS = 32


def _select_block(iq_ref, ik_ref, sm_ref, h_ref, wg_ref, bg_ref, mask_ref, z_ref, key_ref, thr_ref,
                  *, k_top, blk, n_chunks):
    c = pl.program_id(2)
    tq = iq_ref.shape[1]
    S = ik_ref.shape[1]
    W = (blk + 1) * tq
    kf = jnp.float32(k_top)
    causal = (lax.broadcasted_iota(jnp.int32, (tq, W), 1)
              <= blk * tq + lax.broadcasted_iota(jnp.int32, (tq, W), 0))

    @pl.when(c == 0)
    def _():
        iq = iq_ref[0]
        ik = ik_ref[0, :W, :IDX_DIM]
        iw = sm_ref[0][:, COL_IW:COL_IW + IDX_HEADS] * IDX_SCALE
        score = jnp.zeros((tq, W), jnp.float32)
        for h in range(IDX_HEADS):
            d = lax.dot_general(iq[:, h * IDX_DIM:(h + 1) * IDX_DIM], ik,
                                (((1,), (1,)), ((), ())), preferred_element_type=jnp.float32)
            score = score + jnp.where(d > 0.0, d * iw[:, h:h + 1], 0.0)
        score = jnp.where(causal, score, NEG)
        bits = lax.bitcast_convert_type(score, jnp.int32)
        key_ref[:, :W] = bits ^ ((bits >> 31) & 0x7FFFFFFF)
        thr_ref[...] = jnp.full(thr_ref.shape, INT_MIN, jnp.int32)

    per_step = N_KEY_BITS // n_chunks
    tz = z_ref.shape[1] // per_step
    thr = thr_ref[...]
    for j in range(per_step):
        zs = slice(j * tz, (j + 1) * tz)
        z = jnp.dot(h_ref[...], wg_ref[:, zs], preferred_element_type=jnp.float32) + bg_ref[:, zs]
        p = c * per_step + j
        cand = jnp.where(p == 0, 0, thr | jnp.left_shift(jnp.int32(1), N_KEY_BITS - 1 - p))
        cnt = jnp.sum(jnp.where(key_ref[:, :W] >= cand, 1.0, 0.0), axis=1, keepdims=True)
        thr = jnp.where(cnt >= kf, cand, thr)
        z_ref[:, zs] = z.astype(z_ref.dtype)
    thr_ref[...] = thr

    @pl.when(c == n_chunks - 1)
    def _():
        _select_write(mask_ref, key_ref, thr_ref[...], causal, kf, blk)


def _select_write(mask_ref, key_ref, thr, causal, kf, blk):
    tq, S = mask_ref.shape[1], mask_ref.shape[2]
    W = (blk + 1) * tq
    key = key_ref[:, :W]
    above = key > thr
    need = kf - jnp.sum(jnp.where(above, 1.0, 0.0), axis=1, keepdims=True)
    upper = jnp.where(lax.broadcasted_iota(jnp.int32, (tq, tq), 0)
                      <= lax.broadcasted_iota(jnp.int32, (tq, tq), 1), 1.0, 0.0).astype(jnp.bfloat16)
    seen = jnp.zeros((tq, 1), jnp.float32)
    for c in range(blk + 1):
        cs = slice(c * tq, (c + 1) * tq)
        tie = key[:, cs] == thr
        incl = seen + jnp.dot(jnp.where(tie, 1.0, 0.0).astype(jnp.bfloat16), upper,
                              preferred_element_type=jnp.float32)
        sel = above[:, cs] | (tie & (incl <= need))
        if c == blk:
            sel = sel & causal[:, cs]
        mask_ref[0, :, cs] = jnp.where(sel, 1.0, 0.0).astype(mask_ref.dtype)
        seen = incl[:, tq - 1:tq]
    if W < S:
        mask_ref[0, :, W:] = jnp.zeros((tq, S - W), mask_ref.dtype)


def _select_kernel(*refs, k_top, n_chunks):
    iq_ref, ik_ref = refs[0], refs[1]
    i = pl.program_id(1)
    for blk in range(ik_ref.shape[1] // iq_ref.shape[1]):
        @pl.when(i == blk)
        def _(blk=blk):
            _select_block(*refs, k_top=k_top, blk=blk, n_chunks=n_chunks)


def _select_and_gates(main, f32grp, h, w_gate, b_gate, l, k_top, tq):
    Bn, S, _ = main.shape
    T, D = h.shape
    N = w_gate.shape[2]
    nq = S // tq
    tc = _tile(N, 2048)
    n_chunks = N // tc
    assert N_KEY_BITS % n_chunks == 0
    return pl.pallas_call(
        functools.partial(_select_kernel, k_top=k_top, n_chunks=n_chunks),
        grid=(Bn, nq, n_chunks),
        in_specs=[pl.BlockSpec((1, tq, 2 * LANE), lambda b, i, c: (b, i, BLK_IQ // 2)),
                  pl.BlockSpec((1, S, LANE), lambda b, i, c: (b, 0, BLK_IK)),
                  pl.BlockSpec((1, tq, LANE), lambda b, i, c: (b, i, BLK_SMALL)),
                  pl.BlockSpec((tq, D), lambda b, i, c: (b * nq + i, 0)),
                  pl.BlockSpec((None, D, tc), lambda b, i, c: (l, 0, c)),
                  pl.BlockSpec((None, 1, tc), lambda b, i, c: (l, 0, c))],
        out_specs=[pl.BlockSpec((1, tq, S), lambda b, i, c: (b, i, 0)),
                   pl.BlockSpec((tq, tc), lambda b, i, c: (b * nq + i, c))],
        out_shape=[jax.ShapeDtypeStruct((Bn, S, S), jnp.bfloat16),
                   jax.ShapeDtypeStruct((T, N), jnp.bfloat16)],
        scratch_shapes=[pltpu.VMEM((tq, S), jnp.int32), pltpu.VMEM((tq, 1), jnp.int32)],
        name="dsa_select_gates",
        compiler_params=_cparams(("parallel", "parallel", "arbitrary")),
    )(main, main, f32grp, h, w_gate, b_gate)


def _t5_thresholds():
    max_exact = N_BUCKETS // 2
    rel = np.arange(MAX_DIST)
    nf = np.maximum(rel, 1).astype(np.float64)
    large = max_exact + (np.log(nf / max_exact) / math.log(MAX_DIST / max_exact)
                         * (N_BUCKETS - max_exact)).astype(np.int64)
    bucket = np.where(rel < max_exact, rel, np.minimum(large, N_BUCKETS - 1))
    out = []
    for b in range(1, N_BUCKETS):
        hit = np.nonzero(bucket >= b)[0]
        out.append(int(hit[0]) if hit.size else MAX_DIST)
    return out


def _attn_kernel(*refs, mode, head_off, lam_init):
    it = iter(refs)
    q_ref, k_ref, v_ref = next(it), next(it), next(it)
    if mode == "dsa":
        mask_ref = next(it)
    if mode in ("dsa", "diff"):
        tab_ref = next(it)
    if mode == "diff":
        lam_ref, gs_ref = next(it), next(it)
    if mode == "fox":
        fc_ref, fr_ref = next(it), next(it)
    o_ref = next(it)
    m_ref, acc_ref = next(it), next(it)
    if mode in ("dsa", "diff"):
        band_ref = next(it)

    i = pl.program_id(1)
    tq = q_ref.shape[1]
    nmap = 2 if mode == "diff" else 1
    dq = HEAD_DIM // nmap
    scale = dq ** -0.5
    biased = mode in ("dsa", "diff")

    c1 = scale * LOG2E
    if biased:
        @pl.when(i == 0)
        def _():
            r = lax.broadcasted_iota(jnp.int32, (tq, 2 * tq), 0)
            c = lax.broadcasted_iota(jnp.int32, (tq, 2 * tq), 1)
            rel = r - c + tq
            vals = [jnp.full((tq, 2 * tq), tab_ref[0, head_off + h] * LOG2E, jnp.float32)
                    for h in range(N_HEADS)]
            for b, first in enumerate(_t5_thresholds(), start=1):
                hit = rel >= first
                vals = [jnp.where(hit, tab_ref[b, head_off + h] * LOG2E, vals[h])
                        for h in range(N_HEADS)]
            for h in range(N_HEADS):
                band_ref[h] = vals[h]

    m_ref[...] = jnp.full(m_ref.shape, NEG, jnp.float32)
    acc_ref[...] = jnp.zeros(acc_ref.shape, jnp.float32)

    def chunk(off, kind, tk):
        ones = jnp.ones((tk, LANE), jnp.bfloat16)
        if mode == "dsa":
            keep = mask_ref[0, :, pl.ds(off, tk)].astype(jnp.float32) > 0.5
        elif kind == "far":
            keep = None
        else:
            keep = (lax.broadcasted_iota(jnp.int32, (tq, tk), 1) - (tk - tq)
                    <= lax.broadcasted_iota(jnp.int32, (tq, tk), 0))
        units = [(h, mp) for h in range(N_HEADS) for mp in range(nmap)]
        logits = []
        for h, mp in units:
            cols = slice(h * HEAD_DIM + mp * dq, h * HEAD_DIM + (mp + 1) * dq)
            logits.append(lax.dot_general(q_ref[0, :, cols], k_ref[0, pl.ds(off, tk), cols],
                                          (((1,), (1,)), ((), ())),
                                          preferred_element_type=jnp.float32))
        m_olds, m_news = [], []
        for u, (h, mp) in enumerate(units):
            t = logits[u] * c1
            if biased:
                if kind == "far":
                    t = t + tab_ref[N_BUCKETS - 1, head_off + h] * LOG2E
                elif kind == "near":
                    t = t + band_ref[h]
                else:
                    t = t + band_ref[h, :, tq:]
            else:
                t = (t + fc_ref[0, :, COL_FD + h:COL_FD + h + 1] * LOG2E
                     - fr_ref[0, h:h + 1, pl.ds(off, tk)] * LOG2E)
            if keep is not None:
                t = jnp.where(keep, t, NEG)
            logits[u] = t
            m_old = m_ref[u]
            m_olds.append(m_old)
            m_news.append(jnp.maximum(m_old, jnp.max(t, axis=1, keepdims=True)))
        probs = []
        for u in range(len(units)):
            probs.append(jnp.exp2(logits[u] - jnp.tile(m_news[u], (1, tk // LANE))).astype(v_ref.dtype))
            m_ref[u] = m_news[u]
        for u, (h, mp) in enumerate(units):
            vc = jnp.concatenate(
                [v_ref[0, pl.ds(off, tk), h * HEAD_DIM:(h + 1) * HEAD_DIM], ones], axis=1)
            alpha = jnp.exp2(m_olds[u] - m_news[u])
            acc_ref[u] = (jnp.tile(alpha, (1, 2)) * acc_ref[u]
                          + jnp.dot(probs[u], vc, preferred_element_type=jnp.float32))

    n_far = jnp.maximum(i - 1, 0)

    def far_body(c, carry):
        chunk(pl.multiple_of(c * 2 * tq, 2 * tq), "far", 2 * tq)
        return carry

    lax.fori_loop(0, n_far // 2, far_body, 0)

    @pl.when(n_far % 2 == 1)
    def _():
        chunk(pl.multiple_of((n_far - 1) * tq, tq), "far", tq)

    @pl.when(i >= 1)
    def _():
        chunk(pl.multiple_of((i - 1) * tq, tq), "near", 2 * tq)

    @pl.when(i == 0)
    def _():
        chunk(0, "diag", tq)

    if mode == "diff":
        lq = lam_ref[...]
        lam_val = (jnp.exp(jnp.sum(lq[0:1] * lq[1:2], axis=1, keepdims=True))
                   - jnp.exp(jnp.sum(lq[2:3] * lq[3:4], axis=1, keepdims=True)) + lam_init)
    for h in range(N_HEADS):
        a = acc_ref[h * nmap]
        o = a[:, :HEAD_DIM] / a[:, HEAD_DIM:]
        if mode == "diff":
            a1 = acc_ref[h * nmap + 1]
            o = o - lam_val * (a1[:, :HEAD_DIM] / a1[:, HEAD_DIM:])
            o = o * lax.rsqrt(jnp.mean(o * o, axis=-1, keepdims=True) + EPS) * gs_ref[...]
            o = o * (1.0 - lam_init)
        o_ref[0, :, h * HEAD_DIM:(h + 1) * HEAD_DIM] = o.astype(o_ref.dtype)


def _attention(main, mode, *, mask=None, tab=None, lam=None, g_subln=None,
               fcol=None, frow=None, lam_init=0.0):
    Bn, S, _ = main.shape
    blk_q, blk_k, blk_v, head_off, tq_pref = {
        "dsa": (BLK_QA, BLK_KA, BLK_VA, 0, 256),
        "diff": (BLK_QB, BLK_KB, BLK_VB, N_HEADS, 512),
        "fox": (BLK_QD, BLK_KD, BLK_VD, 0, 512),
    }[mode]
    tq = _tile(S, tq_pref)
    nmap = 2 if mode == "diff" else 1
    W = N_HEADS * HEAD_DIM
    args = [main, main, main]
    specs = [pl.BlockSpec((1, tq, W), lambda b, i: (b, i, blk_q // N_HEADS)),
             pl.BlockSpec((1, S, W), lambda b, i: (b, 0, blk_k // N_HEADS)),
             pl.BlockSpec((1, S, W), lambda b, i: (b, 0, blk_v // N_HEADS))]
    scratch = [pltpu.VMEM((N_HEADS * nmap, tq, LANE), jnp.float32),
               pltpu.VMEM((N_HEADS * nmap, tq, 2 * HEAD_DIM), jnp.float32)]
    if mode == "dsa":
        args.append(mask)
        specs.append(pl.BlockSpec((1, tq, S), lambda b, i: (b, i, 0)))
    if mode in ("dsa", "diff"):
        args.append(tab)
        specs.append(pl.BlockSpec(memory_space=pltpu.SMEM))
        scratch.append(pltpu.VMEM((N_HEADS, tq, 2 * tq), jnp.float32))
    if mode == "diff":
        args += [lam, g_subln.reshape(1, HEAD_DIM)]
        specs += [pl.BlockSpec((4, DIFF_DIM), lambda b, i: (0, 0)),
                  pl.BlockSpec((1, HEAD_DIM), lambda b, i: (0, 0))]
    if mode == "fox":
        args += [fcol, frow]
        specs += [pl.BlockSpec((1, tq, LANE), lambda b, i: (b, i, 0)),
                  pl.BlockSpec((1, N_HEADS, S), lambda b, i: (b, 0, 0))]
    return pl.pallas_call(
        functools.partial(_attn_kernel, mode=mode, head_off=head_off, lam_init=lam_init),
        grid=(Bn, S // tq),
        in_specs=specs,
        out_specs=pl.BlockSpec((1, tq, W), lambda b, i: (b, i, 0)),
        out_shape=jax.ShapeDtypeStruct((Bn, S, MIX_W), jnp.bfloat16),
        scratch_shapes=scratch,
        name="attn_" + mode,
        compiler_params=_cparams(("parallel", "arbitrary")),
    )(*args)


def _conv_kernel(bg_ref, cg_ref, xin_ref, w_ref, o_ref):
    u = cg_ref[0] * xin_ref[0]
    row = lax.broadcasted_iota(jnp.int32, u.shape, 0)
    w = w_ref[...]
    y = u * w[CONV_K - 1:CONV_K, :]
    for d in range(1, CONV_K):
        shifted = jnp.where(row >= d, pltpu.roll(u, d, 0), 0.0)
        y = y + shifted * w[CONV_K - 1 - d:CONV_K - d, :]
    o_ref[0] = (bg_ref[0] * y).astype(o_ref.dtype)


def _short_conv(f32grp, conv_w):
    Bn, S, _ = f32grp.shape
    tc = LANE
    per = CONV_CH // tc
    return pl.pallas_call(
        _conv_kernel,
        grid=(Bn, per),
        in_specs=[pl.BlockSpec((1, S, tc), lambda b, j: (b, 0, j)),
                  pl.BlockSpec((1, S, tc), lambda b, j: (b, 0, per + j)),
                  pl.BlockSpec((1, S, tc), lambda b, j: (b, 0, 2 * per + j)),
                  pl.BlockSpec((CONV_K, tc), lambda b, j: (0, j))],
        out_specs=pl.BlockSpec((1, S, tc), lambda b, j: (b, 0, j)),
        out_shape=jax.ShapeDtypeStruct((Bn, S, CONV_CH), jnp.bfloat16),
        name="short_conv",
        compiler_params=_cparams(("parallel", "parallel")),
    )(f32grp, f32grp, f32grp, conv_w)


def _mix_kernel(*refs):
    z_refs = refs[0:N_BRANCH]
    y_refs = refs[N_BRANCH:2 * N_BRANCH]
    wb_refs = refs[2 * N_BRANCH:3 * N_BRANCH]
    o_ref = refs[3 * N_BRANCH]
    merged = None
    for g in range(N_BRANCH):
        gate = jax.nn.sigmoid(z_refs[g][...].astype(jnp.float32))
        proj = jnp.dot(y_refs[g][...], wb_refs[g][...], preferred_element_type=jnp.float32)
        merged = gate * proj if merged is None else merged + gate * proj
    o_ref[...] = merged.astype(o_ref.dtype)


def _mix(z, ys, w_branch, l):
    T = z.shape[0]
    D = w_branch.shape[3]
    tm = _tile(T, 1024)
    tn = _tile(D, 1024)
    per = D // tn
    specs = [pl.BlockSpec((tm, tn), lambda i, n, g=g: (i, g * per + n)) for g in range(N_BRANCH)]
    specs += [pl.BlockSpec((tm, MIX_W), lambda i, n: (i, 0)) for _ in range(N_BRANCH)]
    specs += [pl.BlockSpec((None, None, MIX_W, tn), lambda i, n, g=g: (l, g, 0, n))
              for g in range(N_BRANCH)]
    return pl.pallas_call(
        _mix_kernel,
        grid=(T // tm, per),
        in_specs=specs,
        out_specs=pl.BlockSpec((tm, tn), lambda i, n: (i, n)),
        out_shape=jax.ShapeDtypeStruct((T, D), jnp.bfloat16),
        name="gated_mix",
        compiler_params=_cparams(("parallel", "parallel")),
    )(*([z] * N_BRANCH), *ys, *([w_branch] * N_BRANCH))


def _ffn_kernel(h_ref, w1_ref, w2_ref, o_ref, u_ref, *, nf):
    f = pl.program_id(1)

    def first(slot):
        u = jnp.dot(h_ref[...], w1_ref[...], preferred_element_type=jnp.float32)
        u_ref[slot] = jnp.square(jnp.maximum(u, 0.0)).astype(u_ref.dtype)

    def second(slot):
        return jnp.dot(u_ref[slot], w2_ref[...], preferred_element_type=jnp.float32)

    @pl.when(f == 0)
    def _():
        o_ref[...] = jnp.zeros(o_ref.shape, o_ref.dtype)
        first(0)

    for parity in range(2):
        @pl.when((f > 0) & (f < nf) & (f % 2 == parity))
        def _(parity=parity):
            part = second(1 - parity)
            first(parity)
            o_ref[...] += part

    @pl.when(f == nf)
    def _():
        o_ref[...] += second((nf - 1) % 2)


def _ffn(h, w1, w2, l):
    T, D = h.shape
    F = w1.shape[2]
    tm = _tile(T, 1024)
    tf = _tile(F, 1024)
    nf = F // tf
    row = pl.BlockSpec((tm, D), lambda i, f: (i, 0))
    return pl.pallas_call(
        functools.partial(_ffn_kernel, nf=nf),
        grid=(T // tm, nf + 1),
        in_specs=[row,
                  pl.BlockSpec((None, D, tf), lambda i, f: (l, 0, jnp.minimum(f, nf - 1))),
                  pl.BlockSpec((None, tf, D), lambda i, f: (l, jnp.maximum(f - 1, 0), 0))],
        out_specs=row,
        out_shape=jax.ShapeDtypeStruct((T, D), jnp.float32),
        scratch_shapes=[pltpu.VMEM((2, tm, tf), jnp.bfloat16)],
        name="ffn",
        compiler_params=_cparams(("parallel", "arbitrary")),
    )(h, w1, w2)


def _split_in_proj(w_in):
    W = N_HEADS * HEAD_DIM
    sizes = (3 * W,
             IDX_HEADS * IDX_DIM + IDX_DIM,
             IDX_HEADS,
             3 * W,
             3 * CONV_CH,
             3 * W,
             N_HEADS)
    offs = np.concatenate([[0], np.cumsum(sizes)])
    a, idx, iw, b, cv, d, fd = [w_in[:, :, offs[j]:offs[j + 1]] for j in range(7)]
    L, D, _ = w_in.shape
    zeros = lambda n: jnp.zeros((L, D, n), w_in.dtype)
    main = jnp.concatenate([a, b, d, idx, zeros(N_MAIN - 9 * W - sizes[1])], axis=-1)
    f32grp = jnp.concatenate([cv, iw, fd, zeros(LANE - IDX_HEADS - N_HEADS)], axis=-1)
    return main.astype(jnp.bfloat16), f32grp.astype(jnp.bfloat16)


def kernel(x, c, w_ada, b_ada, g_norm1, w_in, b_f, conv_w, lam, g_subln, rel_bias,
           w_gate, b_gate, w_branch, w_o, g_norm2, w_ff1, w_ff2, g_final):
    Bn, S, D = x.shape
    L = w_ada.shape[0]
    T = Bn * S
    k_top = min(TOPK_MAX, S // 4)
    bf = jnp.bfloat16

    w_main, w_f32 = _split_in_proj(w_in)
    w_gate_b, w_branch_b, w_o_b = w_gate.astype(bf), w_branch.astype(bf), w_o.astype(bf)
    w_ff1_b, w_ff2_b = w_ff1.astype(bf), w_ff2.astype(bf)

    mod = _modulation(c, w_ada, b_ada).reshape(L, Bn, 6, 1, D)
    xt = x.reshape(T, D)
    delta, gate = None, None
    for l in range(L):
        sh1, sc1, gt1, sh2, sc2, gt2 = [mod[l, :, j] for j in range(6)]
        x_new, h = _norm(xt, g_norm1[l], sc1, sh1, seq=S, delta=delta, gate=gate)
        xt = xt if x_new is None else x_new
        main = _mm(h, w_main, l, bf, 1024).reshape(Bn, S, N_MAIN)
        f32grp = _mm(h, w_f32, l, jnp.float32, N_F32).reshape(Bn, S, N_F32)

        b_pad = jnp.zeros((1, LANE), jnp.float32).at[0, COL_FD:COL_FD + N_HEADS].set(b_f[l])
        fcol = _forget_cumsum(f32grp, b_pad)
        frow = jnp.transpose(fcol[:, :, COL_FD:COL_FD + N_HEADS], (0, 2, 1))

        mask, z = _select_and_gates(main, f32grp, h, w_gate_b, b_gate.reshape(L, 1, N_BRANCH * D),
                                    l, k_top, _tile(S, 512))
        lam_init = 0.8 - 0.6 * math.exp(-0.3 * l)
        y_a = _attention(main, "dsa", mask=mask, tab=rel_bias)
        y_b = _attention(main, "diff", tab=rel_bias, lam=lam[l], g_subln=g_subln[l],
                         lam_init=lam_init)
        y_c = _short_conv(f32grp, conv_w[l])
        y_d = _attention(main, "fox", fcol=fcol, frow=frow)
        ys = [y.reshape(T, MIX_W) for y in (y_a, y_b, y_c, y_d)]

        merged = _mix(z, ys, w_branch_b, l)
        xt, h2 = _mm_norm(merged, w_o_b, l, xt, gt1, g_norm2[l], sc2, sh2, seq=S)
        delta = _ffn(h2, w_ff1_b, w_ff2_b, l)
        gate = gt2
    out = _norm(xt, g_final, seq=S, delta=delta, gate=gate)
    return out.reshape(Bn, S, D)
```

```python
import functools
import math

import numpy as np
import jax
import jax.numpy as jnp
from jax import lax
from jax.experimental import pallas as pl
from jax.experimental.pallas import tpu as pltpu

HEAD_DIM = 128
N_HEADS = 4
DIFF_DIM = HEAD_DIM // 2
CONV_CH = 512
CONV_K = 3
IDX_HEADS = 4
IDX_DIM = 64
TOPK_MAX = 256
MIX_W = 512
N_BRANCH = 4
N_BUCKETS = 32
MAX_DIST = 128
EPS = 1e-6
NEG = -1e30
IDX_SCALE = (IDX_DIM ** -0.5) * (IDX_HEADS ** -0.5)
INT_MIN = -(2 ** 31)
LOG2E = 1.0 / math.log(2.0)

LANE = 128
VMEM_LIMIT = 56 * 1024 * 1024

BLK_QA, BLK_KA, BLK_VA = 0, 4, 8
BLK_QB, BLK_KB, BLK_VB = 12, 16, 20
BLK_QD, BLK_KD, BLK_VD = 24, 28, 32
BLK_IQ = 36
BLK_IK = 38
N_MAIN = 40 * LANE
N_F32 = 3 * CONV_CH + LANE
BLK_SMALL = 3 * CONV_CH // LANE
COL_IW, COL_FD = 0, 4


def _cparams(sem):
    return pltpu.CompilerParams(dimension_semantics=sem, vmem_limit_bytes=VMEM_LIMIT)


def _tile(n, pref):
    t = min(n, pref)
    while n % t:
        t -= 1
    return t


def _mod_kernel(c_ref, w_ref, b_ref, o_ref):
    c = c_ref[...]
    ca = (c * jax.nn.sigmoid(c)).astype(jnp.bfloat16)
    w = w_ref[0].astype(jnp.bfloat16)
    o_ref[0] = jnp.dot(ca, w, preferred_element_type=jnp.float32) + b_ref[0]


def _modulation(c, w_ada, b_ada):
    L, D, N = w_ada.shape
    Bn = c.shape[0]
    tn = _tile(N, 1024)
    return pl.pallas_call(
        _mod_kernel,
        grid=(L, N // tn),
        in_specs=[
            pl.BlockSpec((Bn, D), lambda l, j: (0, 0)),
            pl.BlockSpec((1, D, tn), lambda l, j: (l, 0, j)),
            pl.BlockSpec((1, 1, tn), lambda l, j: (l, 0, j)),
        ],
        out_specs=pl.BlockSpec((1, Bn, tn), lambda l, j: (l, 0, j)),
        out_shape=jax.ShapeDtypeStruct((L, Bn, N), jnp.float32),
        name="ada_mod",
        compiler_params=_cparams(("parallel", "parallel")),
    )(c, w_ada, b_ada.reshape(L, 1, N))


def _norm_kernel(*refs, has_delta, closing):
    it = iter(refs)
    x_ref = next(it)
    d_ref = next(it) if has_delta else None
    gate_ref = next(it) if has_delta else None
    g_ref = next(it)
    sc_ref, sh_ref = (None, None) if closing else (next(it), next(it))
    xo_ref = next(it) if has_delta and not closing else None
    h_ref = next(it)

    x = x_ref[...]
    if has_delta:
        x = x + gate_ref[0] * d_ref[...]
    if xo_ref is not None:
        xo_ref[...] = x
    if closing:
        y = x * lax.rsqrt(jnp.mean(x * x, axis=-1, keepdims=True) + EPS) * g_ref[...]
    else:
        y = _rms_mod(x, g_ref[...], sc_ref[0], sh_ref[0])
    h_ref[...] = y.astype(h_ref.dtype)


def _norm(x, g, sc=None, sh=None, *, seq, delta=None, gate=None):
    T, D = x.shape
    tm = _tile(seq, 512)
    per_seq = seq // tm
    has_delta = delta is not None
    closing = sc is None
    row = pl.BlockSpec((tm, D), lambda i: (i, 0))
    per_b = pl.BlockSpec((1, 1, D), lambda i: (i // per_seq, 0, 0))
    args, specs = [x], [row]
    if has_delta:
        args += [delta, gate]
        specs += [row, per_b]
    args.append(g.reshape(1, D))
    specs.append(pl.BlockSpec((1, D), lambda i: (0, 0)))
    if not closing:
        args += [sc, sh]
        specs += [per_b, per_b]
    out_shape, out_specs = [], []
    if has_delta and not closing:
        out_shape.append(jax.ShapeDtypeStruct((T, D), jnp.float32))
        out_specs.append(row)
    out_shape.append(jax.ShapeDtypeStruct((T, D), jnp.float32 if closing else jnp.bfloat16))
    out_specs.append(row)
    res = pl.pallas_call(
        functools.partial(_norm_kernel, has_delta=has_delta, closing=closing),
        grid=(T // tm,),
        in_specs=specs,
        out_specs=out_specs,
        out_shape=out_shape,
        name="norm",
        compiler_params=_cparams(("parallel",)),
    )(*args)
    if closing:
        return res[0]
    if has_delta:
        return res[0], res[1]
    return None, res[0]


def _rms_mod(x, g, sc, sh):
    y = x * lax.rsqrt(jnp.mean(x * x, axis=-1, keepdims=True) + EPS) * g
    return y * (1.0 + sc) + sh


def _mm_norm_kernel(a_ref, w_ref, x_ref, gate_ref, g_ref, sc_ref, sh_ref, xo_ref, h_ref):
    acc = jnp.dot(a_ref[...], w_ref[...], preferred_element_type=jnp.float32)
    x = x_ref[...] + gate_ref[0] * acc
    xo_ref[...] = x
    h_ref[...] = _rms_mod(x, g_ref[...], sc_ref[0], sh_ref[0]).astype(h_ref.dtype)


def _mm_norm(a, w, l, x, gate, g, sc, sh, *, seq):
    T, D = x.shape
    K = a.shape[1]
    tm = _tile(seq, 512)
    per_seq = seq // tm
    row = pl.BlockSpec((tm, D), lambda i: (i, 0))
    per_b = pl.BlockSpec((1, 1, D), lambda i: (i // per_seq, 0, 0))
    return pl.pallas_call(
        _mm_norm_kernel,
        grid=(T // tm,),
        in_specs=[pl.BlockSpec((tm, K), lambda i: (i, 0)),
                  pl.BlockSpec((None, K, D), lambda i: (l, 0, 0)),
                  row, per_b, pl.BlockSpec((1, D), lambda i: (0, 0)), per_b, per_b],
        out_specs=[row, row],
        out_shape=[jax.ShapeDtypeStruct((T, D), jnp.float32),
                   jax.ShapeDtypeStruct((T, D), jnp.bfloat16)],
        name="outproj_norm",
        compiler_params=_cparams(("parallel",)),
    )(a, w, x, gate, g.reshape(1, D), sc, sh)


def _mm_kernel(a_ref, w_ref, o_ref):
    o_ref[...] = jnp.dot(a_ref[...], w_ref[...],
                         preferred_element_type=jnp.float32).astype(o_ref.dtype)


def _mm(a, w, l, out_dtype, tn_pref):
    M, K = a.shape
    N = w.shape[2]
    tm = _tile(M, 1024)
    tn = _tile(N, tn_pref)
    return pl.pallas_call(
        _mm_kernel,
        grid=(M // tm, N // tn),
        in_specs=[pl.BlockSpec((tm, K), lambda i, j: (i, 0)),
                  pl.BlockSpec((None, K, tn), lambda i, j: (l, 0, j))],
        out_specs=pl.BlockSpec((tm, tn), lambda i, j: (i, j)),
        out_shape=jax.ShapeDtypeStruct((M, N), out_dtype),
        name="matmul",
        compiler_params=_cparams(("parallel", "parallel")),
    )(a, w)


def _cumsum_kernel(s_ref, b_ref, o_ref):
    S = s_ref.shape[1]
    x = s_ref[0] + b_ref[...]
    ls = jnp.minimum(x, 0.0) - jnp.log(1.0 + jnp.exp(-jnp.abs(x)))
    r = lax.broadcasted_iota(jnp.int32, (LANE, LANE), 0)
    c = lax.broadcasted_iota(jnp.int32, (LANE, LANE), 1)
    tri = jnp.where(r >= c, 1.0, 0.0).astype(jnp.bfloat16)
    hi = ls.astype(jnp.bfloat16)
    r1 = ls - hi.astype(jnp.float32)
    mid = r1.astype(jnp.bfloat16)
    lo = (r1 - mid.astype(jnp.float32)).astype(jnp.bfloat16)
    carry = jnp.zeros((1, LANE), jnp.float32)
    for j in range(S // LANE):
        rows = slice(j * LANE, (j + 1) * LANE)
        blk = (jnp.dot(tri, hi[rows], preferred_element_type=jnp.float32)
               + jnp.dot(tri, mid[rows], preferred_element_type=jnp.float32)
               + jnp.dot(tri, lo[rows], preferred_element_type=jnp.float32)) + carry
        o_ref[0, j * LANE:(j + 1) * LANE, :] = blk
        carry = blk[LANE - 1:LANE, :]


def _forget_cumsum(small, b_pad):
    Bn, S, _ = small.shape
    return pl.pallas_call(
        _cumsum_kernel,
        grid=(Bn,),
        in_specs=[pl.BlockSpec((1, S, LANE), lambda b: (b, 0, BLK_SMALL)),
                  pl.BlockSpec((1, LANE), lambda b: (0, 0))],
        out_specs=pl.BlockSpec((1, S, LANE), lambda b: (b, 0, 0)),
        out_shape=jax.ShapeDtypeStruct((Bn, S, LANE), jnp.float32),
        name="forget_cumsum",
        compiler_params=_cparams(("parallel",)),
    )(small, b_pad)


N_KEY_BITS = 32


def _select_block(iq_ref, ik_ref, sm_ref, h_ref, wg_ref, bg_ref, mask_ref, z_ref, key_ref, thr_ref,
                  *, k_top, blk, n_chunks):
    c = pl.program_id(2)
    tq = iq_ref.shape[1]
    S = ik_ref.shape[1]
    W = (blk + 1) * tq
    kf = jnp.float32(k_top)
    causal = (lax.broadcasted_iota(jnp.int32, (tq, W), 1)
              <= blk * tq + lax.broadcasted_iota(jnp.int32, (tq, W), 0))

    @pl.when(c == 0)
    def _():
        iq = iq_ref[0]
        ik = ik_ref[0, :W, :IDX_DIM]
        iw = sm_ref[0][:, COL_IW:COL_IW + IDX_HEADS] * IDX_SCALE
        score = jnp.zeros((tq, W), jnp.float32)
        for h in range(IDX_HEADS):
            d = lax.dot_general(iq[:, h * IDX_DIM:(h + 1) * IDX_DIM], ik,
                                (((1,), (1,)), ((), ())), preferred_element_type=jnp.float32)
            score = score + jnp.where(d > 0.0, d * iw[:, h:h + 1], 0.0)
        score = jnp.where(causal, score, NEG)
        bits = lax.bitcast_convert_type(score, jnp.int32)
        key_ref[:, :W] = bits ^ ((bits >> 31) & 0x7FFFFFFF)
        thr_ref[...] = jnp.full(thr_ref.shape, INT_MIN, jnp.int32)

    per_step = N_KEY_BITS // n_chunks
    tz = z_ref.shape[1] // per_step
    thr = thr_ref[...]
    for j in range(per_step):
        zs = slice(j * tz, (j + 1) * tz)
        z = jnp.dot(h_ref[...], wg_ref[:, zs], preferred_element_type=jnp.float32) + bg_ref[:, zs]
        p = c * per_step + j
        cand = jnp.where(p == 0, 0, thr | jnp.left_shift(jnp.int32(1), N_KEY_BITS - 1 - p))
        cnt = jnp.sum(jnp.where(key_ref[:, :W] >= cand, 1.0, 0.0), axis=1, keepdims=True)
        thr = jnp.where(cnt >= kf, cand, thr)
        z_ref[:, zs] = z.astype(z_ref.dtype)
    thr_ref[...] = thr

    @pl.when(c == n_chunks - 1)
    def _():
        _select_write(mask_ref, key_ref, thr_ref[...], causal, kf, blk)


def _select_write(mask_ref, key_ref, thr, causal, kf, blk):
    tq, S = mask_ref.shape[1], mask_ref.shape[2]
    W = (blk + 1) * tq
    key = key_ref[:, :W]
    above = key > thr
    need = kf - jnp.sum(jnp.where(above, 1.0, 0.0), axis=1, keepdims=True)
    upper = jnp.where(lax.broadcasted_iota(jnp.int32, (tq, tq), 0)
                      <= lax.broadcasted_iota(jnp.int32, (tq, tq), 1), 1.0, 0.0).astype(jnp.bfloat16)
    seen = jnp.zeros((tq, 1), jnp.float32)
    for c in range(blk + 1):
        cs = slice(c * tq, (c + 1) * tq)
        tie = key[:, cs] == thr
        incl = seen + jnp.dot(jnp.where(tie, 1.0, 0.0).astype(jnp.bfloat16), upper,
                              preferred_element_type=jnp.float32)
        sel = above[:, cs] | (tie & (incl <= need))
        if c == blk:
            sel = sel & causal[:, cs]
        mask_ref[0, :, cs] = jnp.where(sel, 1.0, 0.0).astype(mask_ref.dtype)
        seen = incl[:, tq - 1:tq]
    if W < S:
        mask_ref[0, :, W:] = jnp.zeros((tq, S - W), mask_ref.dtype)


def _select_kernel(*refs, k_top, n_chunks):
    iq_ref, ik_ref = refs[0], refs[1]
    i = pl.program_id(1)
    for blk in range(ik_ref.shape[1] // iq_ref.shape[1]):
        @pl.when(i == blk)
        def _(blk=blk):
            _select_block(*refs, k_top=k_top, blk=blk, n_chunks=n_chunks)


def _select_and_gates(main, f32grp, h, w_gate, b_gate, l, k_top, tq):
    Bn, S, _ = main.shape
    T, D = h.shape
    N = w_gate.shape[2]
    nq = S // tq
    tc = _tile(N, 2048)
    n_chunks = N // tc
    assert N_KEY_BITS % n_chunks == 0
    return pl.pallas_call(
        functools.partial(_select_kernel, k_top=k_top, n_chunks=n_chunks),
        grid=(Bn, nq, n_chunks),
        in_specs=[pl.BlockSpec((1, tq, 2 * LANE), lambda b, i, c: (b, i, BLK_IQ // 2)),
                  pl.BlockSpec((1, S, LANE), lambda b, i, c: (b, 0, BLK_IK)),
                  pl.BlockSpec((1, tq, LANE), lambda b, i, c: (b, i, BLK_SMALL)),
                  pl.BlockSpec((tq, D), lambda b, i, c: (b * nq + i, 0)),
                  pl.BlockSpec((None, D, tc), lambda b, i, c: (l, 0, c)),
                  pl.BlockSpec((None, 1, tc), lambda b, i, c: (l, 0, c))],
        out_specs=[pl.BlockSpec((1, tq, S), lambda b, i, c: (b, i, 0)),
                   pl.BlockSpec((tq, tc), lambda b, i, c: (b * nq + i, c))],
        out_shape=[jax.ShapeDtypeStruct((Bn, S, S), jnp.bfloat16),
                   jax.ShapeDtypeStruct((T, N), jnp.bfloat16)],
        scratch_shapes=[pltpu.VMEM((tq, S), jnp.int32), pltpu.VMEM((tq, 1), jnp.int32)],
        name="dsa_select_gates",
        compiler_params=_cparams(("parallel", "parallel", "arbitrary")),
    )(main, main, f32grp, h, w_gate, b_gate)


def _t5_thresholds():
    max_exact = N_BUCKETS // 2
    rel = np.arange(MAX_DIST)
    nf = np.maximum(rel, 1).astype(np.float64)
    large = max_exact + (np.log(nf / max_exact) / math.log(MAX_DIST / max_exact)
                         * (N_BUCKETS - max_exact)).astype(np.int64)
    bucket = np.where(rel < max_exact, rel, np.minimum(large, N_BUCKETS - 1))
    out = []
    for b in range(1, N_BUCKETS):
        hit = np.nonzero(bucket >= b)[0]
        out.append(int(hit[0]) if hit.size else MAX_DIST)
    return out


def _attn_kernel(*refs, mode, head_off, lam_init):
    it = iter(refs)
    q_ref, k_ref, v_ref = next(it), next(it), next(it)
    if mode == "dsa":
        mask_ref = next(it)
    if mode in ("dsa", "diff"):
        tab_ref = next(it)
    if mode == "diff":
        lam_ref, gs_ref = next(it), next(it)
    if mode == "fox":
        fc_ref, fr_ref = next(it), next(it)
    o_ref = next(it)
    m_ref, acc_ref = next(it), next(it)
    if mode in ("dsa", "diff"):
        band_ref = next(it)

    i = pl.program_id(1)
    tq = q_ref.shape[1]
    nmap = 2 if mode == "diff" else 1
    dq = HEAD_DIM // nmap
    scale = dq ** -0.5
    biased = mode in ("dsa", "diff")

    c1 = scale * LOG2E
    if biased:
        @pl.when(i == 0)
        def _():
            r = lax.broadcasted_iota(jnp.int32, (tq, 2 * tq), 0)
            c = lax.broadcasted_iota(jnp.int32, (tq, 2 * tq), 1)
            rel = r - c + tq
            vals = [jnp.full((tq, 2 * tq), tab_ref[0, head_off + h] * LOG2E, jnp.float32)
                    for h in range(N_HEADS)]
            for b, first in enumerate(_t5_thresholds(), start=1):
                hit = rel >= first
                vals = [jnp.where(hit, tab_ref[b, head_off + h] * LOG2E, vals[h])
                        for h in range(N_HEADS)]
            for h in range(N_HEADS):
                band_ref[h] = vals[h]

    m_ref[...] = jnp.full(m_ref.shape, NEG, jnp.float32)
    acc_ref[...] = jnp.zeros(acc_ref.shape, jnp.float32)

    def chunk(off, kind, tk):
        ones = jnp.ones((tk, LANE), jnp.bfloat16)
        if mode == "dsa":
            keep = mask_ref[0, :, pl.ds(off, tk)].astype(jnp.float32) > 0.5
        elif kind == "far":
            keep = None
        else:
            keep = (lax.broadcasted_iota(jnp.int32, (tq, tk), 1) - (tk - tq)
                    <= lax.broadcasted_iota(jnp.int32, (tq, tk), 0))
        units = [(h, mp) for h in range(N_HEADS) for mp in range(nmap)]
        logits = []
        for h, mp in units:
            cols = slice(h * HEAD_DIM + mp * dq, h * HEAD_DIM + (mp + 1) * dq)
            logits.append(lax.dot_general(q_ref[0, :, cols], k_ref[0, pl.ds(off, tk), cols],
                                          (((1,), (1,)), ((), ())),
                                          preferred_element_type=jnp.float32))
        m_olds, m_news = [], []
        for u, (h, mp) in enumerate(units):
            t = logits[u] * c1
            if biased:
                if kind == "far":
                    t = t + tab_ref[N_BUCKETS - 1, head_off + h] * LOG2E
                elif kind == "near":
                    t = t + band_ref[h]
                else:
                    t = t + band_ref[h, :, tq:]
            else:
                t = (t + fc_ref[0, :, COL_FD + h:COL_FD + h + 1] * LOG2E
                     - fr_ref[0, h:h + 1, pl.ds(off, tk)] * LOG2E)
            if keep is not None:
                t = jnp.where(keep, t, NEG)
            logits[u] = t
            m_old = m_ref[u]
            m_olds.append(m_old)
            m_news.append(jnp.maximum(m_old, jnp.max(t, axis=1, keepdims=True)))
        probs = []
        for u in range(len(units)):
            probs.append(jnp.exp2(logits[u] - jnp.tile(m_news[u], (1, tk // LANE))).astype(v_ref.dtype))
            m_ref[u] = m_news[u]
        for u, (h, mp) in enumerate(units):
            vc = jnp.concatenate(
                [v_ref[0, pl.ds(off, tk), h * HEAD_DIM:(h + 1) * HEAD_DIM], ones], axis=1)
            alpha = jnp.exp2(m_olds[u] - m_news[u])
            acc_ref[u] = (jnp.tile(alpha, (1, 2)) * acc_ref[u]
                          + jnp.dot(probs[u], vc, preferred_element_type=jnp.float32))

    n_far = jnp.maximum(i - 1, 0)

    def far_body(c, carry):
        chunk(pl.multiple_of(c * 2 * tq, 2 * tq), "far", 2 * tq)
        return carry

    lax.fori_loop(0, n_far // 2, far_body, 0)

    @pl.when(n_far % 2 == 1)
    def _():
        chunk(pl.multiple_of((n_far - 1) * tq, tq), "far", tq)

    @pl.when(i >= 1)
    def _():
        chunk(pl.multiple_of((i - 1) * tq, tq), "near", 2 * tq)

    @pl.when(i == 0)
    def _():
        chunk(0, "diag", tq)

    if mode == "diff":
        lq = lam_ref[...]
        lam_val = (jnp.exp(jnp.sum(lq[0:1] * lq[1:2], axis=1, keepdims=True))
                   - jnp.exp(jnp.sum(lq[2:3] * lq[3:4], axis=1, keepdims=True)) + lam_init)
    for h in range(N_HEADS):
        a = acc_ref[h * nmap]
        o = a[:, :HEAD_DIM] / a[:, HEAD_DIM:]
        if mode == "diff":
            a1 = acc_ref[h * nmap + 1]
            o = o - lam_val * (a1[:, :HEAD_DIM] / a1[:, HEAD_DIM:])
            o = o * lax.rsqrt(jnp.mean(o * o, axis=-1, keepdims=True) + EPS) * gs_ref[...]
            o = o * (1.0 - lam_init)
        o_ref[0, :, h * HEAD_DIM:(h + 1) * HEAD_DIM] = o.astype(o_ref.dtype)


def _attention(main, mode, *, mask=None, tab=None, lam=None, g_subln=None,
               fcol=None, frow=None, lam_init=0.0):
    Bn, S, _ = main.shape
    blk_q, blk_k, blk_v, head_off, tq_pref = {
        "dsa": (BLK_QA, BLK_KA, BLK_VA, 0, 256),
        "diff": (BLK_QB, BLK_KB, BLK_VB, N_HEADS, 512),
        "fox": (BLK_QD, BLK_KD, BLK_VD, 0, 512),
    }[mode]
    tq = _tile(S, tq_pref)
    nmap = 2 if mode == "diff" else 1
    W = N_HEADS * HEAD_DIM
    args = [main, main, main]
    specs = [pl.BlockSpec((1, tq, W), lambda b, i: (b, i, blk_q // N_HEADS)),
             pl.BlockSpec((1, S, W), lambda b, i: (b, 0, blk_k // N_HEADS)),
             pl.BlockSpec((1, S, W), lambda b, i: (b, 0, blk_v // N_HEADS))]
    scratch = [pltpu.VMEM((N_HEADS * nmap, tq, LANE), jnp.float32),
               pltpu.VMEM((N_HEADS * nmap, tq, 2 * HEAD_DIM), jnp.float32)]
    if mode == "dsa":
        args.append(mask)
        specs.append(pl.BlockSpec((1, tq, S), lambda b, i: (b, i, 0)))
    if mode in ("dsa", "diff"):
        args.append(tab)
        specs.append(pl.BlockSpec(memory_space=pltpu.SMEM))
        scratch.append(pltpu.VMEM((N_HEADS, tq, 2 * tq), jnp.float32))
    if mode == "diff":
        args += [lam, g_subln.reshape(1, HEAD_DIM)]
        specs += [pl.BlockSpec((4, DIFF_DIM), lambda b, i: (0, 0)),
                  pl.BlockSpec((1, HEAD_DIM), lambda b, i: (0, 0))]
    if mode == "fox":
        args += [fcol, frow]
        specs += [pl.BlockSpec((1, tq, LANE), lambda b, i: (b, i, 0)),
                  pl.BlockSpec((1, N_HEADS, S), lambda b, i: (b, 0, 0))]
    return pl.pallas_call(
        functools.partial(_attn_kernel, mode=mode, head_off=head_off, lam_init=lam_init),
        grid=(Bn, S // tq),
        in_specs=specs,
        out_specs=pl.BlockSpec((1, tq, W), lambda b, i: (b, i, 0)),
        out_shape=jax.ShapeDtypeStruct((Bn, S, MIX_W), jnp.bfloat16),
        scratch_shapes=scratch,
        name="attn_" + mode,
        compiler_params=_cparams(("parallel", "arbitrary")),
    )(*args)


def _conv_kernel(bg_ref, cg_ref, xin_ref, w_ref, o_ref):
    u = cg_ref[0] * xin_ref[0]
    row = lax.broadcasted_iota(jnp.int32, u.shape, 0)
    w = w_ref[...]
    y = u * w[CONV_K - 1:CONV_K, :]
    for d in range(1, CONV_K):
        shifted = jnp.where(row >= d, pltpu.roll(u, d, 0), 0.0)
        y = y + shifted * w[CONV_K - 1 - d:CONV_K - d, :]
    o_ref[0] = (bg_ref[0] * y).astype(o_ref.dtype)


def _short_conv(f32grp, conv_w):
    Bn, S, _ = f32grp.shape
    tc = LANE
    per = CONV_CH // tc
    return pl.pallas_call(
        _conv_kernel,
        grid=(Bn, per),
        in_specs=[pl.BlockSpec((1, S, tc), lambda b, j: (b, 0, j)),
                  pl.BlockSpec((1, S, tc), lambda b, j: (b, 0, per + j)),
                  pl.BlockSpec((1, S, tc), lambda b, j: (b, 0, 2 * per + j)),
                  pl.BlockSpec((CONV_K, tc), lambda b, j: (0, j))],
        out_specs=pl.BlockSpec((1, S, tc), lambda b, j: (b, 0, j)),
        out_shape=jax.ShapeDtypeStruct((Bn, S, CONV_CH), jnp.bfloat16),
        name="short_conv",
        compiler_params=_cparams(("parallel", "parallel")),
    )(f32grp, f32grp, f32grp, conv_w)


def _mix_kernel(*refs):
    z_refs = refs[0:N_BRANCH]
    y_refs = refs[N_BRANCH:2 * N_BRANCH]
    wb_refs = refs[2 * N_BRANCH:3 * N_BRANCH]
    o_ref = refs[3 * N_BRANCH]
    merged = None
    for g in range(N_BRANCH):
        gate = jax.nn.sigmoid(z_refs[g][...].astype(jnp.float32))
        proj = jnp.dot(y_refs[g][...], wb_refs[g][...], preferred_element_type=jnp.float32)
        merged = gate * proj if merged is None else merged + gate * proj
    o_ref[...] = merged.astype(o_ref.dtype)


def _mix(z, ys, w_branch, l):
    T = z.shape[0]
    D = w_branch.shape[3]
    tm = _tile(T, 1024)
    tn = _tile(D, 1024)
    per = D // tn
    specs = [pl.BlockSpec((tm, tn), lambda i, n, g=g: (i, g * per + n)) for g in range(N_BRANCH)]
    specs += [pl.BlockSpec((tm, MIX_W), lambda i, n: (i, 0)) for _ in range(N_BRANCH)]
    specs += [pl.BlockSpec((None, None, MIX_W, tn), lambda i, n, g=g: (l, g, 0, n))
              for g in range(N_BRANCH)]
    return pl.pallas_call(
        _mix_kernel,
        grid=(T // tm, per),
        in_specs=specs,
        out_specs=pl.BlockSpec((tm, tn), lambda i, n: (i, n)),
        out_shape=jax.ShapeDtypeStruct((T, D), jnp.bfloat16),
        name="gated_mix",
        compiler_params=_cparams(("parallel", "parallel")),
    )(*([z] * N_BRANCH), *ys, *([w_branch] * N_BRANCH))


def _ffn_kernel(h_ref, w1_ref, w2_ref, o_ref, u_ref, *, nf):
    f = pl.program_id(1)

    def first(slot):
        u = jnp.dot(h_ref[...], w1_ref[...], preferred_element_type=jnp.float32)
        u_ref[slot] = jnp.square(jnp.maximum(u, 0.0)).astype(u_ref.dtype)

    def second(slot):
        return jnp.dot(u_ref[slot], w2_ref[...], preferred_element_type=jnp.float32)

    @pl.when(f == 0)
    def _():
        o_ref[...] = jnp.zeros(o_ref.shape, o_ref.dtype)
        first(0)

    for parity in range(2):
        @pl.when((f > 0) & (f < nf) & (f % 2 == parity))
        def _(parity=parity):
            part = second(1 - parity)
            first(parity)
            o_ref[...] += part

    @pl.when(f == nf)
    def _():
        o_ref[...] += second((nf - 1) % 2)


def _ffn(h, w1, w2, l):
    T, D = h.shape
    F = w1.shape[2]
    tm = _tile(T, 1024)
    tf = _tile(F, 1024)
    nf = F // tf
    row = pl.BlockSpec((tm, D), lambda i, f: (i, 0))
    return pl.pallas_call(
        functools.partial(_ffn_kernel, nf=nf),
        grid=(T // tm, nf + 1),
        in_specs=[row,
                  pl.BlockSpec((None, D, tf), lambda i, f: (l, 0, jnp.minimum(f, nf - 1))),
                  pl.BlockSpec((None, tf, D), lambda i, f: (l, jnp.maximum(f - 1, 0), 0))],
        out_specs=row,
        out_shape=jax.ShapeDtypeStruct((T, D), jnp.float32),
        scratch_shapes=[pltpu.VMEM((2, tm, tf), jnp.bfloat16)],
        name="ffn",
        compiler_params=_cparams(("parallel", "arbitrary")),
    )(h, w1, w2)


def _split_in_proj(w_in):
    W = N_HEADS * HEAD_DIM
    sizes = (3 * W,
             IDX_HEADS * IDX_DIM + IDX_DIM,
             IDX_HEADS,
             3 * W,
             3 * CONV_CH,
             3 * W,
             N_HEADS)
    offs = np.concatenate([[0], np.cumsum(sizes)])
    a, idx, iw, b, cv, d, fd = [w_in[:, :, offs[j]:offs[j + 1]] for j in range(7)]
    L, D, _ = w_in.shape
    zeros = lambda n: jnp.zeros((L, D, n), w_in.dtype)
    main = jnp.concatenate([a, b, d, idx, zeros(N_MAIN - 9 * W - sizes[1])], axis=-1)
    f32grp = jnp.concatenate([cv, iw, fd, zeros(LANE - IDX_HEADS - N_HEADS)], axis=-1)
    return main.astype(jnp.bfloat16), f32grp.astype(jnp.bfloat16)


def kernel(x, c, w_ada, b_ada, g_norm1, w_in, b_f, conv_w, lam, g_subln, rel_bias,
           w_gate, b_gate, w_branch, w_o, g_norm2, w_ff1, w_ff2, g_final):
    Bn, S, D = x.shape
    L = w_ada.shape[0]
    T = Bn * S
    k_top = min(TOPK_MAX, S // 4)
    bf = jnp.bfloat16

    w_main, w_f32 = _split_in_proj(w_in)
    w_gate_b, w_branch_b, w_o_b = w_gate.astype(bf), w_branch.astype(bf), w_o.astype(bf)
    w_ff1_b, w_ff2_b = w_ff1.astype(bf), w_ff2.astype(bf)

    mod = _modulation(c, w_ada, b_ada).reshape(L, Bn, 6, 1, D)
    xt = x.reshape(T, D)
    delta, gate = None, None
    for l in range(L):
        sh1, sc1, gt1, sh2, sc2, gt2 = [mod[l, :, j] for j in range(6)]
        x_new, h = _norm(xt, g_norm1[l], sc1, sh1, seq=S, delta=delta, gate=gate)
        xt = xt if x_new is None else x_new
        main = _mm(h, w_main, l, bf, 2560).reshape(Bn, S, N_MAIN)
        f32grp = _mm(h, w_f32, l, jnp.float32, N_F32).reshape(Bn, S, N_F32)

        b_pad = jnp.zeros((1, LANE), jnp.float32).at[0, COL_FD:COL_FD + N_HEADS].set(b_f[l])
        fcol = _forget_cumsum(f32grp, b_pad)
        frow = jnp.transpose(fcol[:, :, COL_FD:COL_FD + N_HEADS], (0, 2, 1))

        mask, z = _select_and_gates(main, f32grp, h, w_gate_b, b_gate.reshape(L, 1, N_BRANCH * D),
                                    l, k_top, _tile(S, 512))
        lam_init = 0.8 - 0.6 * math.exp(-0.3 * l)
        y_a = _attention(main, "dsa", mask=mask, tab=rel_bias)
        y_b = _attention(main, "diff", tab=rel_bias, lam=lam[l], g_subln=g_subln[l],
                         lam_init=lam_init)
        y_c = _short_conv(f32grp, conv_w[l])
        y_d = _attention(main, "fox", fcol=fcol, frow=frow)
        ys = [y.reshape(T, MIX_W) for y in (y_a, y_b, y_c, y_d)]

        merged = _mix(z, ys, w_branch_b, l)
        xt, h2 = _mm_norm(merged, w_o_b, l, xt, gt1, g_norm2[l], sc2, sh2, seq=S)
        delta = _ffn(h2, w_ff1_b, w_ff2_b, l)
        gate = gt2
    out = _norm(xt, g_final, seq=S, delta=delta, gate=gate)
    return out.reshape(Bn, S, D)
```

```python
import functools
import math

import numpy as np
import jax
import jax.numpy as jnp
from jax import lax
from jax.experimental import pallas as pl
from jax.experimental.pallas import tpu as pltpu

HEAD_DIM = 128
N_HEADS = 4
DIFF_DIM = HEAD_DIM // 2
CONV_CH = 512
CONV_K = 3
IDX_HEADS = 4
IDX_DIM = 64
TOPK_MAX = 256
MIX_W = 512
N_BRANCH = 4
N_BUCKETS = 32
MAX_DIST = 128
EPS = 1e-6
NEG = -1e30
IDX_SCALE = (IDX_DIM ** -0.5) * (IDX_HEADS ** -0.5)
INT_MIN = -(2 ** 31)
LOG2E = 1.0 / math.log(2.0)

LANE = 128
VMEM_LIMIT = 56 * 1024 * 1024

BLK_QA, BLK_KA, BLK_VA = 0, 4, 8
BLK_QB, BLK_KB, BLK_VB = 12, 16, 20
BLK_QD, BLK_KD, BLK_VD = 24, 28, 32
BLK_IQ = 36
BLK_IK = 38
N_MAIN = 40 * LANE
N_F32 = 3 * CONV_CH + LANE
BLK_SMALL = 3 * CONV_CH // LANE
COL_IW, COL_FD = 0, 4


def _cparams(sem):
    return pltpu.CompilerParams(dimension_semantics=sem, vmem_limit_bytes=VMEM_LIMIT)


def _tile(n, pref):
    t = min(n, pref)
    while n % t:
        t -= 1
    return t


def _mod_kernel(c_ref, w_ref, b_ref, o_ref):
    c = c_ref[...]
    ca = (c * jax.nn.sigmoid(c)).astype(jnp.bfloat16)
    w = w_ref[0].astype(jnp.bfloat16)
    o_ref[0] = jnp.dot(ca, w, preferred_element_type=jnp.float32) + b_ref[0]


def _modulation(c, w_ada, b_ada):
    L, D, N = w_ada.shape
    Bn = c.shape[0]
    tn = _tile(N, 1024)
    return pl.pallas_call(
        _mod_kernel,
        grid=(L, N // tn),
        in_specs=[
            pl.BlockSpec((Bn, D), lambda l, j: (0, 0)),
            pl.BlockSpec((1, D, tn), lambda l, j: (l, 0, j)),
            pl.BlockSpec((1, 1, tn), lambda l, j: (l, 0, j)),
        ],
        out_specs=pl.BlockSpec((1, Bn, tn), lambda l, j: (l, 0, j)),
        out_shape=jax.ShapeDtypeStruct((L, Bn, N), jnp.float32),
        name="ada_mod",
        compiler_params=_cparams(("parallel", "parallel")),
    )(c, w_ada, b_ada.reshape(L, 1, N))


def _norm_kernel(*refs, has_delta, closing):
    it = iter(refs)
    x_ref = next(it)
    d_ref = next(it) if has_delta else None
    gate_ref = next(it) if has_delta else None
    g_ref = next(it)
    sc_ref, sh_ref = (None, None) if closing else (next(it), next(it))
    xo_ref = next(it) if has_delta and not closing else None
    h_ref = next(it)

    x = x_ref[...]
    if has_delta:
        x = x + gate_ref[0] * d_ref[...]
    if xo_ref is not None:
        xo_ref[...] = x
    if closing:
        y = x * lax.rsqrt(jnp.mean(x * x, axis=-1, keepdims=True) + EPS) * g_ref[...]
    else:
        y = _rms_mod(x, g_ref[...], sc_ref[0], sh_ref[0])
    h_ref[...] = y.astype(h_ref.dtype)


def _norm(x, g, sc=None, sh=None, *, seq, delta=None, gate=None):
    T, D = x.shape
    tm = _tile(seq, 512)
    per_seq = seq // tm
    has_delta = delta is not None
    closing = sc is None
    row = pl.BlockSpec((tm, D), lambda i: (i, 0))
    per_b = pl.BlockSpec((1, 1, D), lambda i: (i // per_seq, 0, 0))
    args, specs = [x], [row]
    if has_delta:
        args += [delta, gate]
        specs += [row, per_b]
    args.append(g.reshape(1, D))
    specs.append(pl.BlockSpec((1, D), lambda i: (0, 0)))
    if not closing:
        args += [sc, sh]
        specs += [per_b, per_b]
    out_shape, out_specs = [], []
    if has_delta and not closing:
        out_shape.append(jax.ShapeDtypeStruct((T, D), jnp.float32))
        out_specs.append(row)
    out_shape.append(jax.ShapeDtypeStruct((T, D), jnp.float32 if closing else jnp.bfloat16))
    out_specs.append(row)
    res = pl.pallas_call(
        functools.partial(_norm_kernel, has_delta=has_delta, closing=closing),
        grid=(T // tm,),
        in_specs=specs,
        out_specs=out_specs,
        out_shape=out_shape,
        name="norm",
        compiler_params=_cparams(("parallel",)),
    )(*args)
    if closing:
        return res[0]
    if has_delta:
        return res[0], res[1]
    return None, res[0]


def _rms_mod(x, g, sc, sh):
    y = x * lax.rsqrt(jnp.mean(x * x, axis=-1, keepdims=True) + EPS) * g
    return y * (1.0 + sc) + sh


def _mm_norm_kernel(a_ref, w_ref, x_ref, gate_ref, g_ref, sc_ref, sh_ref, xo_ref, h_ref):
    acc = jnp.dot(a_ref[...], w_ref[...], preferred_element_type=jnp.float32)
    x = x_ref[...] + gate_ref[0] * acc
    xo_ref[...] = x
    h_ref[...] = _rms_mod(x, g_ref[...], sc_ref[0], sh_ref[0]).astype(h_ref.dtype)


def _mm_norm(a, w, l, x, gate, g, sc, sh, *, seq):
    T, D = x.shape
    K = a.shape[1]
    tm = _tile(seq, 512)
    per_seq = seq // tm
    row = pl.BlockSpec((tm, D), lambda i: (i, 0))
    per_b = pl.BlockSpec((1, 1, D), lambda i: (i // per_seq, 0, 0))
    return pl.pallas_call(
        _mm_norm_kernel,
        grid=(T // tm,),
        in_specs=[pl.BlockSpec((tm, K), lambda i: (i, 0)),
                  pl.BlockSpec((None, K, D), lambda i: (l, 0, 0)),
                  row, per_b, pl.BlockSpec((1, D), lambda i: (0, 0)), per_b, per_b],
        out_specs=[row, row],
        out_shape=[jax.ShapeDtypeStruct((T, D), jnp.float32),
                   jax.ShapeDtypeStruct((T, D), jnp.bfloat16)],
        name="outproj_norm",
        compiler_params=_cparams(("parallel",)),
    )(a, w, x, gate, g.reshape(1, D), sc, sh)


def _mm_kernel(a_ref, w_ref, o_ref):
    o_ref[...] = jnp.dot(a_ref[...], w_ref[...],
                         preferred_element_type=jnp.float32).astype(o_ref.dtype)


def _mm(a, w, l, out_dtype, tn_pref):
    M, K = a.shape
    N = w.shape[2]
    tm = _tile(M, 1024)
    tn = _tile(N, tn_pref)
    return pl.pallas_call(
        _mm_kernel,
        grid=(M // tm, N // tn),
        in_specs=[pl.BlockSpec((tm, K), lambda i, j: (i, 0)),
                  pl.BlockSpec((None, K, tn), lambda i, j: (l, 0, j))],
        out_specs=pl.BlockSpec((tm, tn), lambda i, j: (i, j)),
        out_shape=jax.ShapeDtypeStruct((M, N), out_dtype),
        name="matmul",
        compiler_params=_cparams(("parallel", "parallel")),
    )(a, w)


def _cumsum_kernel(s_ref, b_ref, o_ref):
    S = s_ref.shape[1]
    x = s_ref[0] + b_ref[...]
    ls = jnp.minimum(x, 0.0) - jnp.log(1.0 + jnp.exp(-jnp.abs(x)))
    r = lax.broadcasted_iota(jnp.int32, (LANE, LANE), 0)
    c = lax.broadcasted_iota(jnp.int32, (LANE, LANE), 1)
    tri = jnp.where(r >= c, 1.0, 0.0).astype(jnp.bfloat16)
    hi = ls.astype(jnp.bfloat16)
    r1 = ls - hi.astype(jnp.float32)
    mid = r1.astype(jnp.bfloat16)
    lo = (r1 - mid.astype(jnp.float32)).astype(jnp.bfloat16)
    carry = jnp.zeros((1, LANE), jnp.float32)
    for j in range(S // LANE):
        rows = slice(j * LANE, (j + 1) * LANE)
        blk = (jnp.dot(tri, hi[rows], preferred_element_type=jnp.float32)
               + jnp.dot(tri, mid[rows], preferred_element_type=jnp.float32)
               + jnp.dot(tri, lo[rows], preferred_element_type=jnp.float32)) + carry
        o_ref[0, j * LANE:(j + 1) * LANE, :] = blk
        carry = blk[LANE - 1:LANE, :]


def _forget_cumsum(small, b_pad):
    Bn, S, _ = small.shape
    return pl.pallas_call(
        _cumsum_kernel,
        grid=(Bn,),
        in_specs=[pl.BlockSpec((1, S, LANE), lambda b: (b, 0, BLK_SMALL)),
                  pl.BlockSpec((1, LANE), lambda b: (0, 0))],
        out_specs=pl.BlockSpec((1, S, LANE), lambda b: (b, 0, 0)),
        out_shape=jax.ShapeDtypeStruct((Bn, S, LANE), jnp.float32),
        name="forget_cumsum",
        compiler_params=_cparams(("parallel",)),
    )(small, b_pad)


N_KEY_BITS = 32


def _select_block(iq_ref, ik_ref, sm_ref, h_ref, wg_ref, bg_ref, mask_ref, z_ref, key_ref, thr_ref,
                  *, k_top, blk, n_chunks):
    c = pl.program_id(2)
    tq = iq_ref.shape[1]
    S = ik_ref.shape[1]
    W = (blk + 1) * tq
    kf = jnp.float32(k_top)
    causal = (lax.broadcasted_iota(jnp.int32, (tq, W), 1)
              <= blk * tq + lax.broadcasted_iota(jnp.int32, (tq, W), 0))

    def scores():
        iq = iq_ref[0]
        ik = ik_ref[0, :W, :IDX_DIM]
        iw = sm_ref[0][:, COL_IW:COL_IW + IDX_HEADS] * IDX_SCALE
        score = jnp.zeros((tq, W), jnp.float32)
        for h in range(IDX_HEADS):
            d = lax.dot_general(iq[:, h * IDX_DIM:(h + 1) * IDX_DIM], ik,
                                (((1,), (1,)), ((), ())), preferred_element_type=jnp.float32)
            score = score + jnp.where(d > 0.0, d * iw[:, h:h + 1], 0.0)
        score = jnp.where(causal, score, NEG)
        bits = lax.bitcast_convert_type(score, jnp.int32)
        key_ref[:, :W] = bits ^ ((bits >> 31) & 0x7FFFFFFF)
        thr_ref[...] = jnp.full(thr_ref.shape, INT_MIN, jnp.int32)

    def passes_and_gates():
        per_step = N_KEY_BITS // n_chunks
        tz = z_ref.shape[1] // per_step
        thr = thr_ref[...]
        for j in range(per_step):
            zs = slice(j * tz, (j + 1) * tz)
            z = (jnp.dot(h_ref[...], wg_ref[:, zs], preferred_element_type=jnp.float32)
                 + bg_ref[:, zs])
            p = c * per_step + j
            cand = jnp.where(p == 0, 0, thr | jnp.left_shift(jnp.int32(1), N_KEY_BITS - 1 - p))
            cnt = jnp.sum(jnp.where(key_ref[:, :W] >= cand, 1.0, 0.0), axis=1, keepdims=True)
            thr = jnp.where(cnt >= kf, cand, thr)
            z_ref[:, zs] = z.astype(z_ref.dtype)
        thr_ref[...] = thr

    def write():
        _select_write(mask_ref, key_ref, thr_ref[...], causal, kf, blk)

    last = n_chunks - 1
    if last == 0:
        scores()
        passes_and_gates()
        write()
        return

    @pl.when(c == 0)
    def _():
        scores()
        passes_and_gates()

    @pl.when((c > 0) & (c < last))
    def _():
        passes_and_gates()

    @pl.when(c == last)
    def _():
        passes_and_gates()
        write()


def _select_write(mask_ref, key_ref, thr, causal, kf, blk):
    tq, S = mask_ref.shape[1], mask_ref.shape[2]
    W = (blk + 1) * tq
    key = key_ref[:, :W]
    above = key > thr
    need = kf - jnp.sum(jnp.where(above, 1.0, 0.0), axis=1, keepdims=True)
    upper = jnp.where(lax.broadcasted_iota(jnp.int32, (tq, tq), 0)
                      <= lax.broadcasted_iota(jnp.int32, (tq, tq), 1), 1.0, 0.0).astype(jnp.bfloat16)
    seen = jnp.zeros((tq, 1), jnp.float32)
    for c in range(blk + 1):
        cs = slice(c * tq, (c + 1) * tq)
        tie = key[:, cs] == thr
        incl = seen + jnp.dot(jnp.where(tie, 1.0, 0.0).astype(jnp.bfloat16), upper,
                              preferred_element_type=jnp.float32)
        sel = above[:, cs] | (tie & (incl <= need))
        if c == blk:
            sel = sel & causal[:, cs]
        mask_ref[0, :, cs] = jnp.where(sel, 1.0, 0.0).astype(mask_ref.dtype)
        seen = incl[:, tq - 1:tq]
    if W < S:
        mask_ref[0, :, W:] = jnp.zeros((tq, S - W), mask_ref.dtype)


def _select_kernel(*refs, k_top, n_chunks):
    iq_ref, ik_ref = refs[0], refs[1]
    i = pl.program_id(1)
    for blk in range(ik_ref.shape[1] // iq_ref.shape[1]):
        @pl.when(i == blk)
        def _(blk=blk):
            _select_block(*refs, k_top=k_top, blk=blk, n_chunks=n_chunks)


def _select_and_gates(main, f32grp, h, w_gate, b_gate, l, k_top, tq):
    Bn, S, _ = main.shape
    T, D = h.shape
    N = w_gate.shape[2]
    nq = S // tq
    tc = _tile(N, 2048)
    n_chunks = N // tc
    assert N_KEY_BITS % n_chunks == 0
    return pl.pallas_call(
        functools.partial(_select_kernel, k_top=k_top, n_chunks=n_chunks),
        grid=(Bn, nq, n_chunks),
        in_specs=[pl.BlockSpec((1, tq, 2 * LANE), lambda b, i, c: (b, i, BLK_IQ // 2)),
                  pl.BlockSpec((1, S, LANE), lambda b, i, c: (b, 0, BLK_IK)),
                  pl.BlockSpec((1, tq, LANE), lambda b, i, c: (b, i, BLK_SMALL)),
                  pl.BlockSpec((tq, D), lambda b, i, c: (b * nq + i, 0)),
                  pl.BlockSpec((None, D, tc), lambda b, i, c: (l, 0, c)),
                  pl.BlockSpec((None, 1, tc), lambda b, i, c: (l, 0, c))],
        out_specs=[pl.BlockSpec((1, tq, S), lambda b, i, c: (b, i, 0)),
                   pl.BlockSpec((tq, tc), lambda b, i, c: (b * nq + i, c))],
        out_shape=[jax.ShapeDtypeStruct((Bn, S, S), jnp.bfloat16),
                   jax.ShapeDtypeStruct((T, N), jnp.bfloat16)],
        scratch_shapes=[pltpu.VMEM((tq, S), jnp.int32), pltpu.VMEM((tq, 1), jnp.int32)],
        name="dsa_select_gates",
        compiler_params=_cparams(("parallel", "parallel", "arbitrary")),
    )(main, main, f32grp, h, w_gate, b_gate)


def _t5_thresholds():
    max_exact = N_BUCKETS // 2
    rel = np.arange(MAX_DIST)
    nf = np.maximum(rel, 1).astype(np.float64)
    large = max_exact + (np.log(nf / max_exact) / math.log(MAX_DIST / max_exact)
                         * (N_BUCKETS - max_exact)).astype(np.int64)
    bucket = np.where(rel < max_exact, rel, np.minimum(large, N_BUCKETS - 1))
    out = []
    for b in range(1, N_BUCKETS):
        hit = np.nonzero(bucket >= b)[0]
        out.append(int(hit[0]) if hit.size else MAX_DIST)
    return out


def _attn_kernel(*refs, mode, head_off, lam_init):
    it = iter(refs)
    q_ref, k_ref, v_ref = next(it), next(it), next(it)
    if mode == "dsa":
        mask_ref = next(it)
    if mode in ("dsa", "diff"):
        tab_ref = next(it)
    if mode == "diff":
        lam_ref, gs_ref = next(it), next(it)
    if mode == "fox":
        fc_ref, fr_ref = next(it), next(it)
    o_ref = next(it)
    m_ref, acc_ref = next(it), next(it)
    if mode in ("dsa", "diff"):
        band_ref = next(it)

    i = pl.program_id(1)
    tq = q_ref.shape[1]
    nmap = 2 if mode == "diff" else 1
    dq = HEAD_DIM // nmap
    scale = dq ** -0.5
    biased = mode in ("dsa", "diff")

    c1 = scale * LOG2E
    if biased:
        @pl.when(i == 0)
        def _():
            r = lax.broadcasted_iota(jnp.int32, (tq, 2 * tq), 0)
            c = lax.broadcasted_iota(jnp.int32, (tq, 2 * tq), 1)
            rel = r - c + tq
            vals = [jnp.full((tq, 2 * tq), tab_ref[0, head_off + h] * LOG2E, jnp.float32)
                    for h in range(N_HEADS)]
            for b, first in enumerate(_t5_thresholds(), start=1):
                hit = rel >= first
                vals = [jnp.where(hit, tab_ref[b, head_off + h] * LOG2E, vals[h])
                        for h in range(N_HEADS)]
            for h in range(N_HEADS):
                band_ref[h] = vals[h]

    m_ref[...] = jnp.full(m_ref.shape, NEG, jnp.float32)
    acc_ref[...] = jnp.zeros(acc_ref.shape, jnp.float32)

    def chunk(off, kind, tk):
        ones = jnp.ones((tk, LANE), jnp.bfloat16)
        if mode == "dsa":
            keep = mask_ref[0, :, pl.ds(off, tk)].astype(jnp.float32) > 0.5
        elif kind == "far":
            keep = None
        else:
            keep = (lax.broadcasted_iota(jnp.int32, (tq, tk), 1) - (tk - tq)
                    <= lax.broadcasted_iota(jnp.int32, (tq, tk), 0))
        units = [(h, mp) for h in range(N_HEADS) for mp in range(nmap)]
        logits = []
        for h, mp in units:
            cols = slice(h * HEAD_DIM + mp * dq, h * HEAD_DIM + (mp + 1) * dq)
            logits.append(lax.dot_general(q_ref[0, :, cols], k_ref[0, pl.ds(off, tk), cols],
                                          (((1,), (1,)), ((), ())),
                                          preferred_element_type=jnp.float32))
        m_olds, m_news = [], []
        for u, (h, mp) in enumerate(units):
            t = logits[u] * c1
            if biased:
                if kind == "far":
                    t = t + tab_ref[N_BUCKETS - 1, head_off + h] * LOG2E
                elif kind == "near":
                    t = t + band_ref[h]
                else:
                    t = t + band_ref[h, :, tq:]
            else:
                t = (t + fc_ref[0, :, COL_FD + h:COL_FD + h + 1] * LOG2E
                     - fr_ref[0, h:h + 1, pl.ds(off, tk)] * LOG2E)
            if keep is not None:
                t = jnp.where(keep, t, NEG)
            logits[u] = t
            m_old = m_ref[u]
            m_olds.append(m_old)
            m_news.append(jnp.maximum(m_old, jnp.max(t, axis=1, keepdims=True)))
        probs = []
        for u in range(len(units)):
            probs.append(jnp.exp2(logits[u] - jnp.tile(m_news[u], (1, tk // LANE))).astype(v_ref.dtype))
            m_ref[u] = m_news[u]
        for u, (h, mp) in enumerate(units):
            vc = jnp.concatenate(
                [v_ref[0, pl.ds(off, tk), h * HEAD_DIM:(h + 1) * HEAD_DIM], ones], axis=1)
            alpha = jnp.exp2(m_olds[u] - m_news[u])
            acc_ref[u] = (jnp.tile(alpha, (1, 2)) * acc_ref[u]
                          + jnp.dot(probs[u], vc, preferred_element_type=jnp.float32))

    n_far = jnp.maximum(i - 1, 0)

    def far_body(c, carry):
        chunk(pl.multiple_of(c * 2 * tq, 2 * tq), "far", 2 * tq)
        return carry

    lax.fori_loop(0, n_far // 2, far_body, 0)

    @pl.when(n_far % 2 == 1)
    def _():
        chunk(pl.multiple_of((n_far - 1) * tq, tq), "far", tq)

    @pl.when(i >= 1)
    def _():
        chunk(pl.multiple_of((i - 1) * tq, tq), "near", 2 * tq)

    @pl.when(i == 0)
    def _():
        chunk(0, "diag", tq)

    if mode == "diff":
        lq = lam_ref[...]
        lam_val = (jnp.exp(jnp.sum(lq[0:1] * lq[1:2], axis=1, keepdims=True))
                   - jnp.exp(jnp.sum(lq[2:3] * lq[3:4], axis=1, keepdims=True)) + lam_init)
    for h in range(N_HEADS):
        a = acc_ref[h * nmap]
        o = a[:, :HEAD_DIM] / a[:, HEAD_DIM:]
        if mode == "diff":
            a1 = acc_ref[h * nmap + 1]
            o = o - lam_val * (a1[:, :HEAD_DIM] / a1[:, HEAD_DIM:])
            o = o * lax.rsqrt(jnp.mean(o * o, axis=-1, keepdims=True) + EPS) * gs_ref[...]
            o = o * (1.0 - lam_init)
        o_ref[0, :, h * HEAD_DIM:(h + 1) * HEAD_DIM] = o.astype(o_ref.dtype)


def _attention(main, mode, *, mask=None, tab=None, lam=None, g_subln=None,
               fcol=None, frow=None, lam_init=0.0):
    Bn, S, _ = main.shape
    blk_q, blk_k, blk_v, head_off, tq_pref = {
        "dsa": (BLK_QA, BLK_KA, BLK_VA, 0, 256),
        "diff": (BLK_QB, BLK_KB, BLK_VB, N_HEADS, 512),
        "fox": (BLK_QD, BLK_KD, BLK_VD, 0, 512),
    }[mode]
    tq = _tile(S, tq_pref)
    nmap = 2 if mode == "diff" else 1
    W = N_HEADS * HEAD_DIM
    args = [main, main, main]
    specs = [pl.BlockSpec((1, tq, W), lambda b, i: (b, i, blk_q // N_HEADS)),
             pl.BlockSpec((1, S, W), lambda b, i: (b, 0, blk_k // N_HEADS)),
             pl.BlockSpec((1, S, W), lambda b, i: (b, 0, blk_v // N_HEADS))]
    scratch = [pltpu.VMEM((N_HEADS * nmap, tq, LANE), jnp.float32),
               pltpu.VMEM((N_HEADS * nmap, tq, 2 * HEAD_DIM), jnp.float32)]
    if mode == "dsa":
        args.append(mask)
        specs.append(pl.BlockSpec((1, tq, S), lambda b, i: (b, i, 0)))
    if mode in ("dsa", "diff"):
        args.append(tab)
        specs.append(pl.BlockSpec(memory_space=pltpu.SMEM))
        scratch.append(pltpu.VMEM((N_HEADS, tq, 2 * tq), jnp.float32))
    if mode == "diff":
        args += [lam, g_subln.reshape(1, HEAD_DIM)]
        specs += [pl.BlockSpec((4, DIFF_DIM), lambda b, i: (0, 0)),
                  pl.BlockSpec((1, HEAD_DIM), lambda b, i: (0, 0))]
    if mode == "fox":
        args += [fcol, frow]
        specs += [pl.BlockSpec((1, tq, LANE), lambda b, i: (b, i, 0)),
                  pl.BlockSpec((1, N_HEADS, S), lambda b, i: (b, 0, 0))]
    return pl.pallas_call(
        functools.partial(_attn_kernel, mode=mode, head_off=head_off, lam_init=lam_init),
        grid=(Bn, S // tq),
        in_specs=specs,
        out_specs=pl.BlockSpec((1, tq, W), lambda b, i: (b, i, 0)),
        out_shape=jax.ShapeDtypeStruct((Bn, S, MIX_W), jnp.bfloat16),
        scratch_shapes=scratch,
        name="attn_" + mode,
        compiler_params=_cparams(("parallel", "arbitrary")),
    )(*args)


def _conv_kernel(bg_ref, cg_ref, xin_ref, w_ref, o_ref):
    u = cg_ref[0] * xin_ref[0]
    row = lax.broadcasted_iota(jnp.int32, u.shape, 0)
    w = w_ref[...]
    y = u * w[CONV_K - 1:CONV_K, :]
    for d in range(1, CONV_K):
        shifted = jnp.where(row >= d, pltpu.roll(u, d, 0), 0.0)
        y = y + shifted * w[CONV_K - 1 - d:CONV_K - d, :]
    o_ref[0] = (bg_ref[0] * y).astype(o_ref.dtype)


def _short_conv(f32grp, conv_w):
    Bn, S, _ = f32grp.shape
    tc = LANE
    per = CONV_CH // tc
    return pl.pallas_call(
        _conv_kernel,
        grid=(Bn, per),
        in_specs=[pl.BlockSpec((1, S, tc), lambda b, j: (b, 0, j)),
                  pl.BlockSpec((1, S, tc), lambda b, j: (b, 0, per + j)),
                  pl.BlockSpec((1, S, tc), lambda b, j: (b, 0, 2 * per + j)),
                  pl.BlockSpec((CONV_K, tc), lambda b, j: (0, j))],
        out_specs=pl.BlockSpec((1, S, tc), lambda b, j: (b, 0, j)),
        out_shape=jax.ShapeDtypeStruct((Bn, S, CONV_CH), jnp.bfloat16),
        name="short_conv",
        compiler_params=_cparams(("parallel", "parallel")),
    )(f32grp, f32grp, f32grp, conv_w)


def _mix_kernel(*refs):
    z_refs = refs[0:N_BRANCH]
    y_refs = refs[N_BRANCH:2 * N_BRANCH]
    wb_refs = refs[2 * N_BRANCH:3 * N_BRANCH]
    o_ref = refs[3 * N_BRANCH]
    merged = None
    for g in range(N_BRANCH):
        gate = jax.nn.sigmoid(z_refs[g][...].astype(jnp.float32))
        proj = jnp.dot(y_refs[g][...], wb_refs[g][...], preferred_element_type=jnp.float32)
        merged = gate * proj if merged is None else merged + gate * proj
    o_ref[...] = merged.astype(o_ref.dtype)


def _mix(z, ys, w_branch, l):
    T = z.shape[0]
    D = w_branch.shape[3]
    tm = _tile(T, 1024)
    tn = _tile(D, 1024)
    per = D // tn
    specs = [pl.BlockSpec((tm, tn), lambda i, n, g=g: (i, g * per + n)) for g in range(N_BRANCH)]
    specs += [pl.BlockSpec((tm, MIX_W), lambda i, n: (i, 0)) for _ in range(N_BRANCH)]
    specs += [pl.BlockSpec((None, None, MIX_W, tn), lambda i, n, g=g: (l, g, 0, n))
              for g in range(N_BRANCH)]
    return pl.pallas_call(
        _mix_kernel,
        grid=(T // tm, per),
        in_specs=specs,
        out_specs=pl.BlockSpec((tm, tn), lambda i, n: (i, n)),
        out_shape=jax.ShapeDtypeStruct((T, D), jnp.bfloat16),
        name="gated_mix",
        compiler_params=_cparams(("parallel", "parallel")),
    )(*([z] * N_BRANCH), *ys, *([w_branch] * N_BRANCH))


def _ffn_kernel(h_ref, w1_ref, w2_ref, o_ref, u_ref, *, nf):
    f = pl.program_id(1)

    def first(slot):
        u = jnp.dot(h_ref[...], w1_ref[...], preferred_element_type=jnp.float32)
        u_ref[slot] = jnp.square(jnp.maximum(u, 0.0)).astype(u_ref.dtype)

    def second(slot):
        return jnp.dot(u_ref[slot], w2_ref[...], preferred_element_type=jnp.float32)

    @pl.when(f == 0)
    def _():
        o_ref[...] = jnp.zeros(o_ref.shape, o_ref.dtype)
        first(0)

    for parity in range(2):
        @pl.when((f > 0) & (f < nf) & (f % 2 == parity))
        def _(parity=parity):
            part = second(1 - parity)
            first(parity)
            o_ref[...] += part

    @pl.when(f == nf)
    def _():
        o_ref[...] += second((nf - 1) % 2)


def _ffn(h, w1, w2, l):
    T, D = h.shape
    F = w1.shape[2]
    tm = _tile(T, 1024)
    tf = _tile(F, 1024)
    nf = F // tf
    row = pl.BlockSpec((tm, D), lambda i, f: (i, 0))
    return pl.pallas_call(
        functools.partial(_ffn_kernel, nf=nf),
        grid=(T // tm, nf + 1),
        in_specs=[row,
                  pl.BlockSpec((None, D, tf), lambda i, f: (l, 0, jnp.minimum(f, nf - 1))),
                  pl.BlockSpec((None, tf, D), lambda i, f: (l, jnp.maximum(f - 1, 0), 0))],
        out_specs=row,
        out_shape=jax.ShapeDtypeStruct((T, D), jnp.float32),
        scratch_shapes=[pltpu.VMEM((2, tm, tf), jnp.bfloat16)],
        name="ffn",
        compiler_params=_cparams(("parallel", "arbitrary")),
    )(h, w1, w2)


def _split_in_proj(w_in):
    W = N_HEADS * HEAD_DIM
    sizes = (3 * W,
             IDX_HEADS * IDX_DIM + IDX_DIM,
             IDX_HEADS,
             3 * W,
             3 * CONV_CH,
             3 * W,
             N_HEADS)
    offs = np.concatenate([[0], np.cumsum(sizes)])
    a, idx, iw, b, cv, d, fd = [w_in[:, :, offs[j]:offs[j + 1]] for j in range(7)]
    L, D, _ = w_in.shape
    zeros = lambda n: jnp.zeros((L, D, n), w_in.dtype)
    main = jnp.concatenate([a, b, d, idx, zeros(N_MAIN - 9 * W - sizes[1])], axis=-1)
    f32grp = jnp.concatenate([cv, iw, fd, zeros(LANE - IDX_HEADS - N_HEADS)], axis=-1)
    return main.astype(jnp.bfloat16), f32grp.astype(jnp.bfloat16)


def kernel(x, c, w_ada, b_ada, g_norm1, w_in, b_f, conv_w, lam, g_subln, rel_bias,
           w_gate, b_gate, w_branch, w_o, g_norm2, w_ff1, w_ff2, g_final):
    Bn, S, D = x.shape
    L = w_ada.shape[0]
    T = Bn * S
    k_top = min(TOPK_MAX, S // 4)
    bf = jnp.bfloat16

    w_main, w_f32 = _split_in_proj(w_in)
    w_gate_b, w_branch_b, w_o_b = w_gate.astype(bf), w_branch.astype(bf), w_o.astype(bf)
    w_ff1_b, w_ff2_b = w_ff1.astype(bf), w_ff2.astype(bf)

    mod = _modulation(c, w_ada, b_ada).reshape(L, Bn, 6, 1, D)
    xt = x.reshape(T, D)
    delta, gate = None, None
    for l in range(L):
        sh1, sc1, gt1, sh2, sc2, gt2 = [mod[l, :, j] for j in range(6)]
        x_new, h = _norm(xt, g_norm1[l], sc1, sh1, seq=S, delta=delta, gate=gate)
        xt = xt if x_new is None else x_new
        main = _mm(h, w_main, l, bf, 2560).reshape(Bn, S, N_MAIN)
        f32grp = _mm(h, w_f32, l, jnp.float32, N_F32).reshape(Bn, S, N_F32)

        b_pad = jnp.zeros((1, LANE), jnp.float32).at[0, COL_FD:COL_FD + N_HEADS].set(b_f[l])
        fcol = _forget_cumsum(f32grp, b_pad)
        frow = jnp.transpose(fcol[:, :, COL_FD:COL_FD + N_HEADS], (0, 2, 1))

        mask, z = _select_and_gates(main, f32grp, h, w_gate_b, b_gate.reshape(L, 1, N_BRANCH * D),
                                    l, k_top, _tile(S, 512))
        lam_init = 0.8 - 0.6 * math.exp(-0.3 * l)
        y_a = _attention(main, "dsa", mask=mask, tab=rel_bias)
        y_b = _attention(main, "diff", tab=rel_bias, lam=lam[l], g_subln=g_subln[l],
                         lam_init=lam_init)
        y_c = _short_conv(f32grp, conv_w[l])
        y_d = _attention(main, "fox", fcol=fcol, frow=frow)
        ys = [y.reshape(T, MIX_W) for y in (y_a, y_b, y_c, y_d)]

        merged = _mix(z, ys, w_branch_b, l)
        xt, h2 = _mm_norm(merged, w_o_b, l, xt, gt1, g_norm2[l], sc2, sh2, seq=S)
        delta = _ffn(h2, w_ff1_b, w_ff2_b, l)
        gate = gt2
    out = _norm(xt, g_final, seq=S, delta=delta, gate=gate)
    return out.reshape(Bn, S, D)
```
